```python
import math
import jax
import jax.numpy as jnp
from jax import lax
import numpy as np

D_MODEL = 2048
BATCH = 8
SEQ = 8192
DEPTH = 4

GRID_W = 64
CTX_LEN = 256
N_BRANCH = 4
D_BRANCH = D_MODEL // N_BRANCH
CHUNK = 128
GMLP_GROUPS = 4
GMLP_GW = D_BRANCH // GMLP_GROUPS
CONV_W = 31
HEAD_DIM = 64
N_Q_HEADS = D_BRANCH // HEAD_DIM
N_KV_HEADS = 2
Q_PER_KV = N_Q_HEADS // N_KV_HEADS
WINDOW = 128
BLOCK = 128
ROPE_BASE = 10000.0
S5_GW = 16
S5_GROUPS = D_BRANCH // S5_GW
S5_STATE = 64
D_FF = 4 * D_MODEL
N_MOD = 6
EPS = 1e-6
NEG_INF = -1e30
COLS_A = 2 * D_BRANCH
COLS_B = 2 * D_BRANCH
COLS_Q = N_Q_HEADS * HEAD_DIM
COLS_KV = N_KV_HEADS * HEAD_DIM
COLS_D = D_BRANCH
IN_COLS = COLS_A + COLS_B + COLS_Q + 2 * COLS_KV + COLS_D

kernel_name = 'hybrid_parallel_gmlp_conformer_swa_s5_dit'


def rms_norm(x, g):
    xf = x.astype(jnp.float32)
    y = xf * lax.rsqrt(jnp.mean(xf * xf, axis=-1, keepdims=True) + EPS)
    return (y * g.astype(jnp.float32)).astype(x.dtype)


def layer_norm(x, g, b):
    xf = x.astype(jnp.float32)
    xc = xf - jnp.mean(xf, axis=-1, keepdims=True)
    var = jnp.mean(xc * xc, axis=-1, keepdims=True)
    return (xc * lax.rsqrt(var + EPS) * g.astype(jnp.float32) + b.astype(jnp.float32)).astype(x.dtype)


def rope_1d(x, pos):
    d = x.shape[-1]
    inv = ROPE_BASE ** (-jnp.arange(0, d, 2, dtype=jnp.float32) / d)
    ang = pos.astype(jnp.float32)[:, None] * inv[None, :]
    cos = jnp.cos(ang)[:, None, :]
    sin = jnp.sin(ang)[:, None, :]
    xf = x.astype(jnp.float32)
    x1, x2 = xf[..., : d // 2], xf[..., d // 2:]
    return jnp.concatenate([x1 * cos - x2 * sin, x1 * sin + x2 * cos], axis=-1).astype(x.dtype)


def axial_rope(x, row, col):
    half = HEAD_DIM // 2
    return jnp.concatenate([rope_1d(x[..., :half], row), rope_1d(x[..., half:], col)], axis=-1)


def split_in(z):
    parts = []
    start = 0
    for width in (COLS_A, COLS_B, COLS_Q, COLS_KV, COLS_KV, COLS_D):
        parts.append(z[..., start:start + width])
        start += width
    return parts


def gmlp_chunk_mix(za, ln_g, ln_b, w_s, b_s):
    za = jax.nn.gelu(za)
    u, v = jnp.split(za, 2, axis=-1)
    v = layer_norm(v, ln_g, ln_b)
    B, L, _ = v.shape
    vb = v.reshape(B, L // CHUNK, CHUNK, GMLP_GROUPS, GMLP_GW)
    mixed = jnp.einsum('gpq,bnqgc->bnpgc', w_s, vb) + b_s.T[None, None, :, :, None]
    return u * mixed.reshape(B, L, D_BRANCH)


def conformer_conv(zb, w_dw, b_dw, ln_g, ln_b):
    a, g = jnp.split(zb, 2, axis=-1)
    y = a * jax.nn.sigmoid(g)
    y = lax.conv_general_dilated(
        y, w_dw[:, None, :], window_strides=(1,), padding=[(CONV_W // 2, CONV_W // 2)],
        dimension_numbers=('NWC', 'WIO', 'NWC'), feature_group_count=D_BRANCH) + b_dw
    return jax.nn.silu(layer_norm(y, ln_g, ln_b))


def windowed_attention(q, k, v, kc, vc, sink):
    B, S = q.shape[:2]
    nb = S // BLOCK
    scale = HEAD_DIM ** -0.5
    qb = q.reshape(B, nb, BLOCK, N_KV_HEADS, Q_PER_KV, HEAD_DIM)

    def band(t):
        tp = jnp.pad(t, ((0, 0), (BLOCK, BLOCK), (0, 0), (0, 0)))
        tp = tp.reshape(B, nb + 2, BLOCK, N_KV_HEADS, HEAD_DIM)
        return jnp.concatenate([tp[:, :-2], tp[:, 1:-1], tp[:, 2:]], axis=2)

    kb, vb = band(k), band(v)
    blk = jnp.arange(nb)[:, None, None]
    qpos = blk * BLOCK + jnp.arange(BLOCK)[None, :, None]
    kpos = (blk - 1) * BLOCK + jnp.arange(3 * BLOCK)[None, None, :]
    allowed = (jnp.abs(qpos - kpos) <= WINDOW) & (kpos >= 0) & (kpos < S)
    s_loc = jnp.einsum('bnqhgd,bnkhd->bnhgqk', qb, kb, preferred_element_type=jnp.float32) * scale
    s_loc = jnp.where(allowed[None, :, None, None], s_loc, NEG_INF)
    s_ctx = jnp.einsum('bnqhgd,bchd->bnhgqc', qb, kc, preferred_element_type=jnp.float32) * scale
    sink_col = jnp.broadcast_to(sink.astype(jnp.float32).reshape(1, 1, N_KV_HEADS, Q_PER_KV, 1, 1),
                                s_loc.shape[:-1] + (1,))
    p = jax.nn.softmax(jnp.concatenate([s_loc, s_ctx, sink_col], axis=-1), axis=-1)
    n_loc = 3 * BLOCK
    n_ctx = kc.shape[1]
    p_loc = p[..., :n_loc].astype(v.dtype)
    p_ctx = p[..., n_loc:n_loc + n_ctx].astype(v.dtype)
    o = (jnp.einsum('bnhgqk,bnkhd->bnqhgd', p_loc, vb)
         + jnp.einsum('bnhgqc,bchd->bnqhgd', p_ctx, vc))
    return o.reshape(B, S, N_Q_HEADS * HEAD_DIM)


def context_attention(qc, kc, vc, sink):
    B, C = qc.shape[:2]
    scale = HEAD_DIM ** -0.5
    qg = qc.reshape(B, C, N_KV_HEADS, Q_PER_KV, HEAD_DIM)
    s = jnp.einsum('bqhgd,bkhd->bhgqk', qg, kc, preferred_element_type=jnp.float32) * scale
    sink_col = jnp.broadcast_to(sink.astype(jnp.float32).reshape(1, N_KV_HEADS, Q_PER_KV, 1, 1),
                                s.shape[:-1] + (1,))
    p = jax.nn.softmax(jnp.concatenate([s, sink_col], axis=-1), axis=-1)
    o = jnp.einsum('bhgqk,bkhd->bqhgd', p[..., :C].astype(vc.dtype), vc)
    return o.reshape(B, C, N_Q_HEADS * HEAD_DIM)


def s5_discretise(a_re, a_im, log_step, b_re, b_im):
    lam = lax.complex(a_re.astype(jnp.float32), a_im.astype(jnp.float32))
    dt = jnp.exp(log_step.astype(jnp.float32))[:, None]
    log_lbar = lam * dt
    lbar = jnp.exp(log_lbar)
    b = lax.complex(b_re.astype(jnp.float32), b_im.astype(jnp.float32))
    bbar = ((lbar - 1.0) / lam)[..., None] * b
    return log_lbar, lbar, bbar


def _ssm_combine(left, right):
    a1, b1 = left
    a2, b2 = right
    return a1 * a2, a2 * b1 + b2


def s5_scan(u, disc, s0):
    log_lbar, lbar, bbar = disc
    bu = jnp.einsum('blgc,gpc->blgp', u.astype(jnp.float32).astype(jnp.complex64), bbar)
    a = jnp.broadcast_to(lbar, bu.shape)
    _, s = lax.associative_scan(_ssm_combine, (a, bu), axis=1)
    if s0 is not None:
        steps = jnp.arange(1, u.shape[1] + 1, dtype=jnp.float32)
        s = s + jnp.exp(log_lbar[None] * steps[:, None, None])[None] * s0[:, None]
    return s


def s5_readout(u, s_f, s_b, c_re, c_im, d_skip, w_glu):
    cf = lax.complex(c_re[0].astype(jnp.float32), c_im[0].astype(jnp.float32))
    cb = lax.complex(c_re[1].astype(jnp.float32), c_im[1].astype(jnp.float32))
    y = (jnp.einsum('gcp,blgp->blgc', cf, s_f) + jnp.einsum('gcp,blgp->blgc', cb, s_b)).real
    B, L = u.shape[:2]
    y = y.reshape(B, L, D_BRANCH) + d_skip.astype(jnp.float32) * u.reshape(B, L, D_BRANCH).astype(jnp.float32)
    y = jax.nn.gelu(y).astype(u.dtype)
    a, g = jnp.split(y @ w_glu, 2, axis=-1)
    return a * jax.nn.sigmoid(g)


def merge_branches(h, branches, w_br, w_gate, b_gate, w_out):
    merged = None
    for kb, br in enumerate(branches):
        term = jax.nn.sigmoid(h @ w_gate[kb] + b_gate[kb]) * (br @ w_br[kb])
        merged = term if merged is None else merged + term
    return merged @ w_out


def sq_relu_mlp(h, w1, w2):
    return jnp.square(jax.nn.relu(h @ w1)) @ w2


def _fwd_setup_inputs(seed: int = 0) -> dict:
    key = jax.random.key(seed)
    keys = jax.random.split(key, 40)

    def nrm(i, shape, s):
        return jax.random.normal(keys[i], shape, jnp.float32) * s

    L, D, G, P = DEPTH, D_MODEL, S5_GROUPS, S5_STATE
    n_idx = jnp.arange(P, dtype=jnp.float32)
    return {
        'x': nrm(0, (BATCH, SEQ, D), 1.0),
        'c': nrm(1, (BATCH, D), 1.0),
        'ctx': nrm(2, (BATCH, CTX_LEN, D), 1.0),
        'c_ctx': nrm(3, (D,), 1.0),
        'w_mod': nrm(4, (L, D, N_MOD * D), 0.5 * D ** -0.5),
        'b_mod': nrm(5, (L, N_MOD * D), 0.02),
        'norm1_g': 1.0 + nrm(6, (L, D), 0.02),
        'norm2_g': 1.0 + nrm(7, (L, D), 0.02),
        'w_in': nrm(8, (L, D, IN_COLS), D ** -0.5),
        'gmlp_ln_g': 1.0 + nrm(9, (L, D_BRANCH), 0.02),
        'gmlp_ln_b': nrm(10, (L, D_BRANCH), 0.02),
        'gmlp_ws': nrm(11, (L, GMLP_GROUPS, CHUNK, CHUNK), CHUNK ** -0.5),
        'gmlp_bs': 1.0 + nrm(12, (L, GMLP_GROUPS, CHUNK), 0.02),
        'conv_w': nrm(13, (L, CONV_W, D_BRANCH), CONV_W ** -0.5),
        'conv_b': nrm(14, (L, D_BRANCH), 0.02),
        'conv_ln_g': 1.0 + nrm(15, (L, D_BRANCH), 0.02),
        'conv_ln_b': nrm(16, (L, D_BRANCH), 0.02),
        'attn_sink': nrm(17, (L, N_Q_HEADS), 0.5),
        's5_a_re': -0.5 + nrm(18, (L, 2, G, P), 0.01),
        's5_a_im': math.pi * n_idx + nrm(19, (L, 2, G, P), 0.01),
        's5_log_step': jax.random.uniform(keys[20], (L, 2, G), jnp.float32, math.log(1e-3), math.log(1e-1)),
        's5_b_re': nrm(21, (L, G, P, S5_GW), (2 * S5_GW) ** -0.5),
        's5_b_im': nrm(22, (L, G, P, S5_GW), (2 * S5_GW) ** -0.5),
        's5_c_re': nrm(23, (L, 2, G, S5_GW, P), 0.25),
        's5_c_im': nrm(24, (L, 2, G, S5_GW, P), 0.25),
        's5_d': nrm(25, (L, D_BRANCH), 0.5),
        's5_w_glu': nrm(26, (L, D_BRANCH, 2 * D_BRANCH), D_BRANCH ** -0.5),
        'w_branch': nrm(27, (L, N_BRANCH, D_BRANCH, D), D_BRANCH ** -0.5),
        'w_gate': nrm(28, (L, N_BRANCH, D, D), D ** -0.5),
        'b_gate': nrm(29, (L, N_BRANCH, D), 0.02),
        'w_out': nrm(30, (L, D, D), D ** -0.5),
        'w_ff1': nrm(31, (L, D, D_FF), D ** -0.5),
        'w_ff2': nrm(32, (L, D_FF, D), D_FF ** -0.5),
        'final_g': 1.0 + nrm(33, (D,), 0.02),
    }


def _fwd_reference(x, c, ctx, c_ctx, w_mod, b_mod, norm1_g, norm2_g, w_in,
              gmlp_ln_g, gmlp_ln_b, gmlp_ws, gmlp_bs,
              conv_w, conv_b, conv_ln_g, conv_ln_b,
              attn_sink,
              s5_a_re, s5_a_im, s5_log_step, s5_b_re, s5_b_im, s5_c_re, s5_c_im, s5_d, s5_w_glu,
              w_branch, w_gate, b_gate, w_out, w_ff1, w_ff2, final_g):
    B, S, _ = x.shape
    n_ctx = ctx.shape[1]
    rows = S // GRID_W
    row = jnp.repeat(jnp.arange(rows), GRID_W)
    col = jnp.tile(jnp.arange(GRID_W), rows)
    cond_x = jax.nn.silu(c)
    cond_c = jax.nn.silu(c_ctx)
    for l in range(DEPTH):
        mod_x = (cond_x @ w_mod[l] + b_mod[l])[:, None, :]
        mod_c = cond_c @ w_mod[l] + b_mod[l]
        shift1, scale1, gate1, shift2, scale2, gate2 = jnp.split(mod_x, N_MOD, axis=-1)
        cshift1, cscale1, cgate1, cshift2, cscale2, cgate2 = jnp.split(mod_c, N_MOD, axis=-1)
        disc_f = s5_discretise(s5_a_re[l, 0], s5_a_im[l, 0], s5_log_step[l, 0], s5_b_re[l], s5_b_im[l])
        disc_b = s5_discretise(s5_a_re[l, 1], s5_a_im[l, 1], s5_log_step[l, 1], s5_b_re[l], s5_b_im[l])

        hc = rms_norm(ctx, norm1_g[l]) * (1.0 + cscale1) + cshift1
        zc_a, zc_b, qc, kc, vc, zc_d = split_in(hc @ w_in[l])
        kc = kc.reshape(B, n_ctx, N_KV_HEADS, HEAD_DIM)
        vc = vc.reshape(B, n_ctx, N_KV_HEADS, HEAD_DIM)
        uc = zc_d.reshape(B, n_ctx, S5_GROUPS, S5_GW)
        sc_f = s5_scan(uc, disc_f, None)
        sc_b_rev = s5_scan(jnp.flip(uc, 1), disc_b, None)

        h = rms_norm(x, norm1_g[l]) * (1.0 + scale1) + shift1
        z_a, z_b, q, k, v, z_d = split_in(h @ w_in[l])
        q = axial_rope(q.reshape(B, S, N_Q_HEADS, HEAD_DIM), row, col)
        k = axial_rope(k.reshape(B, S, N_KV_HEADS, HEAD_DIM), row, col)
        v = v.reshape(B, S, N_KV_HEADS, HEAD_DIM)
        u = z_d.reshape(B, S, S5_GROUPS, S5_GW)
        s_f = s5_scan(u, disc_f, sc_f[:, -1])
        s_b = jnp.flip(s5_scan(jnp.flip(u, 1), disc_b, sc_b_rev[:, -1]), 1)
        branches = (
            gmlp_chunk_mix(z_a, gmlp_ln_g[l], gmlp_ln_b[l], gmlp_ws[l], gmlp_bs[l]),
            conformer_conv(z_b, conv_w[l], conv_b[l], conv_ln_g[l], conv_ln_b[l]),
            windowed_attention(q, k, v, kc, vc, attn_sink[l]),
            s5_readout(u, s_f, s_b, s5_c_re[l], s5_c_im[l], s5_d[l], s5_w_glu[l]),
        )
        x = x + gate1 * merge_branches(h, branches, w_branch[l], w_gate[l], b_gate[l], w_out[l])
        h2 = rms_norm(x, norm2_g[l]) * (1.0 + scale2) + shift2
        x = x + gate2 * sq_relu_mlp(h2, w_ff1[l], w_ff2[l])

        if l < DEPTH - 1:
            branches_c = (
                gmlp_chunk_mix(zc_a, gmlp_ln_g[l], gmlp_ln_b[l], gmlp_ws[l], gmlp_bs[l]),
                conformer_conv(zc_b, conv_w[l], conv_b[l], conv_ln_g[l], conv_ln_b[l]),
                context_attention(qc, kc, vc, attn_sink[l]),
                s5_readout(uc, sc_f, jnp.flip(sc_b_rev, 1), s5_c_re[l], s5_c_im[l], s5_d[l], s5_w_glu[l]),
            )
            ctx = ctx + cgate1 * merge_branches(hc, branches_c, w_branch[l], w_gate[l], b_gate[l], w_out[l])
            hc2 = rms_norm(ctx, norm2_g[l]) * (1.0 + cscale2) + cshift2
            ctx = ctx + cgate2 * sq_relu_mlp(hc2, w_ff1[l], w_ff2[l])
    return rms_norm(x, final_g)


import jax as _jax
import jax.numpy as _jnp

TWIN_FORMAT = 'train_step'
FWD_PARAMS = ['x', 'c', 'ctx', 'c_ctx', 'w_mod', 'b_mod', 'norm1_g', 'norm2_g', 'w_in', 'gmlp_ln_g', 'gmlp_ln_b', 'gmlp_ws', 'gmlp_bs', 'conv_w', 'conv_b', 'conv_ln_g', 'conv_ln_b', 'attn_sink', 's5_a_re', 's5_a_im', 's5_log_step', 's5_b_re', 's5_b_im', 's5_c_re', 's5_c_im', 's5_d', 's5_w_glu', 'w_branch', 'w_gate', 'b_gate', 'w_out', 'w_ff1', 'w_ff2', 'final_g']
TWIN_WEIGHTS = ['c_ctx', 'w_mod', 'b_mod', 'norm1_g', 'norm2_g', 'w_in', 'gmlp_ln_g', 'gmlp_ln_b', 'gmlp_ws', 'gmlp_bs', 'conv_w', 'conv_b', 'conv_ln_g', 'conv_ln_b', 'attn_sink', 's5_a_re', 's5_a_im', 's5_log_step', 's5_b_re', 's5_b_im', 's5_c_re', 's5_c_im', 's5_d', 's5_w_glu', 'w_branch', 'w_gate', 'b_gate', 'w_out', 'w_ff1', 'w_ff2', 'final_g']
TWIN_DIFF_INPUT = 'x'
TWIN_INPUTS = ['x', 'c', 'ctx', 'c_ctx', 'w_mod', 'b_mod', 'norm1_g', 'norm2_g', 'w_in', 'gmlp_ln_g', 'gmlp_ln_b', 'gmlp_ws', 'gmlp_bs', 'conv_w', 'conv_b', 'conv_ln_g', 'conv_ln_b', 'attn_sink', 's5_a_re', 's5_a_im', 's5_log_step', 's5_b_re', 's5_b_im', 's5_c_re', 's5_c_im', 's5_d', 's5_w_glu', 'w_branch', 'w_gate', 'b_gate', 'w_out', 'w_ff1', 'w_ff2', 'final_g', 'loss_target', 'm_c_ctx', 'm_w_mod', 'm_b_mod', 'm_norm1_g', 'm_norm2_g', 'm_w_in', 'm_gmlp_ln_g', 'm_gmlp_ln_b', 'm_gmlp_ws', 'm_gmlp_bs', 'm_conv_w', 'm_conv_b', 'm_conv_ln_g', 'm_conv_ln_b', 'm_attn_sink', 'm_s5_a_re', 'm_s5_a_im', 'm_s5_log_step', 'm_s5_b_re', 'm_s5_b_im', 'm_s5_c_re', 'm_s5_c_im', 'm_s5_d', 'm_s5_w_glu', 'm_w_branch', 'm_w_gate', 'm_b_gate', 'm_w_out', 'm_w_ff1', 'm_w_ff2', 'm_final_g', 'v_c_ctx', 'v_w_mod', 'v_b_mod', 'v_norm1_g', 'v_norm2_g', 'v_w_in', 'v_gmlp_ln_g', 'v_gmlp_ln_b', 'v_gmlp_ws', 'v_gmlp_bs', 'v_conv_w', 'v_conv_b', 'v_conv_ln_g', 'v_conv_ln_b', 'v_attn_sink', 'v_s5_a_re', 'v_s5_a_im', 'v_s5_log_step', 'v_s5_b_re', 'v_s5_b_im', 'v_s5_c_re', 'v_s5_c_im', 'v_s5_d', 'v_s5_w_glu', 'v_w_branch', 'v_w_gate', 'v_b_gate', 'v_w_out', 'v_w_ff1', 'v_w_ff2', 'v_final_g']
TWIN_OUTPUTS = ['loss', 'grad_x', 'grad_c_ctx', 'grad_w_mod', 'grad_b_mod', 'grad_norm1_g', 'grad_norm2_g', 'grad_w_in', 'grad_gmlp_ln_g', 'grad_gmlp_ln_b', 'grad_gmlp_ws', 'grad_gmlp_bs', 'grad_conv_w', 'grad_conv_b', 'grad_conv_ln_g', 'grad_conv_ln_b', 'grad_attn_sink', 'grad_s5_a_re', 'grad_s5_a_im', 'grad_s5_log_step', 'grad_s5_b_re', 'grad_s5_b_im', 'grad_s5_c_re', 'grad_s5_c_im', 'grad_s5_d', 'grad_s5_w_glu', 'grad_w_branch', 'grad_w_gate', 'grad_b_gate', 'grad_w_out', 'grad_w_ff1', 'grad_w_ff2', 'grad_final_g', 'delta_c_ctx', 'delta_w_mod', 'delta_b_mod', 'delta_norm1_g', 'delta_norm2_g', 'delta_w_in', 'delta_gmlp_ln_g', 'delta_gmlp_ln_b', 'delta_gmlp_ws', 'delta_gmlp_bs', 'delta_conv_w', 'delta_conv_b', 'delta_conv_ln_g', 'delta_conv_ln_b', 'delta_attn_sink', 'delta_s5_a_re', 'delta_s5_a_im', 'delta_s5_log_step', 'delta_s5_b_re', 'delta_s5_b_im', 'delta_s5_c_re', 'delta_s5_c_im', 'delta_s5_d', 'delta_s5_w_glu', 'delta_w_branch', 'delta_w_gate', 'delta_b_gate', 'delta_w_out', 'delta_w_ff1', 'delta_w_ff2', 'delta_final_g', 'new_m_c_ctx', 'new_m_w_mod', 'new_m_b_mod', 'new_m_norm1_g', 'new_m_norm2_g', 'new_m_w_in', 'new_m_gmlp_ln_g', 'new_m_gmlp_ln_b', 'new_m_gmlp_ws', 'new_m_gmlp_bs', 'new_m_conv_w', 'new_m_conv_b', 'new_m_conv_ln_g', 'new_m_conv_ln_b', 'new_m_attn_sink', 'new_m_s5_a_re', 'new_m_s5_a_im', 'new_m_s5_log_step', 'new_m_s5_b_re', 'new_m_s5_b_im', 'new_m_s5_c_re', 'new_m_s5_c_im', 'new_m_s5_d', 'new_m_s5_w_glu', 'new_m_w_branch', 'new_m_w_gate', 'new_m_b_gate', 'new_m_w_out', 'new_m_w_ff1', 'new_m_w_ff2', 'new_m_final_g', 'new_v_c_ctx', 'new_v_w_mod', 'new_v_b_mod', 'new_v_norm1_g', 'new_v_norm2_g', 'new_v_w_in', 'new_v_gmlp_ln_g', 'new_v_gmlp_ln_b', 'new_v_gmlp_ws', 'new_v_gmlp_bs', 'new_v_conv_w', 'new_v_conv_b', 'new_v_conv_ln_g', 'new_v_conv_ln_b', 'new_v_attn_sink', 'new_v_s5_a_re', 'new_v_s5_a_im', 'new_v_s5_log_step', 'new_v_s5_b_re', 'new_v_s5_b_im', 'new_v_s5_c_re', 'new_v_s5_c_im', 'new_v_s5_d', 'new_v_s5_w_glu', 'new_v_w_branch', 'new_v_w_gate', 'new_v_b_gate', 'new_v_w_out', 'new_v_w_ff1', 'new_v_w_ff2', 'new_v_final_g']
TWIN_LEAF_KINDS = {'loss': 'loss', 'grad_x': 'grad_x', 'grad_c_ctx': 'grad_w', 'grad_w_mod': 'grad_w', 'grad_b_mod': 'grad_w', 'grad_norm1_g': 'grad_w', 'grad_norm2_g': 'grad_w', 'grad_w_in': 'grad_w', 'grad_gmlp_ln_g': 'grad_w', 'grad_gmlp_ln_b': 'grad_w', 'grad_gmlp_ws': 'grad_w', 'grad_gmlp_bs': 'grad_w', 'grad_conv_w': 'grad_w', 'grad_conv_b': 'grad_w', 'grad_conv_ln_g': 'grad_w', 'grad_conv_ln_b': 'grad_w', 'grad_attn_sink': 'grad_w', 'grad_s5_a_re': 'grad_w', 'grad_s5_a_im': 'grad_w', 'grad_s5_log_step': 'grad_w', 'grad_s5_b_re': 'grad_w', 'grad_s5_b_im': 'grad_w', 'grad_s5_c_re': 'grad_w', 'grad_s5_c_im': 'grad_w', 'grad_s5_d': 'grad_w', 'grad_s5_w_glu': 'grad_w', 'grad_w_branch': 'grad_w', 'grad_w_gate': 'grad_w', 'grad_b_gate': 'grad_w', 'grad_w_out': 'grad_w', 'grad_w_ff1': 'grad_w', 'grad_w_ff2': 'grad_w', 'grad_final_g': 'grad_w', 'delta_c_ctx': 'delta_w', 'delta_w_mod': 'delta_w', 'delta_b_mod': 'delta_w', 'delta_norm1_g': 'delta_w', 'delta_norm2_g': 'delta_w', 'delta_w_in': 'delta_w', 'delta_gmlp_ln_g': 'delta_w', 'delta_gmlp_ln_b': 'delta_w', 'delta_gmlp_ws': 'delta_w', 'delta_gmlp_bs': 'delta_w', 'delta_conv_w': 'delta_w', 'delta_conv_b': 'delta_w', 'delta_conv_ln_g': 'delta_w', 'delta_conv_ln_b': 'delta_w', 'delta_attn_sink': 'delta_w', 'delta_s5_a_re': 'delta_w', 'delta_s5_a_im': 'delta_w', 'delta_s5_log_step': 'delta_w', 'delta_s5_b_re': 'delta_w', 'delta_s5_b_im': 'delta_w', 'delta_s5_c_re': 'delta_w', 'delta_s5_c_im': 'delta_w', 'delta_s5_d': 'delta_w', 'delta_s5_w_glu': 'delta_w', 'delta_w_branch': 'delta_w', 'delta_w_gate': 'delta_w', 'delta_b_gate': 'delta_w', 'delta_w_out': 'delta_w', 'delta_w_ff1': 'delta_w', 'delta_w_ff2': 'delta_w', 'delta_final_g': 'delta_w', 'new_m_c_ctx': 'new_m', 'new_m_w_mod': 'new_m', 'new_m_b_mod': 'new_m', 'new_m_norm1_g': 'new_m', 'new_m_norm2_g': 'new_m', 'new_m_w_in': 'new_m', 'new_m_gmlp_ln_g': 'new_m', 'new_m_gmlp_ln_b': 'new_m', 'new_m_gmlp_ws': 'new_m', 'new_m_gmlp_bs': 'new_m', 'new_m_conv_w': 'new_m', 'new_m_conv_b': 'new_m', 'new_m_conv_ln_g': 'new_m', 'new_m_conv_ln_b': 'new_m', 'new_m_attn_sink': 'new_m', 'new_m_s5_a_re': 'new_m', 'new_m_s5_a_im': 'new_m', 'new_m_s5_log_step': 'new_m', 'new_m_s5_b_re': 'new_m', 'new_m_s5_b_im': 'new_m', 'new_m_s5_c_re': 'new_m', 'new_m_s5_c_im': 'new_m', 'new_m_s5_d': 'new_m', 'new_m_s5_w_glu': 'new_m', 'new_m_w_branch': 'new_m', 'new_m_w_gate': 'new_m', 'new_m_b_gate': 'new_m', 'new_m_w_out': 'new_m', 'new_m_w_ff1': 'new_m', 'new_m_w_ff2': 'new_m', 'new_m_final_g': 'new_m', 'new_v_c_ctx': 'new_v', 'new_v_w_mod': 'new_v', 'new_v_b_mod': 'new_v', 'new_v_norm1_g': 'new_v', 'new_v_norm2_g': 'new_v', 'new_v_w_in': 'new_v', 'new_v_gmlp_ln_g': 'new_v', 'new_v_gmlp_ln_b': 'new_v', 'new_v_gmlp_ws': 'new_v', 'new_v_gmlp_bs': 'new_v', 'new_v_conv_w': 'new_v', 'new_v_conv_b': 'new_v', 'new_v_conv_ln_g': 'new_v', 'new_v_conv_ln_b': 'new_v', 'new_v_attn_sink': 'new_v', 'new_v_s5_a_re': 'new_v', 'new_v_s5_a_im': 'new_v', 'new_v_s5_log_step': 'new_v', 'new_v_s5_b_re': 'new_v', 'new_v_s5_b_im': 'new_v', 'new_v_s5_c_re': 'new_v', 'new_v_s5_c_im': 'new_v', 'new_v_s5_d': 'new_v', 'new_v_s5_w_glu': 'new_v', 'new_v_w_branch': 'new_v', 'new_v_w_gate': 'new_v', 'new_v_b_gate': 'new_v', 'new_v_w_out': 'new_v', 'new_v_w_ff1': 'new_v', 'new_v_w_ff2': 'new_v', 'new_v_final_g': 'new_v'}


def _forward(args):
    return _fwd_reference(*[args[k] for k in FWD_PARAMS])


def _output_shape():
    def fwd():
        inp = _fwd_setup_inputs(0)
        return _fwd_reference(*[inp[k] for k in FWD_PARAMS])
    out = _jax.eval_shape(fwd)
    return out.shape, out.dtype

N_MICROBATCH = 1
ADAM_LR = 0.001
ADAM_B1 = 0.9
ADAM_B2 = 0.999
ADAM_EPS = 1e-08
ADAM_WD = 0.01
ADAM_STEP = 10
PER_EXAMPLE_BATCH_AXIS = {'x': 0, 'c': 0, 'ctx': 0, 'loss_target': 0}
SHARED_INPUTS = []
_WEIGHT_DTYPES = {'c_ctx': _jnp.float32, 'w_mod': _jnp.float32, 'b_mod': _jnp.float32, 'norm1_g': _jnp.float32, 'norm2_g': _jnp.float32, 'w_in': _jnp.float32, 'gmlp_ln_g': _jnp.float32, 'gmlp_ln_b': _jnp.float32, 'gmlp_ws': _jnp.float32, 'gmlp_bs': _jnp.float32, 'conv_w': _jnp.float32, 'conv_b': _jnp.float32, 'conv_ln_g': _jnp.float32, 'conv_ln_b': _jnp.float32, 'attn_sink': _jnp.float32, 's5_a_re': _jnp.float32, 's5_a_im': _jnp.float32, 's5_log_step': _jnp.float32, 's5_b_re': _jnp.float32, 's5_b_im': _jnp.float32, 's5_c_re': _jnp.float32, 's5_c_im': _jnp.float32, 's5_d': _jnp.float32, 's5_w_glu': _jnp.float32, 'w_branch': _jnp.float32, 'w_gate': _jnp.float32, 'b_gate': _jnp.float32, 'w_out': _jnp.float32, 'w_ff1': _jnp.float32, 'w_ff2': _jnp.float32, 'final_g': _jnp.float32}
MOMENT_SCALE = {'c_ctx': 9.573914e-03, 'w_mod': 5.160384e-02, 'b_mod': 9.161159e-02, 'norm1_g': 3.017446e-02, 'norm2_g': 5.080642e-02, 'w_in': 2.288788e-02, 'gmlp_ln_g': 2.709712e-02, 'gmlp_ln_b': 2.647365e-02, 'gmlp_ws': 2.695734e-02, 'gmlp_bs': 2.749434e-02, 'conv_w': 2.370836e-02, 'conv_b': 4.139938e-02, 'conv_ln_g': 2.809250e-02, 'conv_ln_b': 2.525025e-02, 'attn_sink': 1.744855e-04, 's5_a_re': 2.876276e-03, 's5_a_im': 3.178811e-03, 's5_log_step': 1.212796e+00, 's5_b_re': 2.423713e-03, 's5_b_im': 2.386278e-03, 's5_c_re': 1.255075e-03, 's5_c_im': 1.239889e-03, 's5_d': 1.556491e-02, 's5_w_glu': 6.091931e-03, 'w_branch': 1.174874e-02, 'w_gate': 4.762517e-03, 'b_gate': 4.420778e-03, 'w_out': 2.355865e-02, 'w_ff1': 2.637527e-02, 'w_ff2': 4.843519e-02, 'final_g': 3.222800e+01}


def _to_microbatches(a, axis):
    t = _jnp.moveaxis(a, axis, 0)
    t = t.reshape((N_MICROBATCH, t.shape[0] // N_MICROBATCH) + t.shape[1:])
    return _jnp.moveaxis(t, 1, axis + 1)


def setup_inputs(seed: int = 0) -> dict:
    inp = _fwd_setup_inputs(seed)
    key = _jax.random.fold_in(_jax.random.key(seed), 7919)
    shape, _ = _output_shape()
    out = dict(inp)
    out["loss_target"] = _jax.random.normal(_jax.random.fold_in(key, 0), shape, _jnp.float32)
    for i, name in enumerate(TWIN_WEIGHTS):
        w = inp[name].astype(_jnp.float32)
        if MOMENT_SCALE is None:
            s = _jnp.sqrt(_jnp.mean(_jnp.square(w)) + 1e-30)
        else:
            s = MOMENT_SCALE[name]
        km, kv = _jax.random.split(_jax.random.fold_in(key, i + 1))
        out[name] = w
        out["m_" + name] = s * _jax.random.normal(km, w.shape, _jnp.float32)
        out["v_" + name] = (s * s) * _jax.random.uniform(kv, w.shape, _jnp.float32, 0.5, 1.5)
    if N_MICROBATCH > 1:
        for name, axis in PER_EXAMPLE_BATCH_AXIS.items():
            out[name] = _to_microbatches(out[name], axis)
    return {'x': out['x'], 'c': out['c'], 'ctx': out['ctx'], 'c_ctx': out['c_ctx'], 'w_mod': out['w_mod'], 'b_mod': out['b_mod'], 'norm1_g': out['norm1_g'], 'norm2_g': out['norm2_g'], 'w_in': out['w_in'], 'gmlp_ln_g': out['gmlp_ln_g'], 'gmlp_ln_b': out['gmlp_ln_b'], 'gmlp_ws': out['gmlp_ws'], 'gmlp_bs': out['gmlp_bs'], 'conv_w': out['conv_w'], 'conv_b': out['conv_b'], 'conv_ln_g': out['conv_ln_g'], 'conv_ln_b': out['conv_ln_b'], 'attn_sink': out['attn_sink'], 's5_a_re': out['s5_a_re'], 's5_a_im': out['s5_a_im'], 's5_log_step': out['s5_log_step'], 's5_b_re': out['s5_b_re'], 's5_b_im': out['s5_b_im'], 's5_c_re': out['s5_c_re'], 's5_c_im': out['s5_c_im'], 's5_d': out['s5_d'], 's5_w_glu': out['s5_w_glu'], 'w_branch': out['w_branch'], 'w_gate': out['w_gate'], 'b_gate': out['b_gate'], 'w_out': out['w_out'], 'w_ff1': out['w_ff1'], 'w_ff2': out['w_ff2'], 'final_g': out['final_g'], 'loss_target': out['loss_target'], 'm_c_ctx': out['m_c_ctx'], 'm_w_mod': out['m_w_mod'], 'm_b_mod': out['m_b_mod'], 'm_norm1_g': out['m_norm1_g'], 'm_norm2_g': out['m_norm2_g'], 'm_w_in': out['m_w_in'], 'm_gmlp_ln_g': out['m_gmlp_ln_g'], 'm_gmlp_ln_b': out['m_gmlp_ln_b'], 'm_gmlp_ws': out['m_gmlp_ws'], 'm_gmlp_bs': out['m_gmlp_bs'], 'm_conv_w': out['m_conv_w'], 'm_conv_b': out['m_conv_b'], 'm_conv_ln_g': out['m_conv_ln_g'], 'm_conv_ln_b': out['m_conv_ln_b'], 'm_attn_sink': out['m_attn_sink'], 'm_s5_a_re': out['m_s5_a_re'], 'm_s5_a_im': out['m_s5_a_im'], 'm_s5_log_step': out['m_s5_log_step'], 'm_s5_b_re': out['m_s5_b_re'], 'm_s5_b_im': out['m_s5_b_im'], 'm_s5_c_re': out['m_s5_c_re'], 'm_s5_c_im': out['m_s5_c_im'], 'm_s5_d': out['m_s5_d'], 'm_s5_w_glu': out['m_s5_w_glu'], 'm_w_branch': out['m_w_branch'], 'm_w_gate': out['m_w_gate'], 'm_b_gate': out['m_b_gate'], 'm_w_out': out['m_w_out'], 'm_w_ff1': out['m_w_ff1'], 'm_w_ff2': out['m_w_ff2'], 'm_final_g': out['m_final_g'], 'v_c_ctx': out['v_c_ctx'], 'v_w_mod': out['v_w_mod'], 'v_b_mod': out['v_b_mod'], 'v_norm1_g': out['v_norm1_g'], 'v_norm2_g': out['v_norm2_g'], 'v_w_in': out['v_w_in'], 'v_gmlp_ln_g': out['v_gmlp_ln_g'], 'v_gmlp_ln_b': out['v_gmlp_ln_b'], 'v_gmlp_ws': out['v_gmlp_ws'], 'v_gmlp_bs': out['v_gmlp_bs'], 'v_conv_w': out['v_conv_w'], 'v_conv_b': out['v_conv_b'], 'v_conv_ln_g': out['v_conv_ln_g'], 'v_conv_ln_b': out['v_conv_ln_b'], 'v_attn_sink': out['v_attn_sink'], 'v_s5_a_re': out['v_s5_a_re'], 'v_s5_a_im': out['v_s5_a_im'], 'v_s5_log_step': out['v_s5_log_step'], 'v_s5_b_re': out['v_s5_b_re'], 'v_s5_b_im': out['v_s5_b_im'], 'v_s5_c_re': out['v_s5_c_re'], 'v_s5_c_im': out['v_s5_c_im'], 'v_s5_d': out['v_s5_d'], 'v_s5_w_glu': out['v_s5_w_glu'], 'v_w_branch': out['v_w_branch'], 'v_w_gate': out['v_w_gate'], 'v_b_gate': out['v_b_gate'], 'v_w_out': out['v_w_out'], 'v_w_ff1': out['v_w_ff1'], 'v_w_ff2': out['v_w_ff2'], 'v_final_g': out['v_final_g']}


def _loss(weights, diff, rest, loss_target):
    with _jax.named_scope("forward"):
        args = {**rest, TWIN_DIFF_INPUT: diff, **{k: w.astype(_WEIGHT_DTYPES[k]) for k, w in weights.items()}}
        y = _forward(args)
    with _jax.named_scope("loss_head"):
        err = _jnp.square(y.astype(_jnp.float32) - loss_target)
        return 0.5 * _jnp.sum(_jnp.mean(err, axis=-1)) if err.ndim else 0.5 * err


def _adamw(w, g, m, v):
    m = ADAM_B1 * m + (1.0 - ADAM_B1) * g
    v = ADAM_B2 * v + (1.0 - ADAM_B2) * _jnp.square(g)
    m_hat = m / (1.0 - ADAM_B1 ** ADAM_STEP)
    v_hat = v / (1.0 - ADAM_B2 ** ADAM_STEP)
    delta = -ADAM_LR * (m_hat / (_jnp.sqrt(v_hat) + ADAM_EPS) + ADAM_WD * w)
    return delta, m, v


def reference(x, c, ctx, c_ctx, w_mod, b_mod, norm1_g, norm2_g, w_in, gmlp_ln_g, gmlp_ln_b, gmlp_ws, gmlp_bs, conv_w, conv_b, conv_ln_g, conv_ln_b, attn_sink, s5_a_re, s5_a_im, s5_log_step, s5_b_re, s5_b_im, s5_c_re, s5_c_im, s5_d, s5_w_glu, w_branch, w_gate, b_gate, w_out, w_ff1, w_ff2, final_g, loss_target, m_c_ctx, m_w_mod, m_b_mod, m_norm1_g, m_norm2_g, m_w_in, m_gmlp_ln_g, m_gmlp_ln_b, m_gmlp_ws, m_gmlp_bs, m_conv_w, m_conv_b, m_conv_ln_g, m_conv_ln_b, m_attn_sink, m_s5_a_re, m_s5_a_im, m_s5_log_step, m_s5_b_re, m_s5_b_im, m_s5_c_re, m_s5_c_im, m_s5_d, m_s5_w_glu, m_w_branch, m_w_gate, m_b_gate, m_w_out, m_w_ff1, m_w_ff2, m_final_g, v_c_ctx, v_w_mod, v_b_mod, v_norm1_g, v_norm2_g, v_w_in, v_gmlp_ln_g, v_gmlp_ln_b, v_gmlp_ws, v_gmlp_bs, v_conv_w, v_conv_b, v_conv_ln_g, v_conv_ln_b, v_attn_sink, v_s5_a_re, v_s5_a_im, v_s5_log_step, v_s5_b_re, v_s5_b_im, v_s5_c_re, v_s5_c_im, v_s5_d, v_s5_w_glu, v_w_branch, v_w_gate, v_b_gate, v_w_out, v_w_ff1, v_w_ff2, v_final_g):
    given = dict(x=x, c=c, ctx=ctx, c_ctx=c_ctx, w_mod=w_mod, b_mod=b_mod, norm1_g=norm1_g, norm2_g=norm2_g, w_in=w_in, gmlp_ln_g=gmlp_ln_g, gmlp_ln_b=gmlp_ln_b, gmlp_ws=gmlp_ws, gmlp_bs=gmlp_bs, conv_w=conv_w, conv_b=conv_b, conv_ln_g=conv_ln_g, conv_ln_b=conv_ln_b, attn_sink=attn_sink, s5_a_re=s5_a_re, s5_a_im=s5_a_im, s5_log_step=s5_log_step, s5_b_re=s5_b_re, s5_b_im=s5_b_im, s5_c_re=s5_c_re, s5_c_im=s5_c_im, s5_d=s5_d, s5_w_glu=s5_w_glu, w_branch=w_branch, w_gate=w_gate, b_gate=b_gate, w_out=w_out, w_ff1=w_ff1, w_ff2=w_ff2, final_g=final_g, loss_target=loss_target, m_c_ctx=m_c_ctx, m_w_mod=m_w_mod, m_b_mod=m_b_mod, m_norm1_g=m_norm1_g, m_norm2_g=m_norm2_g, m_w_in=m_w_in, m_gmlp_ln_g=m_gmlp_ln_g, m_gmlp_ln_b=m_gmlp_ln_b, m_gmlp_ws=m_gmlp_ws, m_gmlp_bs=m_gmlp_bs, m_conv_w=m_conv_w, m_conv_b=m_conv_b, m_conv_ln_g=m_conv_ln_g, m_conv_ln_b=m_conv_ln_b, m_attn_sink=m_attn_sink, m_s5_a_re=m_s5_a_re, m_s5_a_im=m_s5_a_im, m_s5_log_step=m_s5_log_step, m_s5_b_re=m_s5_b_re, m_s5_b_im=m_s5_b_im, m_s5_c_re=m_s5_c_re, m_s5_c_im=m_s5_c_im, m_s5_d=m_s5_d, m_s5_w_glu=m_s5_w_glu, m_w_branch=m_w_branch, m_w_gate=m_w_gate, m_b_gate=m_b_gate, m_w_out=m_w_out, m_w_ff1=m_w_ff1, m_w_ff2=m_w_ff2, m_final_g=m_final_g, v_c_ctx=v_c_ctx, v_w_mod=v_w_mod, v_b_mod=v_b_mod, v_norm1_g=v_norm1_g, v_norm2_g=v_norm2_g, v_w_in=v_w_in, v_gmlp_ln_g=v_gmlp_ln_g, v_gmlp_ln_b=v_gmlp_ln_b, v_gmlp_ws=v_gmlp_ws, v_gmlp_bs=v_gmlp_bs, v_conv_w=v_conv_w, v_conv_b=v_conv_b, v_conv_ln_g=v_conv_ln_g, v_conv_ln_b=v_conv_ln_b, v_attn_sink=v_attn_sink, v_s5_a_re=v_s5_a_re, v_s5_a_im=v_s5_a_im, v_s5_log_step=v_s5_log_step, v_s5_b_re=v_s5_b_re, v_s5_b_im=v_s5_b_im, v_s5_c_re=v_s5_c_re, v_s5_c_im=v_s5_c_im, v_s5_d=v_s5_d, v_s5_w_glu=v_s5_w_glu, v_w_branch=v_w_branch, v_w_gate=v_w_gate, v_b_gate=v_b_gate, v_w_out=v_w_out, v_w_ff1=v_w_ff1, v_w_ff2=v_w_ff2, v_final_g=v_final_g)
    weights = {n: given[n] for n in TWIN_WEIGHTS}
    shared = {n: given[n] for n in SHARED_INPUTS}
    per_example = {n: given[n] for n in ['x', 'c', 'ctx']}
    grad_fn = _jax.value_and_grad(_loss, argnums=(0, 1))

    def one_microbatch(ex, loss_target):
        ex = dict(ex)
        diff = ex.pop(TWIN_DIFF_INPUT)
        return grad_fn(weights, diff, {**shared, **ex}, loss_target)

    if N_MICROBATCH == 1:
        loss, (grad_w, grad_x) = one_microbatch(per_example, given["loss_target"])
    else:
        def body(carry, xs):
            loss_sum, grad_sum = carry
            l_k, (gw_k, gx_k) = one_microbatch(xs[0], xs[1])
            with _jax.named_scope("update"):
                return (loss_sum + l_k, _jax.tree.map(_jnp.add, grad_sum, gw_k)), gx_k

        init = (_jnp.zeros((), _jnp.float32), _jax.tree.map(_jnp.zeros_like, weights))
        (loss, grad_w), grad_x = _jax.lax.scan(body, init, (per_example, given["loss_target"]))
    with _jax.named_scope("update"):
        delta_w, new_m, new_v = {}, {}, {}
        for n in TWIN_WEIGHTS:
            delta_w[n], new_m[n], new_v[n] = _adamw(weights[n], grad_w[n], given["m_" + n], given["v_" + n])
    return (loss, grad_x, *[grad_w[n] for n in TWIN_WEIGHTS], *[delta_w[n] for n in TWIN_WEIGHTS],
            *[new_m[n] for n in TWIN_WEIGHTS], *[new_v[n] for n in TWIN_WEIGHTS])
```

```python
import functools
import math

import jax
import jax.numpy as jnp
from jax import lax
from jax.experimental import pallas as pl
from jax.experimental.pallas import tpu as pltpu

F32 = jnp.float32
BF16 = jnp.bfloat16
MESH = pl.DeviceIdType.MESH

V7X_VMEM_BYTES = 64 * 1024 * 1024
VMEM_LIMIT = V7X_VMEM_BYTES - 8 * 1024 * 1024
LANES = 128
SUBLANES = 8

N_BRANCH = 4
CHUNK = 128
GMLP_GROUPS = 4
CONV_W = 31
CONV_HALO = 16
HEAD_DIM = 64
N_KV_HEADS = 2
WINDOW = 128
BLOCK = 128
ROPE_BASE = 10000.0
GRID_W = 64
S5_GW = 16
S5_STATE = 64
N_MOD = 6
EPS = 1e-6
NEG_INF = -1e30
ADAM_LR = 0.001
ADAM_B1 = 0.9
ADAM_B2 = 0.999
ADAM_EPS = 1e-08
ADAM_WD = 0.01
ADAM_STEP = 10

ROW_TILE = 256


def _cparams(sem=None, **kw):
    if sem is not None:
        kw["dimension_semantics"] = sem
    return pltpu.CompilerParams(vmem_limit_bytes=VMEM_LIMIT, **kw)


def _pick(n, cands):
    for c in cands:
        if n % c == 0:
            return c
    return n


def _bdot(a, b, ca, cb):
    return lax.dot_general(a.astype(BF16), b.astype(BF16), (((ca,), (cb,)), ((), ())),
                           preferred_element_type=F32)


@jax.custom_vjp
def _mm_nn(a, b):
    return _bdot(a, b, 1, 0)


def _mm_nn_fwd(a, b):
    return _bdot(a, b, 1, 0), (a, b)


def _mm_nn_bwd(res, g):
    a, b = res
    return _bdot(g, b, 1, 1).astype(a.dtype), _bdot(a, g, 0, 0).astype(b.dtype)


_mm_nn.defvjp(_mm_nn_fwd, _mm_nn_bwd)


@jax.custom_vjp
def _mm_nt(a, b):
    return _bdot(a, b, 1, 1)


def _mm_nt_fwd(a, b):
    return _bdot(a, b, 1, 1), (a, b)


def _mm_nt_bwd(res, g):
    a, b = res
    return _bdot(g, b, 1, 0).astype(a.dtype), _bdot(g, a, 0, 0).astype(b.dtype)


_mm_nt.defvjp(_mm_nt_fwd, _mm_nt_bwd)


def _matmul(a, b, *, ta=False, tb=False, tm=None, tn=None, tk=None, name,
            out_dtypes=(F32,), epi=None, tiles=(), rowvecs=(), b_blocks=None, out_blocks=None):
    m, k = (a.shape[1], a.shape[0]) if ta else a.shape
    if b_blocks is None:
        k2, n = (b.shape[1], b.shape[0]) if tb else b.shape
        assert k == k2, (a.shape, b.shape, ta, tb)
    else:
        n = b_blocks[0]
    tm = tm or _pick(m, (1024, 512, 256, 128) if ta else (768, 512, 384, 256, 128))
    tn = tn or _pick(n, (2048, 1664, 1024, 512, 256, 128) if ta else (1024, 1664, 512, 256, 128))
    tk = tk or _pick(k, (2048, 1024, 768, 512, 384, 256, 128))
    assert m % tm == 0 and n % tn == 0 and k % tk == 0, (m, n, k, tm, tn, tk)
    nk = k // tk
    n_t, n_v, n_o = len(tiles), len(rowvecs), len(out_dtypes)
    ca, cb = (0 if ta else 1), (1 if tb else 0)

    def body(*refs):
        a_ref, b_ref = refs[:2]
        t_refs = refs[2:2 + n_t]
        v_refs = refs[2 + n_t:2 + n_t + n_v]
        o_refs = refs[2 + n_t + n_v:2 + n_t + n_v + n_o]

        def finish(acc):
            outs = (acc,) if epi is None else epi(acc, *[t[...] for t in t_refs], *[v[...] for v in v_refs])
            for o_ref, o in zip(o_refs, outs):
                o_ref[...] = o.astype(o_ref.dtype)

        part = _bdot(a_ref[...], b_ref[...], ca, cb)
        if nk == 1:
            finish(part)
        else:
            acc_ref = refs[-1]
            kk = pl.program_id(2)

            @pl.when(kk == 0)
            def _():
                acc_ref[...] = part

            @pl.when(kk > 0)
            def _():
                acc_ref[...] += part

            @pl.when(kk == nk - 1)
            def _():
                finish(acc_ref[...])

    a_spec = pl.BlockSpec((tk, tm), lambda i, j, kk: (kk, i)) if ta else pl.BlockSpec((tm, tk), lambda i, j, kk: (i, kk))
    b_spec = pl.BlockSpec((tn, tk), lambda i, j, kk: (j, kk)) if tb else pl.BlockSpec((tk, tn), lambda i, j, kk: (kk, j))
    if b_blocks is not None:
        b_spec = pl.BlockSpec(b_blocks[1], b_blocks[2])
    in_specs = [a_spec, b_spec]
    in_specs += [pl.BlockSpec((tm, tn), lambda i, j, kk: (i, j)) for _ in tiles]
    in_specs += [pl.BlockSpec((1, tn), lambda i, j, kk: (0, j)) for _ in rowvecs]
    out_specs = [pl.BlockSpec((tm, tn), lambda i, j, kk: (i, j)) for _ in out_dtypes]
    out_shape = [jax.ShapeDtypeStruct((m, n), d) for d in out_dtypes]
    if out_blocks is not None:
        out_specs = [pl.BlockSpec(out_blocks[1], out_blocks[2])]
        out_shape = [jax.ShapeDtypeStruct(out_blocks[0], out_dtypes[0])]
    out = pl.pallas_call(
        body, name=name,
        grid=(m // tm, n // tn, nk),
        in_specs=in_specs,
        out_specs=out_specs,
        out_shape=out_shape,
        scratch_shapes=[pltpu.VMEM((tm, tn), F32)] if nk > 1 else [],
        compiler_params=_cparams(("parallel", "parallel", "arbitrary")),
    )(a, b, *tiles, *rowvecs)
    return out[0] if n_o == 1 else tuple(out)


def _param_spec(kind, p, n_lat):
    if kind == "mod":
        return pl.BlockSpec((None,) + p.shape[1:], lambda i: (i // n_lat,) + (0,) * (p.ndim - 1))
    return pl.BlockSpec(p.shape, lambda i: (0,) * p.ndim)


def _rowwise(fn, rows, params, outs, *, name, tm=ROW_TILE, s_rows=None):
    r = rows[0].shape[0]
    assert r % tm == 0, (r, tm)
    n_lat = r // tm + 1 if s_rows is None else s_rows // tm
    n_r, n_p = len(rows), len(params)

    def body(*refs):
        vals = [x[...] for x in refs[:n_r + n_p]]
        res = fn(*vals)
        for o_ref, o in zip(refs[n_r + n_p:], res):
            o_ref[...] = o.astype(o_ref.dtype)

    out = pl.pallas_call(
        body, name=name, grid=(r // tm,),
        in_specs=[pl.BlockSpec((tm, x.shape[1]), lambda i: (i, 0)) for x in rows]
        + [_param_spec(kind, p, n_lat) for kind, p in params],
        out_specs=[pl.BlockSpec((tm, w), lambda i: (i, 0)) for w, _ in outs],
        out_shape=[jax.ShapeDtypeStruct((r, w), d) for w, d in outs],
        compiler_params=_cparams(("parallel",)),
    )(*rows, *[p for _, p in params])
    return tuple(out)


def _rowwise_vjp(fn, rows, params, cts, *, name, row_grads, tm=ROW_TILE, s_rows=None, adds=None):
    r = rows[0].shape[0]
    assert r % tm == 0, (r, tm)
    n_lat = r // tm + 1 if s_rows is None else s_rows // tm
    adds = adds or {}
    n_r, n_p, n_c, n_a = len(rows), len(params), len(cts), len(adds)
    want = [i for i, d in enumerate(row_grads) if d is not None]
    add_at = {idx: k for k, idx in enumerate(sorted(adds))}

    def body(*refs):
        i = pl.program_id(0)
        prim = [x[...].astype(F32) for x in refs[:n_r + n_p]]
        ct = [x[...].astype(F32) for x in refs[n_r + n_p:n_r + n_p + n_c]]
        a_refs = refs[n_r + n_p + n_c:n_r + n_p + n_c + n_a]
        o_refs = refs[n_r + n_p + n_c + n_a:]
        _, vjp = jax.vjp(fn, *prim)
        grads = vjp(tuple(ct))
        for o_ref, idx in zip(o_refs[:len(want)], want):
            g = grads[idx] if idx not in add_at else grads[idx] + a_refs[add_at[idx]][...]
            o_ref[...] = g.astype(o_ref.dtype)
        for o_ref, (kind, _), g in zip(o_refs[len(want):], params, grads[n_r:]):
            first = (i == 0) | (i == n_lat) if kind == "mod" else (i == 0)

            @pl.when(first)
            def _(o_ref=o_ref, g=g):
                o_ref[...] = g

            @pl.when(jnp.logical_not(first))
            def _(o_ref=o_ref, g=g):
                o_ref[...] += g

    out = pl.pallas_call(
        body, name=name, grid=(r // tm,),
        in_specs=[pl.BlockSpec((tm, x.shape[1]), lambda i: (i, 0)) for x in rows]
        + [_param_spec(kind, p, n_lat) for kind, p in params]
        + [pl.BlockSpec((tm, c.shape[1]), lambda i: (i, 0)) for c in cts]
        + [pl.BlockSpec((tm, adds[idx].shape[1]), lambda i: (i, 0)) for idx in sorted(adds)],
        out_specs=[pl.BlockSpec((tm, rows[idx].shape[1]), lambda i: (i, 0)) for idx in want]
        + [_param_spec(kind, p, n_lat) for kind, p in params],
        out_shape=[jax.ShapeDtypeStruct(rows[idx].shape, row_grads[idx]) for idx in want]
        + [jax.ShapeDtypeStruct(p.shape, F32) for _, p in params],
        compiler_params=_cparams(("arbitrary",)),
    )(*rows, *[p for _, p in params], *cts, *[adds[idx] for idx in sorted(adds)])
    return tuple(out[:len(want)]), tuple(out[len(want):])


def _rms_mod(x, g, scale, shift):
    y = x * lax.rsqrt(jnp.mean(x * x, axis=-1, keepdims=True) + EPS)
    return ((y * g) * (1.0 + scale) + shift,)


def _resid(x, o, gate):
    return (x + gate * o,)


def _layer_norm(x, g, b):
    xc = x - jnp.mean(x, axis=-1, keepdims=True)
    var = jnp.mean(xc * xc, axis=-1, keepdims=True)
    return xc * lax.rsqrt(var + EPS) * g + b


def _gmlp(za, ln_g, ln_b, ws, b_full):
    db = za.shape[1] // 2
    gw = db // GMLP_GROUPS
    za = jax.nn.gelu(za)
    u, v = za[:, :db], za[:, db:]
    v = _layer_norm(v, ln_g, ln_b)
    chunks = []
    for n in range(za.shape[0] // CHUNK):
        vn = v[n * CHUNK:(n + 1) * CHUNK]
        cols = [_mm_nn(ws[g], vn[:, g * gw:(g + 1) * gw]) for g in range(GMLP_GROUPS)]
        chunks.append(jnp.concatenate(cols, axis=1) + b_full)
    mixed = chunks[0] if len(chunks) == 1 else jnp.concatenate(chunks, axis=0)
    return (u * mixed,)


def _glu_gate(zb):
    db = zb.shape[1] // 2
    return (zb[:, :db] * jax.nn.sigmoid(zb[:, db:]),)


def _ln_silu(yc, g, b):
    return (jax.nn.silu(_layer_norm(yc, g, b)),)


def _s5_act(ys, u, d_skip):
    return (jax.nn.gelu(ys + d_skip * u),)


def _merge(*args):
    g, ps, b = args[0], args[1:1 + N_BRANCH], args[1 + N_BRANCH]
    d = ps[0].shape[1]
    s = jax.nn.sigmoid(g + b)
    out = s[:, :d] * ps[0]
    for k in range(1, N_BRANCH):
        out = out + s[:, k * d:(k + 1) * d] * ps[k]
    return (out,)


def _loss_rows(x, tgt, g):
    y = x * lax.rsqrt(jnp.mean(x * x, axis=-1, keepdims=True) + EPS) * g
    e = y - tgt
    return (0.5 * jnp.mean(e * e, axis=-1, keepdims=True),)


def _adamw(w, g, m, v):
    m = ADAM_B1 * m + (1.0 - ADAM_B1) * g
    v = ADAM_B2 * v + (1.0 - ADAM_B2) * jnp.square(g)
    m_hat = m / (1.0 - ADAM_B1 ** ADAM_STEP)
    v_hat = v / (1.0 - ADAM_B2 ** ADAM_STEP)
    delta = -ADAM_LR * (m_hat / (jnp.sqrt(v_hat) + ADAM_EPS) + ADAM_WD * w)
    return delta, m, v


def _whole(fn, args, outs, *, name):
    n_a = len(args)

    def body(*refs):
        res = fn(*[x[...] for x in refs[:n_a]])
        for o_ref, o in zip(refs[n_a:], res):
            o_ref[...] = o.astype(o_ref.dtype)

    return tuple(pl.pallas_call(
        body, name=name,
        out_shape=[jax.ShapeDtypeStruct(s, d) for s, d in outs],
        compiler_params=_cparams(),
    )(*args))


def _whole_vjp(fn, args, cts, *, name):
    n_a, n_c = len(args), len(cts)

    def body(*refs):
        prim = [x[...] for x in refs[:n_a]]
        ct = [x[...] for x in refs[n_a:n_a + n_c]]
        _, vjp = jax.vjp(fn, *prim)
        for o_ref, g in zip(refs[n_a + n_c:], vjp(tuple(ct))):
            o_ref[...] = g

    return tuple(pl.pallas_call(
        body, name=name,
        out_shape=[jax.ShapeDtypeStruct(a.shape, F32) for a in args],
        compiler_params=_cparams(),
    )(*args, *cts))


def _conv_flags(i, n_lat, n_tiles):
    has_prev = jnp.logical_and(i != 0, i != n_lat)
    has_next = jnp.logical_and(i != n_lat - 1, i != n_tiles - 1)
    return has_prev, has_next


def _halo_specs(width, tm, n_rows):
    per = tm // CONV_HALO
    last = n_rows // CONV_HALO - 1
    prev = pl.BlockSpec((CONV_HALO, width), lambda i: (jnp.maximum(i * per - 1, 0), 0))
    cur = pl.BlockSpec((tm, width), lambda i: (i, 0))
    nxt = pl.BlockSpec((CONV_HALO, width), lambda i: (jnp.minimum((i + 1) * per, last), 0))
    return [prev, cur, nxt]


def _with_halo(prev, cur, nxt, has_prev, has_next):
    prev = jnp.where(has_prev, prev, 0.0)
    nxt = jnp.where(has_next, nxt, 0.0)
    return jnp.concatenate([prev, cur, nxt], axis=0)


def _conv_fwd(zb, conv_w, conv_b, ln_g, ln_b, *, n_lat, tm=ROW_TILE):
    r, db = zb.shape[0], zb.shape[1] // 2
    n_tiles = r // tm
    half = CONV_W // 2

    def body(zp_ref, zc_ref, zn_ref, w_ref, b_ref, g_ref, lb_ref, out_ref, y_ref, yc_ref):
        i = pl.program_id(0)
        has_prev, has_next = _conv_flags(i, n_lat, n_tiles)
        y = _glu_gate(zc_ref[...])[0]
        y_ext = _with_halo(_glu_gate(zp_ref[...])[0], y, _glu_gate(zn_ref[...])[0], has_prev, has_next)
        w = w_ref[...]
        acc = jnp.zeros((tm, db), F32) + b_ref[...]
        for k in range(CONV_W):
            s = CONV_HALO - half + k
            acc = acc + w[k:k + 1, :] * y_ext[s:s + tm, :]
        y_ref[...] = y
        yc_ref[...] = acc
        out_ref[...] = _ln_silu(acc, g_ref[...], lb_ref[...])[0].astype(out_ref.dtype)

    full = lambda p: pl.BlockSpec(p.shape, lambda i: (0,) * p.ndim)
    return pl.pallas_call(
        body, name="conv_fwd", grid=(n_tiles,),
        in_specs=_halo_specs(2 * db, tm, r) + [full(conv_w), full(conv_b), full(ln_g), full(ln_b)],
        out_specs=[pl.BlockSpec((tm, db), lambda i: (i, 0))] * 3,
        out_shape=[jax.ShapeDtypeStruct((r, db), BF16), jax.ShapeDtypeStruct((r, db), F32),
                   jax.ShapeDtypeStruct((r, db), F32)],
        compiler_params=_cparams(("parallel",)),
    )(zb, zb, zb, conv_w, conv_b, ln_g, ln_b)


def _conv_bwd(d_yc, y, zb, conv_w, *, n_lat, tm=ROW_TILE):
    r, db = y.shape
    n_tiles = r // tm
    half = CONV_W // 2

    def body(gp_ref, gc_ref, gn_ref, yp_ref, yc_ref, yn_ref, z_ref, w_ref, dz_ref, dw_ref, db_ref):
        i = pl.program_id(0)
        has_prev, has_next = _conv_flags(i, n_lat, n_tiles)
        g = gc_ref[...]
        g_ext = _with_halo(gp_ref[...], g, gn_ref[...], has_prev, has_next)
        y_ext = _with_halo(yp_ref[...], yc_ref[...], yn_ref[...], has_prev, has_next)
        w = w_ref[...]

        @pl.when(i == 0)
        def _():
            dw_ref[...] = jnp.zeros_like(dw_ref)
            db_ref[...] = jnp.zeros_like(db_ref)

        d_y = jnp.zeros((tm, db), F32)
        for k in range(CONV_W):
            s = CONV_HALO + half - k
            d_y = d_y + w[k:k + 1, :] * g_ext[s:s + tm, :]
            s = CONV_HALO - half + k
            dw_ref[pl.ds(k, 1), :] += jnp.sum(g * y_ext[s:s + tm, :], axis=0, keepdims=True)
        db_ref[...] += jnp.sum(g, axis=0, keepdims=True)
        _, vjp = jax.vjp(_glu_gate, z_ref[...])
        dz_ref[...] = vjp((d_y,))[0]

    return pl.pallas_call(
        body, name="conv_bwd", grid=(n_tiles,),
        in_specs=_halo_specs(db, tm, r) + _halo_specs(db, tm, r)
        + [pl.BlockSpec((tm, 2 * db), lambda i: (i, 0)), pl.BlockSpec(conv_w.shape, lambda i: (0, 0))],
        out_specs=[pl.BlockSpec((tm, 2 * db), lambda i: (i, 0)), pl.BlockSpec((CONV_W, db), lambda i: (0, 0)),
                   pl.BlockSpec((1, db), lambda i: (0, 0))],
        out_shape=[jax.ShapeDtypeStruct((r, 2 * db), F32), jax.ShapeDtypeStruct((CONV_W, db), F32),
                   jax.ShapeDtypeStruct((1, db), F32)],
        compiler_params=_cparams(("arbitrary",)),
    )(d_yc, d_yc, d_yc, y, y, y, zb, conv_w)


def _rope(x, cos, sin_signed, perm):
    return x * cos + jnp.dot(x, perm, precision=lax.Precision.HIGHEST, preferred_element_type=F32) * sin_signed


def _softmax3(s_loc, s_ctx, sink_col):
    m = sink_col
    if s_loc is not None:
        m = jnp.maximum(m, jnp.max(s_loc, axis=-1, keepdims=True))
    m = lax.stop_gradient(jnp.maximum(m, jnp.max(s_ctx, axis=-1, keepdims=True)))
    e_ctx = jnp.exp(s_ctx - m)
    den = jnp.sum(e_ctx, axis=-1, keepdims=True) + jnp.exp(sink_col - m)
    if s_loc is None:
        return None, e_ctx / den
    e_loc = jnp.exp(s_loc - m)
    den = den + jnp.sum(e_loc, axis=-1, keepdims=True)
    return e_loc / den, e_ctx / den


def _attn_latent(q4, kb, vb, kc, vc, sink_col, cq, sq, ck, sk, perm, allowed):
    qpk = q4.shape[0]
    scale = HEAD_DIM ** -0.5
    q = _rope(q4.reshape(qpk * BLOCK, HEAD_DIM), jnp.concatenate([cq] * qpk, 0), jnp.concatenate([sq] * qpk, 0), perm)
    k = _rope(kb, ck, sk, perm)
    s_loc = jnp.where(allowed, _mm_nt(q, k) * scale, NEG_INF)
    s_ctx = _mm_nt(q, kc) * scale
    p_loc, p_ctx = _softmax3(s_loc, s_ctx, sink_col)
    o = _mm_nn(p_loc, vb) + _mm_nn(p_ctx, vc)
    return o.reshape(qpk, BLOCK, HEAD_DIM)


def _attn_context(q4, kc, vc, sink_col):
    qpk = q4.shape[0]
    scale = HEAD_DIM ** -0.5
    s_ctx = _mm_nt(q4.reshape(qpk * BLOCK, HEAD_DIM), kc) * scale
    _, p_ctx = _softmax3(None, s_ctx, sink_col)
    return _mm_nn(p_ctx, vc).reshape(qpk, BLOCK, HEAD_DIM)


def _attn_specs(nkv, qpk, nq, n_ctx, s_len):
    blk = lambda off: (lambda i: (0, jnp.clip(i + off, 0, nq - 1), 0))
    tab = lambda off: (lambda i: (jnp.clip(i + off, 0, nq - 1), 0))
    q_spec = pl.BlockSpec((nkv, qpk, BLOCK, HEAD_DIM), lambda i: (0, 0, i, 0))
    band = [pl.BlockSpec((nkv, BLOCK, HEAD_DIM), blk(off)) for off in (-1, 0, 1)]
    ctx = pl.BlockSpec((nkv, n_ctx, HEAD_DIM), lambda i: (0, s_len // n_ctx, 0))
    sink = pl.BlockSpec((nkv, qpk * BLOCK, 1), lambda i: (0, 0, 0))
    tabs = [pl.BlockSpec((BLOCK, HEAD_DIM), tab(off)) for off in (-1, 0, 1)]
    perm = pl.BlockSpec((HEAD_DIM, HEAD_DIM), lambda i: (0, 0))
    return q_spec, band, ctx, sink, tabs, perm


def _attn_mask(i, qpk, s_len):
    qpos = i * BLOCK + lax.broadcasted_iota(jnp.int32, (BLOCK, 3 * BLOCK), 0)
    kpos = (i - 1) * BLOCK + lax.broadcasted_iota(jnp.int32, (BLOCK, 3 * BLOCK), 1)
    ok = (jnp.abs(qpos - kpos) <= WINDOW) & (kpos >= 0) & (kpos < s_len)
    return jnp.concatenate([ok] * qpk, axis=0)


def _attn_fwd(q, k, v, sink_rows, cos, sin_signed, perm, *, s_len):
    nkv, qpk, r, _ = q.shape
    nq, n_ctx = s_len // BLOCK, r - s_len
    q_spec, band, ctx, sink, tabs, perm_spec = _attn_specs(nkv, qpk, nq, n_ctx, s_len)

    def body(q_ref, kp, kc_, kn, vp, vc_, vn, kx, vx, sk_ref, cp, cc, cn, sp, sc, sn, perm_ref, o_ref):
        i = pl.program_id(0)

        @pl.when(i < nq)
        def _():
            allowed = _attn_mask(i, qpk, s_len)
            ck = jnp.concatenate([cp[...], cc[...], cn[...]], 0)
            sk = jnp.concatenate([sp[...], sc[...], sn[...]], 0)
            for g in range(nkv):
                kb = jnp.concatenate([kp[g], kc_[g], kn[g]], 0)
                vb = jnp.concatenate([vp[g], vc_[g], vn[g]], 0)
                o = _attn_latent(q_ref[g], kb, vb, kx[g], vx[g], sk_ref[g], cc[...], sc[...], ck, sk, perm_ref[...], allowed)
                o_ref[g] = o.astype(o_ref.dtype)

        @pl.when(i >= nq)
        def _():
            for g in range(nkv):
                o_ref[g] = _attn_context(q_ref[g], kx[g], vx[g], sk_ref[g]).astype(o_ref.dtype)

    return pl.pallas_call(
        body, name="attn_fwd", grid=(r // BLOCK,),
        in_specs=[q_spec] + band + band + [ctx, ctx, sink] + tabs + tabs + [perm_spec],
        out_specs=q_spec,
        out_shape=jax.ShapeDtypeStruct(q.shape, BF16),
        compiler_params=_cparams(("parallel",)),
    )(q, k, k, k, v, v, v, k, v, sink_rows, cos, cos, cos, sin_signed, sin_signed, sin_signed, perm)


def _attn_bwd(q, k, v, sink_rows, cos, sin_signed, perm, d_o, *, s_len):
    nkv, qpk, r, _ = q.shape
    nq, n_ctx = s_len // BLOCK, r - s_len
    n_steps = r // BLOCK
    q_spec, band, ctx, sink, tabs, perm_spec = _attn_specs(nkv, qpk, nq, n_ctx, s_len)

    def body(q_ref, kp, kc_, kn, vp, vc_, vn, kx, vx, sk_ref, cp, cc, cn, sp, sc, sn, perm_ref, do_ref,
             dq_ref, dk_hbm, dv_hbm, dsk_ref, dk_acc, dv_acc):
        i = pl.program_id(0)

        @pl.when(i == 0)
        def _():
            dk_acc[...] = jnp.zeros_like(dk_acc)
            dv_acc[...] = jnp.zeros_like(dv_acc)
            dsk_ref[...] = jnp.zeros_like(dsk_ref)

        ctx_rows = pl.ds(s_len, n_ctx)

        @pl.when(i < nq)
        def _():
            allowed = _attn_mask(i, qpk, s_len)
            ck = jnp.concatenate([cp[...], cc[...], cn[...]], 0)
            sk = jnp.concatenate([sp[...], sc[...], sn[...]], 0)
            for g in range(nkv):
                kb = jnp.concatenate([kp[g], kc_[g], kn[g]], 0)
                vb = jnp.concatenate([vp[g], vc_[g], vn[g]], 0)
                fn = lambda q4, kb_, vb_, kc, vc, s_col: _attn_latent(
                    q4, kb_, vb_, kc, vc, s_col, cc[...], sc[...], ck, sk, perm_ref[...], allowed)
                _, vjp = jax.vjp(fn, q_ref[g], kb, vb, kx[g], vx[g], sk_ref[g])
                dq4, dkb, dvb, dkc, dvc, dsk = vjp(do_ref[g].astype(F32))
                dq_ref[g] = dq4
                for seg, off in enumerate((-1, 0, 1)):
                    rows = pl.ds(pl.multiple_of(jnp.clip(i + off, 0, nq - 1) * BLOCK, BLOCK), BLOCK)
                    dk_acc[g, rows, :] += dkb[seg * BLOCK:(seg + 1) * BLOCK]
                    dv_acc[g, rows, :] += dvb[seg * BLOCK:(seg + 1) * BLOCK]
                dk_acc[g, ctx_rows, :] += dkc
                dv_acc[g, ctx_rows, :] += dvc
                dsk_ref[g] += dsk

        @pl.when(i >= nq)
        def _():
            for g in range(nkv):
                _, vjp = jax.vjp(_attn_context, q_ref[g], kx[g], vx[g], sk_ref[g])
                dq4, dkc, dvc, dsk = vjp(do_ref[g].astype(F32))
                dq_ref[g] = dq4
                dk_acc[g, ctx_rows, :] += dkc
                dv_acc[g, ctx_rows, :] += dvc
                dsk_ref[g] += dsk

        @pl.when(i == n_steps - 1)
        def _():
            pltpu.sync_copy(dk_acc, dk_hbm)
            pltpu.sync_copy(dv_acc, dv_hbm)

    any_spec = pl.BlockSpec(memory_space=pl.ANY)
    return pl.pallas_call(
        body, name="attn_bwd", grid=(n_steps,),
        in_specs=[q_spec] + band + band + [ctx, ctx, sink] + tabs + tabs + [perm_spec, q_spec],
        out_specs=[q_spec, any_spec, any_spec, sink],
        out_shape=[jax.ShapeDtypeStruct(q.shape, F32), jax.ShapeDtypeStruct(k.shape, F32),
                   jax.ShapeDtypeStruct(v.shape, F32), jax.ShapeDtypeStruct(sink_rows.shape, F32)],
        scratch_shapes=[pltpu.VMEM(k.shape, F32), pltpu.VMEM(v.shape, F32)],
        compiler_params=_cparams(("arbitrary",)),
    )(q, k, k, k, v, v, v, k, v, sink_rows, cos, cos, cos, sin_signed, sin_signed, sin_signed, perm, d_o)


def _scan_orders(n_lat, n_ctx):
    fwd = lambda i: jnp.where(i < n_ctx, n_lat + i, i - n_ctx)
    bwd = lambda i: jnp.where(i < n_ctx, n_lat + n_ctx - 1 - i, n_lat - 1 - (i - n_ctx))
    return fwd, bwd


def _s5_scan(bu_f, bu_b, lbar, *, n_lat, tb=ROW_TILE):
    r, ch2, _ = bu_f.shape
    ch = ch2 // 2
    n_tiles = r // tb
    of, ob = _scan_orders(n_lat, n_tiles - n_lat)

    def body(bf_ref, bb_ref, a_ref, sf_ref, sb_ref, st_ref):
        @pl.when(pl.program_id(0) == 0)
        def _():
            st_ref[...] = jnp.zeros_like(st_ref)

        afr, afi, abr, abi = a_ref[0], a_ref[1], a_ref[2], a_ref[3]

        def step(t, carry):
            sfr, sfi, sbr, sbi = carry
            x = bf_ref[t]
            nfr = afr * sfr - afi * sfi + x[:ch]
            nfi = afr * sfi + afi * sfr + x[ch:]
            sf_ref[t] = jnp.concatenate([nfr, nfi], axis=0)
            u = tb - 1 - t
            x = bb_ref[u]
            nbr = abr * sbr - abi * sbi + x[:ch]
            nbi = abr * sbi + abi * sbr + x[ch:]
            sb_ref[u] = jnp.concatenate([nbr, nbi], axis=0)
            return nfr, nfi, nbr, nbi

        out = lax.fori_loop(0, tb, step, (st_ref[0], st_ref[1], st_ref[2], st_ref[3]), unroll=4)
        for n in range(4):
            st_ref[n] = out[n]

    spec = lambda order: pl.BlockSpec((tb, ch2, LANES), lambda i: (order(i), 0, 0))
    return pl.pallas_call(
        body, name="s5_scan", grid=(n_tiles,),
        in_specs=[spec(of), spec(ob), pl.BlockSpec(lbar.shape, lambda i: (0, 0, 0))],
        out_specs=[spec(of), spec(ob)],
        out_shape=[jax.ShapeDtypeStruct(bu_f.shape, F32), jax.ShapeDtypeStruct(bu_b.shape, F32)],
        scratch_shapes=[pltpu.VMEM((4, ch, LANES), F32)],
        compiler_params=_cparams(("arbitrary",)),
    )(bu_f, bu_b, lbar)


def _s5_scan_bwd(ds_f, ds_b, s_f, s_b, lbar, *, n_lat, tb=ROW_TILE):
    r, ch2, _ = ds_f.shape
    ch = ch2 // 2
    n_tiles = r // tb
    of, ob = _scan_orders(n_lat, n_tiles - n_lat)
    rof = lambda i: of(n_tiles - 1 - i)
    rob = lambda i: ob(n_tiles - 1 - i)

    def body(gf_ref, gb_ref, sf_ref, sb_ref, a_ref, df_ref, db_ref, da_ref, st_ref):
        @pl.when(pl.program_id(0) == 0)
        def _():
            st_ref[...] = jnp.zeros_like(st_ref)
            da_ref[...] = jnp.zeros_like(da_ref)

        afr, afi, abr, abi = a_ref[0], a_ref[1], a_ref[2], a_ref[3]

        def one(g_re, g_im, acc_re, acc_im, a_re, a_im, s, ds):
            acc_re = acc_re + s[:ch] * g_re + s[ch:] * g_im
            acc_im = acc_im - s[ch:] * g_re + s[:ch] * g_im
            n_re = ds[:ch] + a_re * g_re + a_im * g_im
            n_im = ds[ch:] - a_im * g_re + a_re * g_im
            return n_re, n_im, acc_re, acc_im

        def step(t, carry):
            gfr, gfi, gbr, gbi, cfr, cfi, cbr, cbi = carry
            u = tb - 1 - t
            gfr, gfi, cfr, cfi = one(gfr, gfi, cfr, cfi, afr, afi, sf_ref[u], gf_ref[u])
            df_ref[u] = jnp.concatenate([gfr, gfi], axis=0)
            gbr, gbi, cbr, cbi = one(gbr, gbi, cbr, cbi, abr, abi, sb_ref[t], gb_ref[t])
            db_ref[t] = jnp.concatenate([gbr, gbi], axis=0)
            return gfr, gfi, gbr, gbi, cfr, cfi, cbr, cbi

        zero = jnp.zeros((ch, LANES), F32)
        out = lax.fori_loop(0, tb, step, (st_ref[0], st_ref[1], st_ref[2], st_ref[3], zero, zero, zero, zero), unroll=4)
        for n in range(4):
            st_ref[n] = out[n]
            da_ref[n] += out[4 + n]

    spec = lambda order: pl.BlockSpec((tb, ch2, LANES), lambda i: (order(i), 0, 0))
    return pl.pallas_call(
        body, name="s5_scan_bwd", grid=(n_tiles,),
        in_specs=[spec(rof), spec(rob), spec(rof), spec(rob), pl.BlockSpec(lbar.shape, lambda i: (0, 0, 0))],
        out_specs=[spec(rof), spec(rob), pl.BlockSpec(lbar.shape, lambda i: (0, 0, 0))],
        out_shape=[jax.ShapeDtypeStruct(ds_f.shape, F32), jax.ShapeDtypeStruct(ds_b.shape, F32),
                   jax.ShapeDtypeStruct(lbar.shape, F32)],
        scratch_shapes=[pltpu.VMEM((4, ch, LANES), F32)],
        compiler_params=_cparams(("arbitrary",)),
    )(ds_f, ds_b, s_f, s_b, lbar)


def _s5_discretise(a_re, a_im, log_step, b_re, b_im):
    dt = jnp.exp(log_step)
    mag = jnp.exp(a_re * dt)
    l_re, l_im = mag * jnp.cos(a_im * dt), mag * jnp.sin(a_im * dt)
    den = a_re * a_re + a_im * a_im
    q_re = ((l_re - 1.0) * a_re + l_im * a_im) / den
    q_im = (l_im * a_re - (l_re - 1.0) * a_im) / den
    bb_re = q_re * b_re[None] - q_im * b_im[None]
    bb_im = q_re * b_im[None] + q_im * b_re[None]
    return l_re, l_im, bb_re, bb_im


N_DEV = 8
N_CHIPS = 4


def _place():
    x, y, c = lax.axis_index("x"), lax.axis_index("y"), lax.axis_index("c")
    chips = [(1 - x, y), (x, 1 - y), (1 - x, 1 - y)]
    return x, y, c, chips


def _allgather8(v, *, reduce=False, name):
    m_per, n = v.shape

    def body(x_ref, out_ref, *scratch):
        if reduce:
            all_ref, send_sems, recv_sems, local_sem = scratch
        else:
            all_ref = out_ref
            send_sems, recv_sems, local_sem = scratch
        x, y, c, chips = _place()
        me, sibling = (x, y, c), (x, y, 1 - c)

        def rows(px, py, pc):
            return all_ref.at[pl.ds((4 * px + 2 * py + pc) * m_per, m_per), :]

        def copy(k, block, to, src=None):
            return pltpu.make_async_remote_copy(
                src_ref=rows(*block) if src is None else src, dst_ref=rows(*block),
                send_sem=send_sems.at[k], recv_sem=recv_sems.at[k], device_id=to, device_id_type=MESH)

        mine = pltpu.make_async_copy(x_ref, rows(*me), local_sem)
        mine.start()
        first = [copy(0, me, sibling, src=x_ref)]
        first += [copy(1 + j, me, (*chip, c), src=x_ref) for j, chip in enumerate(chips)]
        for cp in first:
            cp.start()
        passed = [copy(4 + j, (*chip, c), sibling) for j, chip in enumerate(chips)]
        for j, chip in enumerate(chips):
            copy(1 + j, (*chip, c), me).wait_recv()
            passed[j].start()
        copy(0, sibling, me).wait_recv()
        for j, chip in enumerate(chips):
            copy(4 + j, (*chip, 1 - c), me).wait_recv()
        for cp in first + passed:
            cp.wait_send()
        mine.wait()
        if reduce:
            acc = all_ref[pl.ds(0, m_per), :]
            for d in range(1, N_DEV):
                acc = acc + all_ref[pl.ds(d * m_per, m_per), :]
            out_ref[...] = acc

    sems = [pltpu.SemaphoreType.DMA((7,)), pltpu.SemaphoreType.DMA((7,)), pltpu.SemaphoreType.DMA]
    return pl.pallas_call(
        body, name=name,
        out_shape=jax.ShapeDtypeStruct((m_per if reduce else N_DEV * m_per, n), v.dtype),
        in_specs=[pl.BlockSpec(memory_space=pltpu.VMEM)],
        out_specs=pl.BlockSpec(memory_space=pltpu.VMEM),
        scratch_shapes=([pltpu.VMEM((N_DEV * m_per, n), v.dtype)] if reduce else []) + sems,
        compiler_params=_cparams(),
    )(v)


def _halves(ref, half):
    h = ref.shape[0] // 2
    return ref.at[pl.ds(half * h, h)]


def _gather_shards(shards):
    n_w = len(shards)

    def body(*refs):
        x_refs, out_refs = refs[:n_w], refs[n_w:2 * n_w]
        send_sems, recv_sems, local_sems = refs[2 * n_w:]
        x, y, c, chips = _place()
        me, sibling = (x, y, c), (x, y, 1 - c)
        chip_no = lambda px, py: 2 * px + py

        def copy(n, k, chip, half, to, src=None):
            dst = _halves(out_refs[n].at[chip_no(*chip)], half)
            return pltpu.make_async_remote_copy(
                src_ref=dst if src is None else src, dst_ref=dst,
                send_sem=send_sems.at[n, k], recv_sem=recv_sems.at[n, k], device_id=to, device_id_type=MESH)

        mine = [pltpu.make_async_copy(x_refs[n], out_refs[n].at[chip_no(x, y)], local_sems.at[n]) for n in range(n_w)]
        for cp in mine:
            cp.start()
        first = [copy(n, j, (x, y), c, (*chip, c), src=_halves(x_refs[n], c))
                 for n in range(n_w) for j, chip in enumerate(chips)]
        for cp in first:
            cp.start()
        passed = []
        for n in range(n_w):
            for j, chip in enumerate(chips):
                copy(n, j, chip, c, me).wait_recv()
                passed.append(copy(n, 3 + j, chip, c, sibling))
                passed[-1].start()
        for n in range(n_w):
            for j, chip in enumerate(chips):
                copy(n, 3 + j, chip, 1 - c, me).wait_recv()
        for cp in first + passed:
            cp.wait_send()
        for cp in mine:
            cp.wait()

    any_spec = pl.BlockSpec(memory_space=pl.ANY)
    return pl.pallas_call(
        body, name="gather_shards",
        out_shape=[jax.ShapeDtypeStruct((N_CHIPS,) + s.shape, s.dtype) for s in shards],
        in_specs=[any_spec] * n_w, out_specs=[any_spec] * n_w,
        scratch_shapes=[pltpu.SemaphoreType.DMA((n_w, 6)), pltpu.SemaphoreType.DMA((n_w, 6)),
                        pltpu.SemaphoreType.DMA((n_w,))],
        compiler_params=_cparams(),
    )(*shards)


def _rs_swap_in(grads):
    n_w = len(grads)

    def body(*refs):
        x_refs, out_refs = refs[:n_w], refs[n_w:2 * n_w]
        send_sems, recv_sems = refs[2 * n_w:]
        x, y, c, _ = _place()
        cps = []
        for n in range(n_w):
            h = x_refs[n].shape[1] // 2
            cps.append(pltpu.make_async_remote_copy(
                src_ref=x_refs[n].at[:, pl.ds((1 - c) * h, h)], dst_ref=out_refs[n],
                send_sem=send_sems.at[n], recv_sem=recv_sems.at[n], device_id=(x, y, 1 - c), device_id_type=MESH))
        for cp in cps:
            cp.start()
        for cp in cps:
            cp.wait()

    any_spec = pl.BlockSpec(memory_space=pl.ANY)
    return pl.pallas_call(
        body, name="rs_swap_in",
        out_shape=[jax.ShapeDtypeStruct((g.shape[0], g.shape[1] // 2) + g.shape[2:], g.dtype) for g in grads],
        in_specs=[any_spec] * n_w, out_specs=[any_spec] * n_w,
        scratch_shapes=[pltpu.SemaphoreType.DMA((n_w,)), pltpu.SemaphoreType.DMA((n_w,))],
        compiler_params=_cparams(),
    )(*grads)


def _chip_exchange(parts):
    n_w = len(parts)

    def body(*refs):
        x_refs, out_refs = refs[:n_w], refs[n_w:2 * n_w]
        send_sems, recv_sems = refs[2 * n_w:]
        x, y, c, chips = _place()
        cps = [pltpu.make_async_remote_copy(
            src_ref=x_refs[n].at[2 * chip[0] + chip[1]], dst_ref=out_refs[n].at[j],
            send_sem=send_sems.at[n, j], recv_sem=recv_sems.at[n, j], device_id=(*chip, c), device_id_type=MESH)
            for n in range(n_w) for j, chip in enumerate(chips)]
        for cp in cps:
            cp.start()
        for cp in cps:
            cp.wait()

    any_spec = pl.BlockSpec(memory_space=pl.ANY)
    return pl.pallas_call(
        body, name="chip_exchange",
        out_shape=[jax.ShapeDtypeStruct((3,) + v.shape[1:], v.dtype) for v in parts],
        in_specs=[any_spec] * n_w, out_specs=[any_spec] * n_w,
        scratch_shapes=[pltpu.SemaphoreType.DMA((n_w, 3)), pltpu.SemaphoreType.DMA((n_w, 3))],
        compiler_params=_cparams(),
    )(*parts)


def _rs_finish(reduced):
    n_w = len(reduced)

    def body(*refs):
        x_refs, out_refs = refs[:n_w], refs[n_w:2 * n_w]
        send_sems, recv_sems, local_sems = refs[2 * n_w:]
        x, y, c, _ = _place()
        local, cps = [], []
        for n in range(n_w):
            dst = _halves(out_refs[n], c)
            local.append(pltpu.make_async_copy(x_refs[n], dst, local_sems.at[n]))
            cps.append(pltpu.make_async_remote_copy(
                src_ref=x_refs[n], dst_ref=dst, send_sem=send_sems.at[n], recv_sem=recv_sems.at[n],
                device_id=(x, y, 1 - c), device_id_type=MESH))
        for cp in local + cps:
            cp.start()
        for n in range(n_w):
            cps[n].wait_send()
            pltpu.make_async_remote_copy(
                src_ref=x_refs[n], dst_ref=_halves(out_refs[n], 1 - c), send_sem=send_sems.at[n],
                recv_sem=recv_sems.at[n], device_id=(x, y, 1 - c), device_id_type=MESH).wait_recv()
            local[n].wait()

    any_spec = pl.BlockSpec(memory_space=pl.ANY)
    return pl.pallas_call(
        body, name="rs_finish",
        out_shape=[jax.ShapeDtypeStruct((2 * v.shape[0],) + v.shape[1:], v.dtype) for v in reduced],
        in_specs=[any_spec] * n_w, out_specs=[any_spec] * n_w,
        scratch_shapes=[pltpu.SemaphoreType.DMA((n_w,)), pltpu.SemaphoreType.DMA((n_w,)),
                        pltpu.SemaphoreType.DMA((n_w,))],
        compiler_params=_cparams(),
    )(*reduced)


def _as_rows(shape):
    return (math.prod(shape[:-1]), shape[-1])


def _row_tile(rows, cols):
    return _pick(rows, tuple(t for t in (2048, 1024, 512, 256, 128, 64, 32, 16, 8) if t * cols * 4 <= (1 << 21)))


def _pair_sum(g, t, c_idx):
    rows, cols = _as_rows(t.shape[1:])
    tr = _row_tile(rows, cols)

    def body(c_ref, g_ref, t_ref, o_ref):
        o_ref[...] = g_ref[...] + t_ref[...]

    out = pl.pallas_call(
        body, name="rs_pair_sum",
        grid_spec=pltpu.PrefetchScalarGridSpec(
            num_scalar_prefetch=1, grid=(N_CHIPS, rows // tr),
            in_specs=[pl.BlockSpec((None, None, tr, cols), lambda j, i, c: (j, c[0], i, 0)),
                      pl.BlockSpec((None, tr, cols), lambda j, i, c: (j, i, 0))],
            out_specs=pl.BlockSpec((None, tr, cols), lambda j, i, c: (j, i, 0))),
        out_shape=jax.ShapeDtypeStruct((N_CHIPS, rows, cols), F32),
        compiler_params=_cparams(("parallel", "parallel")),
    )(c_idx, g.reshape(N_CHIPS, 2, rows, cols), t.reshape(N_CHIPS, rows, cols))
    return out.reshape(t.shape)


def _chip_sum(q, u, j_idx):
    rows, cols = _as_rows(q.shape[1:])
    tr = _row_tile(rows, cols)

    def body(j_ref, q_ref, u_ref, o_ref):
        o_ref[...] = ((q_ref[...] + u_ref[0]) + u_ref[1]) + u_ref[2]

    out = pl.pallas_call(
        body, name="rs_chip_sum",
        grid_spec=pltpu.PrefetchScalarGridSpec(
            num_scalar_prefetch=1, grid=(rows // tr,),
            in_specs=[pl.BlockSpec((None, tr, cols), lambda i, j: (j[0], i, 0)),
                      pl.BlockSpec((3, tr, cols), lambda i, j: (0, i, 0))],
            out_specs=pl.BlockSpec((tr, cols), lambda i, j: (i, 0))),
        out_shape=jax.ShapeDtypeStruct((rows, cols), F32),
        compiler_params=_cparams(("parallel",)),
    )(j_idx, q.reshape(N_CHIPS, rows, cols), u.reshape(3, rows, cols))
    return out.reshape(q.shape[1:])


def _reduce_scatter(grads):
    x, y, c = lax.axis_index("x"), lax.axis_index("y"), lax.axis_index("c")
    c_idx = jnp.reshape(c, (1,)).astype(jnp.int32)
    j_idx = jnp.reshape(2 * x + y, (1,)).astype(jnp.int32)
    theirs = _rs_swap_in(grads)
    pair = [_pair_sum(g, t, c_idx) for g, t in zip(grads, theirs)]
    got = _chip_exchange(pair)
    reduced = [_chip_sum(q, u, j_idx) for q, u in zip(pair, got)]
    return _rs_finish(reduced)


WEIGHT_NAMES = ('c_ctx', 'w_mod', 'b_mod', 'norm1_g', 'norm2_g', 'w_in', 'gmlp_ln_g', 'gmlp_ln_b', 'gmlp_ws',
                'gmlp_bs', 'conv_w', 'conv_b', 'conv_ln_g', 'conv_ln_b', 'attn_sink', 's5_a_re', 's5_a_im',
                's5_log_step', 's5_b_re', 's5_b_im', 's5_c_re', 's5_c_im', 's5_d', 's5_w_glu', 'w_branch',
                'w_gate', 'b_gate', 'w_out', 'w_ff1', 'w_ff2', 'final_g')
BIG = ('w_in', 's5_w_glu', 'w_branch', 'w_gate', 'w_out', 'w_ff1', 'w_ff2')
SMALL = ('norm1_g', 'norm2_g', 'gmlp_ln_g', 'gmlp_ln_b', 'gmlp_ws', 'gmlp_bs', 'conv_w', 'conv_b', 'conv_ln_g',
         'conv_ln_b', 'attn_sink', 's5_a_re', 's5_a_im', 's5_log_step', 's5_b_re', 's5_b_im', 's5_c_re',
         's5_c_im', 's5_d', 'b_gate')
SMALL_SHARDED = ('conv_w', 'b_gate')


def _from_shards(name, sh):
    j = sh.shape[0]
    if name in ('w_in', 's5_w_glu', 'w_ff1'):
        return sh.transpose(1, 0, 2).reshape(sh.shape[1], j * sh.shape[2])
    if name == 'w_branch':
        return sh.transpose(1, 2, 0, 3).reshape(sh.shape[1], sh.shape[2], j * sh.shape[3])
    if name == 'w_gate':
        return sh.transpose(0, 2, 1, 3).reshape(j * sh.shape[2], sh.shape[1] * sh.shape[3])
    return sh.reshape(j * sh.shape[1], sh.shape[2])


def _to_shards(name, dw, j=N_CHIPS):
    if name in ('w_in', 's5_w_glu', 'w_ff1'):
        return dw.reshape(dw.shape[0], j, dw.shape[1] // j).transpose(1, 0, 2)
    if name == 'w_branch':
        return dw.reshape(dw.shape[0], dw.shape[1], j, dw.shape[2] // j).transpose(2, 0, 1, 3)
    if name == 'w_gate':
        d = dw.shape[0]
        return dw.reshape(j, d // j, dw.shape[1] // d, d).transpose(0, 2, 1, 3)
    return dw.reshape(j, dw.shape[0] // j, dw.shape[1])


def _pack_rows(flat_parts, lead):
    flat = jnp.concatenate(flat_parts, axis=-1)
    assert flat.shape[-1] % LANES == 0, flat.shape
    return flat.reshape(lead + (flat.shape[-1] // LANES, LANES))


def _block_diag(blocks):
    g, a, b = blocks.shape
    eye = jnp.eye(g, dtype=blocks.dtype)
    return (blocks[:, :, None, :] * eye[:, None, :, None]).reshape(g * a, g * b)


def _diag_blocks(mat, g):
    a, b = mat.shape[0] // g, mat.shape[1] // g
    eye = jnp.eye(g, dtype=mat.dtype)
    return (mat.reshape(g, a, g, b) * eye[:, None, :, None]).sum(axis=2)


def _rope_tables(s_len):
    rows = s_len // GRID_W
    row = jnp.repeat(jnp.arange(rows), GRID_W).astype(F32)
    col = jnp.tile(jnp.arange(GRID_W), rows).astype(F32)
    d = HEAD_DIM // 2
    inv = ROPE_BASE ** (-jnp.arange(0, d, 2, dtype=F32) / d)
    ar, ac = row[:, None] * inv[None, :], col[:, None] * inv[None, :]
    cos = jnp.concatenate([jnp.cos(ar), jnp.cos(ar), jnp.cos(ac), jnp.cos(ac)], axis=1)
    sin_signed = jnp.concatenate([-jnp.sin(ar), jnp.sin(ar), -jnp.sin(ac), jnp.sin(ac)], axis=1)
    idx = jnp.arange(HEAD_DIM)
    partner = jnp.where(idx % d < d // 2, idx + d // 2, idx - d // 2)
    perm = (idx[:, None] == partner[None, :]).astype(F32)
    return cos, sin_signed, perm


def _loss_and_grad(xa, tgt, g, *, s_len, tm=ROW_TILE):
    d = xa.shape[1]

    def body(x_ref, t_ref, g_ref, l_ref, dx_ref, dg_ref):
        i = pl.program_id(0)
        (rows,), vjp = jax.vjp(_loss_rows, x_ref[...], t_ref[...], g_ref[...])
        dx, _, dg = vjp((jnp.ones_like(rows),))
        dx_ref[...] = dx
        part = jnp.zeros(l_ref.shape, F32) + jnp.sum(rows)

        @pl.when(i == 0)
        def _():
            l_ref[...] = part
            dg_ref[...] = dg

        @pl.when(i > 0)
        def _():
            l_ref[...] += part
            dg_ref[...] += dg

    return pl.pallas_call(
        body, name="loss_head", grid=(s_len // tm,),
        in_specs=[pl.BlockSpec((tm, d), lambda i: (i, 0)), pl.BlockSpec((tm, d), lambda i: (i, 0)),
                  pl.BlockSpec((1, d), lambda i: (0, 0))],
        out_specs=[pl.BlockSpec((SUBLANES, LANES), lambda i: (0, 0)), pl.BlockSpec((tm, d), lambda i: (i, 0)),
                   pl.BlockSpec((1, d), lambda i: (0, 0))],
        out_shape=[jax.ShapeDtypeStruct((SUBLANES, LANES), F32), jax.ShapeDtypeStruct((s_len, d), F32),
                   jax.ShapeDtypeStruct((1, d), F32)],
        compiler_params=_cparams(("arbitrary",)),
    )(xa, tgt, g)


def _adamw_call(w, g, m, v, *, name):
    shape = w.shape
    cols = shape[-1] if w.ndim > 1 else LANES
    two_d = lambda a: a.reshape(-1, cols)
    rows = two_d(w).shape[0]
    tm = _pick(rows, tuple(t for t in (1024, 512, 256, 128, 64, 32, 16, 8) if t * cols * 4 <= (1 << 20)))
    outs = _rowwise(_adamw, [two_d(w), two_d(g), two_d(m), two_d(v)], [], [(cols, F32)] * 3, name=name, tm=tm)
    return tuple(o.reshape(shape) for o in outs)


def kernel(x, c, ctx, c_ctx, w_mod, b_mod, norm1_g, norm2_g, w_in, gmlp_ln_g, gmlp_ln_b, gmlp_ws, gmlp_bs, conv_w, conv_b, conv_ln_g, conv_ln_b, attn_sink, s5_a_re, s5_a_im, s5_log_step, s5_b_re, s5_b_im, s5_c_re, s5_c_im, s5_d, s5_w_glu, w_branch, w_gate, b_gate, w_out, w_ff1, w_ff2, final_g, loss_target, m_c_ctx, m_w_mod, m_b_mod, m_norm1_g, m_norm2_g, m_w_in, m_gmlp_ln_g, m_gmlp_ln_b, m_gmlp_ws, m_gmlp_bs, m_conv_w, m_conv_b, m_conv_ln_g, m_conv_ln_b, m_attn_sink, m_s5_a_re, m_s5_a_im, m_s5_log_step, m_s5_b_re, m_s5_b_im, m_s5_c_re, m_s5_c_im, m_s5_d, m_s5_w_glu, m_w_branch, m_w_gate, m_b_gate, m_w_out, m_w_ff1, m_w_ff2, m_final_g, v_c_ctx, v_w_mod, v_b_mod, v_norm1_g, v_norm2_g, v_w_in, v_gmlp_ln_g, v_gmlp_ln_b, v_gmlp_ws, v_gmlp_bs, v_conv_w, v_conv_b, v_conv_ln_g, v_conv_ln_b, v_attn_sink, v_s5_a_re, v_s5_a_im, v_s5_log_step, v_s5_b_re, v_s5_b_im, v_s5_c_re, v_s5_c_im, v_s5_d, v_s5_w_glu, v_w_branch, v_w_gate, v_b_gate, v_w_out, v_w_ff1, v_w_ff2, v_final_g):
    given = dict(locals())
    p = {n: given[n] for n in WEIGHT_NAMES}
    mom = {n: given["m_" + n] for n in WEIGHT_NAMES}
    var = {n: given["v_" + n] for n in WEIGHT_NAMES}

    s_len, d = x.shape[1], x.shape[2]
    n_ctx = ctx.shape[1]
    r = s_len + n_ctx
    n_layers = w_mod.shape[0]
    db = d // N_BRANCH
    n_q = db // HEAD_DIM
    qpk = n_q // N_KV_HEADS
    g5 = db // S5_GW
    gp = g5 * S5_STATE
    ch = gp // LANES
    gw = db // GMLP_GROUPS
    n_lat = s_len // ROW_TILE
    assert s_len % ROW_TILE == 0 and n_ctx % ROW_TILE == 0 and s_len % n_ctx == 0 and gp % LANES == 0
    ax, ay, ac = lax.axis_index("x"), lax.axis_index("y"), lax.axis_index("c")
    j_me = 2 * ax + ay
    d_me = 4 * ax + 2 * ay + ac
    rw = functools.partial(_rowwise, s_rows=s_len)
    rwv = functools.partial(_rowwise_vjp, s_rows=s_len)
    add_epi = lambda acc, t: (acc + t,)

    c_all = _allgather8(jnp.broadcast_to(c, (SUBLANES, d)), name="gather_c")[::SUBLANES]
    c16 = jnp.concatenate([c_all, jnp.broadcast_to(c_ctx[None], (N_DEV, d))], axis=0)
    silu_fn = lambda a: (jax.nn.silu(a),)
    cond16 = _whole(silu_fn, [c16], [(c16.shape, F32)], name="silu_c")[0]
    ncol = w_mod.shape[2]
    b_mod_sh = lax.dynamic_slice_in_dim(b_mod, j_me * ncol, ncol, axis=1)
    mod_part = jnp.concatenate([
        _matmul(cond16, w_mod[l], tm=2 * N_DEV, rowvecs=[b_mod_sh[l][None]], epi=add_epi, name="mod_proj")
        for l in range(n_layers)], axis=0)
    mod_all = _allgather8(mod_part, name="gather_mod").reshape(N_CHIPS, 2, n_layers, 2 * N_DEV, ncol)[:, 0]
    mod_all = mod_all.transpose(1, 2, 0, 3).reshape(n_layers, 2 * N_DEV, N_CHIPS * ncol)

    def mods_of(l):
        two = jnp.stack([lax.dynamic_index_in_dim(mod_all[l], d_me, 0, keepdims=False), mod_all[l, N_DEV]])
        return [two[:, None, k * d:(k + 1) * d] for k in range(N_MOD)]

    d_sh, f_all = d // N_CHIPS, w_ff1.shape[2] * N_CHIPS
    f_sh = f_all // N_CHIPS

    def gather_layer(l):
        st = dict(zip(BIG, _gather_shards([p[n][l].astype(BF16) for n in BIG])))
        w = {n: _from_shards(n, st[n]) for n in ("w_in", "s5_w_glu", "w_branch")}
        w["w_out"] = st["w_out"].reshape(d, d)
        w["w_ff2"] = st["w_ff2"].reshape(f_all, d)
        w["w_gate"], w["w_ff1"] = st["w_gate"], st["w_ff1"]
        return w

    g_tn = min(1024, d)
    gate_fwd = dict(tk=d_sh, tn=g_tn, b_blocks=(N_BRANCH * d, (None, None, d_sh, g_tn),
                                                lambda i, j, kk: (kk, j // (d // g_tn), 0, j % (d // g_tn))))
    gate_bwd = dict(tb=True, tn=d_sh, tk=d, b_blocks=(d, (None, None, d_sh, d), lambda i, j, kk: (j, kk, 0, 0)))
    gw_tn = min(2048, d)
    gate_wgt = dict(ta=True, tm=d_sh, tn=gw_tn, out_blocks=(
        (N_CHIPS, N_BRANCH, d_sh, d), (None, None, d_sh, gw_tn),
        lambda i, j, kk: (i, j // (d // gw_tn), 0, j % (d // gw_tn))))
    f_tn = min(1024, f_sh)
    ff1_fwd = dict(tn=f_tn, tk=d, b_blocks=(f_all, (None, d, f_tn), lambda i, j, kk: (j // (f_sh // f_tn), 0, j % (f_sh // f_tn))))
    f_tk = min(2048, f_sh)
    fb_tn = min(1024, d)
    ff1_bwd = dict(tb=True, tn=fb_tn, tk=f_tk, b_blocks=(
        d, (None, fb_tn, f_tk), lambda i, j, kk: (kk // (f_sh // f_tk), j, kk % (f_sh // f_tk))))
    fw_tn, fw_tm = min(2048, f_sh), min(1024, d)
    ff1_wgt = dict(ta=True, tm=fw_tm, tn=fw_tn, out_blocks=(
        (N_CHIPS, d, f_sh), (None, fw_tm, fw_tn), lambda i, j, kk: (j // (f_sh // fw_tn), i, j % (f_sh // fw_tn))))

    def pad_rows(flat):
        n = -(-flat.shape[0] // (SUBLANES * LANES)) * SUBLANES * LANES
        return jnp.pad(flat, (0, n - flat.shape[0])).reshape(n // LANES, LANES)

    sh_flat = pad_rows(jnp.concatenate([conv_w.reshape(-1), b_gate.reshape(-1)]))
    sh_all = _allgather8(sh_flat, name="gather_small_w").reshape(N_CHIPS, 2, -1)[:, 0]
    conv_w_full = sh_all[:, :conv_w.size].reshape((N_CHIPS,) + conv_w.shape).transpose(1, 2, 0, 3)
    conv_w_full = conv_w_full.reshape(n_layers, CONV_W, db)
    b_gate_full = sh_all[:, conv_w.size:conv_w.size + b_gate.size].reshape((N_CHIPS,) + b_gate.shape)
    b_gate_full = b_gate_full.transpose(1, 2, 0, 3).reshape(n_layers, 1, N_BRANCH * d)

    cos, sin_signed, perm = _rope_tables(s_len)
    xa = jnp.concatenate([x[0], ctx[0]], axis=0)
    col = lambda a: a[None] if a.ndim == 1 else a
    to_heads = lambda a, nh: a.reshape(r, N_KV_HEADS, nh // N_KV_HEADS, HEAD_DIM).transpose(1, 2, 0, 3)
    from_heads = lambda a: a.transpose(2, 0, 1, 3).reshape(r, -1)
    sink_rows_of = lambda l: jnp.repeat(attn_sink[l].reshape(N_KV_HEADS, qpk, 1), BLOCK, axis=1).reshape(
        N_KV_HEADS, qpk * BLOCK, 1)
    slab = lambda a: a.reshape(r, 2 * ch, LANES)
    flat2 = lambda a: a.reshape(r, 2 * gp)

    saved = []
    for l in range(n_layers):
        w = gather_layer(l)
        shift1, scale1, gate1, shift2, scale2, gate2 = mods_of(l)
        n1g, n2g = col(norm1_g[l]), col(norm2_g[l])
        h = rw(_rms_mod, [xa], [("full", n1g), ("mod", scale1), ("mod", shift1)], [(d, BF16)], name="norm1")[0]
        z = _matmul(h, w["w_in"], name="in_proj")
        o1, o2, o3, o4, o5 = 2 * db, 4 * db, 4 * db + n_q * HEAD_DIM, 4 * db + (n_q + N_KV_HEADS) * HEAD_DIM, \
            4 * db + (n_q + 2 * N_KV_HEADS) * HEAD_DIM
        za, zb, zq, zk, zv, zd = z[:, :o1], z[:, o1:o2], z[:, o2:o3], z[:, o3:o4], z[:, o4:o5], z[:, o5:]
        b_full = jnp.repeat(gmlp_bs[l].T, gw, axis=1)
        gmlp_params = [("full", col(gmlp_ln_g[l])), ("full", col(gmlp_ln_b[l])), ("full", gmlp_ws[l]), ("full", b_full)]
        br_a = rw(_gmlp, [za], gmlp_params, [(db, BF16)], name="gmlp")[0]
        conv_params = (conv_w_full[l], col(conv_b[l]), col(conv_ln_g[l]), col(conv_ln_b[l]))
        br_b, y_conv, yc_conv = _conv_fwd(zb, *conv_params, n_lat=n_lat)
        q4, k3, v3 = to_heads(zq, n_q), to_heads(zk, N_KV_HEADS)[:, 0], to_heads(zv, N_KV_HEADS)[:, 0]
        sink_rows = sink_rows_of(l)
        br_c = from_heads(_attn_fwd(q4, k3, v3, sink_rows, cos, sin_signed, perm, s_len=s_len))
        disc_in = [s5_a_re[l].reshape(2, gp, 1), s5_a_im[l].reshape(2, gp, 1),
                   jnp.broadcast_to(s5_log_step[l][:, :, None], (2, g5, S5_STATE)).reshape(2, gp, 1),
                   s5_b_re[l].reshape(gp, S5_GW), s5_b_im[l].reshape(gp, S5_GW)]
        l_re, l_im, bb_re, bb_im = _whole(
            _s5_discretise, disc_in, [((2, gp, 1), F32)] * 2 + [((2, gp, S5_GW), F32)] * 2, name="s5_disc")
        lbar = jnp.stack([l_re[0], l_im[0], l_re[1], l_im[1]]).reshape(4, ch, LANES)
        bd_of = lambda a: _block_diag(a.reshape(g5, S5_STATE, S5_GW).transpose(0, 2, 1))
        cd_of = lambda a: _block_diag(a.transpose(0, 2, 1))
        bd = [jnp.concatenate([bd_of(bb_re[k]), bd_of(bb_im[k])], axis=1).astype(BF16) for k in range(2)]
        cd = [jnp.concatenate([cd_of(s5_c_re[l, k]), -cd_of(s5_c_im[l, k])], axis=0).astype(BF16) for k in range(2)]
        bu_f = _matmul(zd, bd[0], name="s5_in_f")
        bu_b = _matmul(zd, bd[1], name="s5_in_b")
        st_f, st_b = _s5_scan(slab(bu_f), slab(bu_b), lbar, n_lat=n_lat)
        ys = _matmul(flat2(st_f), cd[0], name="s5_out_f")
        ys = _matmul(flat2(st_b), cd[1], tiles=[ys], epi=add_epi, name="s5_out_b")
        d_skip = col(s5_d[l])
        yg = rw(_s5_act, [ys, zd], [("full", d_skip)], [(db, BF16)], name="s5_act")[0]
        glu_pre = _matmul(yg, w["s5_w_glu"], name="s5_glu_proj")
        br_d = rw(_glu_gate, [glu_pre], [], [(db, BF16)], name="s5_glu")[0]
        branches = (br_a, br_b, br_c, br_d)
        gates = _matmul(h, w["w_gate"], name="gate_proj", **gate_fwd)
        projs = [_matmul(branches[k], w["w_branch"][k], name="branch_proj") for k in range(N_BRANCH)]
        merged = rw(_merge, [gates] + projs, [("full", b_gate_full[l])], [(d, BF16)], name="merge", tm=ROW_TILE // 2)[0]
        o = _matmul(merged, w["w_out"], name="out_proj")
        x1 = rw(_resid, [xa, o], [("mod", gate1)], [(d, F32)], name="resid1")[0]
        h2 = rw(_rms_mod, [x1], [("full", n2g), ("mod", scale2), ("mod", shift2)], [(d, BF16)], name="norm2")[0]
        f1, act = _matmul(h2, w["w_ff1"], out_dtypes=(F32, BF16), name="ff1",
                          epi=lambda acc: (acc, jnp.square(jnp.maximum(acc, 0.0))), **ff1_fwd)
        o_ff = _matmul(act, w["w_ff2"], name="ff2")
        x2 = rw(_resid, [x1, o_ff], [("mod", gate2)], [(d, F32)], name="resid2")[0]
        saved.append(dict(
            w=w, mods=(shift1, scale1, gate1, shift2, scale2, gate2), n1g=n1g, n2g=n2g, xa=xa, h=h, z=z,
            gmlp_params=gmlp_params, conv_params=conv_params, y_conv=y_conv, yc_conv=yc_conv, q4=q4, k3=k3, v3=v3,
            sink_rows=sink_rows, disc_in=disc_in, lbar=lbar, bd=bd, cd=cd, st_f=st_f, st_b=st_b, ys=ys,
            d_skip=d_skip, yg=yg, glu_pre=glu_pre, branches=branches, gates=gates, projs=projs, merged=merged,
            o=o, x1=x1, h2=h2, f1=f1, act=act, o_ff=o_ff))
        xa = x2

    loss_sum, dx_lat, d_final_g = _loss_and_grad(xa, loss_target[0], col(final_g), s_len=s_len)
    loss = lax.psum(loss_sum[0, 0], ("x", "y", "c"))
    dxa = jnp.concatenate([dx_lat, jnp.zeros((n_ctx, d), F32)], axis=0)

    big_grads = [None] * n_layers
    small_grads = [None] * n_layers
    d_mods = [None] * n_layers
    o1, o2, o3, o4, o5 = 2 * db, 4 * db, 4 * db + n_q * HEAD_DIM, 4 * db + (n_q + N_KV_HEADS) * HEAD_DIM, \
        4 * db + (n_q + 2 * N_KV_HEADS) * HEAD_DIM
    for l in reversed(range(n_layers)):
        sv = saved[l]
        w = sv["w"]
        shift1, scale1, gate1, shift2, scale2, gate2 = sv["mods"]
        z = sv["z"]
        za, zb, zd = z[:, :o1], z[:, o1:o2], z[:, o5:]
        bg, sg = {}, {}
        (d_x1, d_off), (d_gate2,) = rwv(_resid, [sv["x1"], sv["o_ff"]], [("mod", gate2)], [dxa],
                                        row_grads=[F32, BF16], name="resid2_b")
        d_f1 = _matmul(d_off, w["w_ff2"], tb=True, tiles=[sv["f1"]], out_dtypes=(BF16,), name="ff2_b",
                       epi=lambda acc, f: (acc * (2.0 * jnp.maximum(f, 0.0)),))
        bg["w_ff2"] = _matmul(sv["act"], d_off, ta=True, name="ff2_w").reshape(N_CHIPS, f_sh, d)
        d_h2 = _matmul(d_f1, w["w_ff1"], name="ff1_b", **ff1_bwd)
        bg["w_ff1"] = _matmul(sv["h2"], d_f1, name="ff1_w", **ff1_wgt)
        (d_x1,), (sg["norm2_g"], d_scale2, d_shift2) = rwv(
            _rms_mod, [sv["x1"]], [("full", sv["n2g"]), ("mod", scale2), ("mod", shift2)], [d_h2],
            row_grads=[F32], adds={0: d_x1}, name="norm2_b")
        (d_xa, d_o), (d_gate1,) = rwv(_resid, [sv["xa"], sv["o"]], [("mod", gate1)], [d_x1],
                                      row_grads=[F32, BF16], name="resid1_b")
        d_merged = _matmul(d_o, w["w_out"], tb=True, name="out_b")
        bg["w_out"] = _matmul(sv["merged"], d_o, ta=True, name="out_w").reshape(N_CHIPS, d_sh, d)
        d_parts, (d_bg,) = rwv(_merge, [sv["gates"]] + sv["projs"], [("full", b_gate_full[l])], [d_merged],
                               row_grads=[BF16] * (1 + N_BRANCH), name="merge_b", tm=ROW_TILE // 4)
        sg["b_gate"] = d_bg.reshape(N_BRANCH, d)
        d_gates, d_projs = d_parts[0], d_parts[1:]
        bg["w_gate"] = _matmul(sv["h"], d_gates, name="gate_w", **gate_wgt)
        d_h = _matmul(d_gates, w["w_gate"], name="gate_b", **gate_bwd)
        d_br = [_matmul(d_projs[k], w["w_branch"][k], tb=True, name="branch_b") for k in range(N_BRANCH)]
        bg["w_branch"] = _to_shards("w_branch", jnp.stack([
            _matmul(sv["branches"][k], d_projs[k], ta=True, name="branch_w") for k in range(N_BRANCH)]))
        (d_glu_pre,), _ = rwv(_glu_gate, [sv["glu_pre"]], [], [d_br[3]], row_grads=[BF16], name="s5_glu_b")
        d_yg = _matmul(d_glu_pre, w["s5_w_glu"], tb=True, name="s5_glu_proj_b")
        bg["s5_w_glu"] = _to_shards("s5_w_glu", _matmul(sv["yg"], d_glu_pre, ta=True, name="s5_glu_proj_w"))
        (d_ys, d_zd), (d_dskip,) = rwv(_s5_act, [sv["ys"], zd], [("full", sv["d_skip"])], [d_yg],
                                       row_grads=[BF16, F32], name="s5_act_b")
        sg["s5_d"] = d_dskip.reshape(db)
        cd, bd = sv["cd"], sv["bd"]
        st2 = [flat2(sv["st_f"]), flat2(sv["st_b"])]
        d_st = [_matmul(d_ys, cd[k], tb=True, name="s5_out_b%d" % k) for k in range(2)]
        d_cd = [_matmul(st2[k], d_ys, ta=True, name="s5_out_w%d" % k) for k in range(2)]
        d_bu_f, d_bu_b, d_lbar = _s5_scan_bwd(slab(d_st[0]), slab(d_st[1]), sv["st_f"], sv["st_b"], sv["lbar"],
                                              n_lat=n_lat)
        d_bu = [flat2(d_bu_f), flat2(d_bu_b)]
        d_zd = _matmul(d_bu[0], bd[0], tb=True, tiles=[d_zd], epi=add_epi, name="s5_in_b0")
        d_zd = _matmul(d_bu[1], bd[1], tb=True, tiles=[d_zd], epi=add_epi, name="s5_in_b1")
        d_bd = [_matmul(zd, d_bu[k], ta=True, name="s5_in_w%d" % k) for k in range(2)]
        blk_c = lambda a: _diag_blocks(a, g5).transpose(0, 2, 1)
        sg["s5_c_re"] = jnp.stack([blk_c(d_cd[k][:gp]) for k in range(2)])
        sg["s5_c_im"] = jnp.stack([-blk_c(d_cd[k][gp:]) for k in range(2)])
        blk_b = lambda a: _diag_blocks(a, g5).transpose(0, 2, 1).reshape(gp, S5_GW)
        d_bb_re = jnp.stack([blk_b(d_bd[k][:, :gp]) for k in range(2)])
        d_bb_im = jnp.stack([blk_b(d_bd[k][:, gp:]) for k in range(2)])
        d_lb = d_lbar.reshape(4, gp, 1)
        disc_ct = [jnp.stack([d_lb[0], d_lb[2]]), jnp.stack([d_lb[1], d_lb[3]]), d_bb_re, d_bb_im]
        d_are, d_aim, d_ls, d_bre, d_bim = _whole_vjp(_s5_discretise, sv["disc_in"], disc_ct, name="s5_disc_b")
        sg["s5_a_re"], sg["s5_a_im"] = d_are.reshape(2, g5, S5_STATE), d_aim.reshape(2, g5, S5_STATE)
        sg["s5_log_step"] = d_ls.reshape(2, g5, S5_STATE).sum(axis=-1)
        sg["s5_b_re"], sg["s5_b_im"] = d_bre.reshape(g5, S5_STATE, S5_GW), d_bim.reshape(g5, S5_STATE, S5_GW)
        d_o4 = to_heads(d_br[2], n_q)
        d_q4, d_k3, d_v3, d_sink_rows = _attn_bwd(sv["q4"], sv["k3"], sv["v3"], sv["sink_rows"], cos, sin_signed,
                                                  perm, d_o4, s_len=s_len)
        sg["attn_sink"] = d_sink_rows.reshape(n_q, BLOCK).sum(axis=-1)
        d_zq = from_heads(d_q4)
        d_zk, d_zv = from_heads(d_k3[:, None]), from_heads(d_v3[:, None])
        cw, cb, clg, clb = sv["conv_params"]
        (d_yc,), (sg["conv_ln_g"], sg["conv_ln_b"]) = rwv(
            _ln_silu, [sv["yc_conv"]], [("full", clg), ("full", clb)], [d_br[1]], row_grads=[F32], name="conv_ln_b")
        d_zb, sg["conv_w"], sg["conv_b"] = _conv_bwd(d_yc, sv["y_conv"], zb, cw, n_lat=n_lat)
        (d_za,), (sg["gmlp_ln_g"], sg["gmlp_ln_b"], sg["gmlp_ws"], d_bfull) = rwv(
            _gmlp, [za], sv["gmlp_params"], [d_br[0]], row_grads=[F32], name="gmlp_b")
        sg["gmlp_bs"] = d_bfull.reshape(CHUNK, GMLP_GROUPS, gw).sum(axis=-1).T
        d_z = jnp.concatenate([d_za, d_zb, d_zq, d_zk, d_zv, d_zd], axis=1).astype(BF16)
        d_h = _matmul(d_z, w["w_in"], tb=True, tiles=[d_h], epi=add_epi, name="in_b")
        bg["w_in"] = _to_shards("w_in", _matmul(sv["h"], d_z, ta=True, name="in_w"))
        (dxa,), (sg["norm1_g"], d_scale1, d_shift1) = rwv(
            _rms_mod, [sv["xa"]], [("full", sv["n1g"]), ("mod", scale1), ("mod", shift1)], [d_h],
            row_grads=[F32], adds={0: d_xa}, name="norm1_b")
        d_mods[l] = jnp.concatenate([d_shift1, d_scale1, d_gate1, d_shift2, d_scale2, d_gate2], axis=-1)[:, 0]
        big_grads[l], small_grads[l] = bg, sg
        saved[l] = None

    grad_x = dxa[:s_len][None]

    dm_loc = jnp.stack(d_mods).reshape(n_layers * 2, N_MOD * d)
    dm_pad = -(-dm_loc.shape[0] // SUBLANES) * SUBLANES
    dm_all = _allgather8(jnp.pad(dm_loc, ((0, dm_pad - dm_loc.shape[0]), (0, 0))), name="gather_dmod")
    dm_all = dm_all.reshape(N_DEV, dm_pad, N_MOD * d)[:, :n_layers * 2].reshape(N_DEV, n_layers, 2, N_MOD * d)
    dm16 = dm_all.transpose(1, 2, 0, 3).reshape(n_layers, 2 * N_DEV, N_MOD * d)
    dm16_sh = lax.dynamic_slice_in_dim(dm16, j_me * ncol, ncol, axis=2)
    col_sum = lambda a: (jnp.sum(a, axis=0, keepdims=True),)
    grads = {}
    grads["w_mod"] = jnp.stack([_matmul(cond16, dm16_sh[l], ta=True, tk=2 * N_DEV, name="mod_w")
                                for l in range(n_layers)])
    grads["b_mod"] = jnp.concatenate([
        _whole(col_sum, [dm16[l]], [((1, N_MOD * d), F32)], name="mod_bias_g")[0] for l in range(n_layers)], axis=0)
    d_cond = _matmul(dm16_sh[0], w_mod[0], tb=True, tm=2 * N_DEV, name="mod_b")
    for l in range(1, n_layers):
        d_cond = _matmul(dm16_sh[l], w_mod[l], tb=True, tm=2 * N_DEV, tiles=[d_cond], epi=add_epi, name="mod_b_acc")
    d_c16 = _whole_vjp(silu_fn, [c16], [d_cond], name="silu_c_b")[0]
    d_cctx_part = jnp.where(ac == 0, d_c16[N_DEV:].sum(axis=0), 0.0)

    for n in BIG:
        grads[n] = [None] * n_layers
    for l in range(n_layers):
        for n, g in zip(BIG, _reduce_scatter([big_grads[l][n] for n in BIG])):
            grads[n][l] = g
    for n in BIG:
        grads[n] = jnp.stack(grads[n])

    small_shapes = {n: small_grads[0][n].shape for n in SMALL}
    for n in SMALL:
        grads[n] = [None] * n_layers
    for l in range(n_layers):
        extra = [d_final_g.reshape(-1), d_cctx_part.reshape(-1)] if l == 0 else []
        flat = pad_rows(jnp.concatenate([small_grads[l][n].reshape(-1) for n in SMALL] + extra))
        red = _allgather8(flat, reduce=True, name="reduce_small").reshape(-1)
        off = 0
        for n in SMALL:
            sz = math.prod(small_shapes[n])
            grads[n][l] = red[off:off + sz].reshape(small_shapes[n])
            off += sz
        if l == 0:
            grads["final_g"] = red[off:off + d]
            grads["c_ctx"] = red[off + d:off + 2 * d]
    for n in SMALL:
        g_full = jnp.stack(grads[n])
        if n in SMALL_SHARDED:
            width = p[n].shape[-1]
            g_full = lax.dynamic_slice_in_dim(g_full, j_me * width, width, axis=g_full.ndim - 1)
        grads[n] = g_full.reshape(p[n].shape)

    delta, new_m, new_v = {}, {}, {}
    large = BIG + ("w_mod",)
    for n in large:
        delta[n], new_m[n], new_v[n] = _adamw_call(p[n], grads[n], mom[n], var[n], name="adamw")
    small = [n for n in WEIGHT_NAMES if n not in large]
    pack = lambda src: pad_rows(jnp.concatenate([src[n].reshape(-1) for n in small]))
    outs = _adamw_call(pack(p), pack(grads), pack(mom), pack(var), name="adamw_small")
    off = 0
    for n in small:
        sz = p[n].size
        delta[n], new_m[n], new_v[n] = (o.reshape(-1)[off:off + sz].reshape(p[n].shape) for o in outs)
        off += sz

    return (loss, grad_x, *[grads[n] for n in WEIGHT_NAMES], *[delta[n] for n in WEIGHT_NAMES],
            *[new_m[n] for n in WEIGHT_NAMES], *[new_v[n] for n in WEIGHT_NAMES])
```

```python
import functools
import math

import jax
import jax.numpy as jnp
from jax import lax
from jax.experimental import pallas as pl
from jax.experimental.pallas import tpu as pltpu

F32 = jnp.float32
BF16 = jnp.bfloat16
MESH = pl.DeviceIdType.MESH

V7X_VMEM_BYTES = 64 * 1024 * 1024
VMEM_LIMIT = V7X_VMEM_BYTES - 8 * 1024 * 1024
LANES = 128
SUBLANES = 8

N_BRANCH = 4
CHUNK = 128
GMLP_GROUPS = 4
CONV_W = 31
CONV_HALO = 16
HEAD_DIM = 64
N_KV_HEADS = 2
WINDOW = 128
BLOCK = 128
ROPE_BASE = 10000.0
GRID_W = 64
S5_GW = 16
S5_STATE = 64
N_MOD = 6
EPS = 1e-6
NEG_INF = -1e30
ADAM_LR = 0.001
ADAM_B1 = 0.9
ADAM_B2 = 0.999
ADAM_EPS = 1e-08
ADAM_WD = 0.01
ADAM_STEP = 10

ROW_TILE = 256


def _cparams(sem=None, **kw):
    if sem is not None:
        kw["dimension_semantics"] = sem
    return pltpu.CompilerParams(vmem_limit_bytes=VMEM_LIMIT, **kw)


def _pick(n, cands):
    for c in cands:
        if n % c == 0:
            return c
    return n


def _bdot(a, b, ca, cb):
    return lax.dot_general(a.astype(BF16), b.astype(BF16), (((ca,), (cb,)), ((), ())),
                           preferred_element_type=F32)


@jax.custom_vjp
def _mm_nn(a, b):
    return _bdot(a, b, 1, 0)


def _mm_nn_fwd(a, b):
    return _bdot(a, b, 1, 0), (a, b)


def _mm_nn_bwd(res, g):
    a, b = res
    return _bdot(g, b, 1, 1).astype(a.dtype), _bdot(a, g, 0, 0).astype(b.dtype)


_mm_nn.defvjp(_mm_nn_fwd, _mm_nn_bwd)


@jax.custom_vjp
def _mm_nt(a, b):
    return _bdot(a, b, 1, 1)


def _mm_nt_fwd(a, b):
    return _bdot(a, b, 1, 1), (a, b)


def _mm_nt_bwd(res, g):
    a, b = res
    return _bdot(g, b, 1, 0).astype(a.dtype), _bdot(g, a, 0, 0).astype(b.dtype)


_mm_nt.defvjp(_mm_nt_fwd, _mm_nt_bwd)


def _matmul(a, b, *, ta=False, tb=False, tm=None, tn=None, tk=None, name,
            out_dtypes=(F32,), epi=None, tiles=(), rowvecs=(), b_blocks=None, out_blocks=None):
    m, k = (a.shape[1], a.shape[0]) if ta else a.shape
    if b_blocks is None:
        k2, n = (b.shape[1], b.shape[0]) if tb else b.shape
        assert k == k2, (a.shape, b.shape, ta, tb)
    else:
        n = b_blocks[0]
    tm = tm or _pick(m, (1024, 512, 256, 128) if ta else (768, 512, 384, 256, 128))
    tn = tn or _pick(n, (2048, 1664, 1024, 512, 256, 128) if ta else (1024, 1664, 512, 256, 128))
    tk = tk or _pick(k, (2048, 1024, 768, 512, 384, 256, 128))
    assert m % tm == 0 and n % tn == 0 and k % tk == 0, (m, n, k, tm, tn, tk)
    nk = k // tk
    n_t, n_v, n_o = len(tiles), len(rowvecs), len(out_dtypes)
    ca, cb = (0 if ta else 1), (1 if tb else 0)

    def body(*refs):
        a_ref, b_ref = refs[:2]
        t_refs = refs[2:2 + n_t]
        v_refs = refs[2 + n_t:2 + n_t + n_v]
        o_refs = refs[2 + n_t + n_v:2 + n_t + n_v + n_o]

        def finish(acc):
            outs = (acc,) if epi is None else epi(acc, *[t[...] for t in t_refs], *[v[...] for v in v_refs])
            for o_ref, o in zip(o_refs, outs):
                o_ref[...] = o.astype(o_ref.dtype)

        b_tile = b_ref[...]
        if b_tile.ndim == 3:
            b_tile = b_tile.reshape(b_tile.shape[0] * b_tile.shape[1], b_tile.shape[2])
        part = _bdot(a_ref[...], b_tile, ca, cb)
        if nk == 1:
            finish(part)
        else:
            acc_ref = refs[-1]
            kk = pl.program_id(2)

            @pl.when(kk == 0)
            def _():
                acc_ref[...] = part

            @pl.when(kk > 0)
            def _():
                acc_ref[...] += part

            @pl.when(kk == nk - 1)
            def _():
                finish(acc_ref[...])

    a_spec = pl.BlockSpec((tk, tm), lambda i, j, kk: (kk, i)) if ta else pl.BlockSpec((tm, tk), lambda i, j, kk: (i, kk))
    b_spec = pl.BlockSpec((tn, tk), lambda i, j, kk: (j, kk)) if tb else pl.BlockSpec((tk, tn), lambda i, j, kk: (kk, j))
    if b_blocks is not None:
        b_spec = pl.BlockSpec(b_blocks[1], b_blocks[2])
    in_specs = [a_spec, b_spec]
    in_specs += [pl.BlockSpec((tm, tn), lambda i, j, kk: (i, j)) for _ in tiles]
    in_specs += [pl.BlockSpec((1, tn), lambda i, j, kk: (0, j)) for _ in rowvecs]
    out_specs = [pl.BlockSpec((tm, tn), lambda i, j, kk: (i, j)) for _ in out_dtypes]
    out_shape = [jax.ShapeDtypeStruct((m, n), d) for d in out_dtypes]
    if out_blocks is not None:
        out_specs = [pl.BlockSpec(out_blocks[1], out_blocks[2])]
        out_shape = [jax.ShapeDtypeStruct(out_blocks[0], out_dtypes[0])]
    out = pl.pallas_call(
        body, name=name,
        grid=(m // tm, n // tn, nk),
        in_specs=in_specs,
        out_specs=out_specs,
        out_shape=out_shape,
        scratch_shapes=[pltpu.VMEM((tm, tn), F32)] if nk > 1 else [],
        compiler_params=_cparams(("parallel", "parallel", "arbitrary")),
    )(a, b, *tiles, *rowvecs)
    return out[0] if n_o == 1 else tuple(out)


def _param_spec(kind, p, n_lat):
    if kind == "mod":
        return pl.BlockSpec((None,) + p.shape[1:], lambda i: (i // n_lat,) + (0,) * (p.ndim - 1))
    return pl.BlockSpec(p.shape, lambda i: (0,) * p.ndim)


def _rowwise(fn, rows, params, outs, *, name, tm=ROW_TILE, s_rows=None):
    r = rows[0].shape[0]
    assert r % tm == 0, (r, tm)
    n_lat = r // tm + 1 if s_rows is None else s_rows // tm
    n_r, n_p = len(rows), len(params)

    def body(*refs):
        vals = [x[...] for x in refs[:n_r + n_p]]
        res = fn(*vals)
        for o_ref, o in zip(refs[n_r + n_p:], res):
            o_ref[...] = o.astype(o_ref.dtype)

    out = pl.pallas_call(
        body, name=name, grid=(r // tm,),
        in_specs=[pl.BlockSpec((tm, x.shape[1]), lambda i: (i, 0)) for x in rows]
        + [_param_spec(kind, p, n_lat) for kind, p in params],
        out_specs=[pl.BlockSpec((tm, w), lambda i: (i, 0)) for w, _ in outs],
        out_shape=[jax.ShapeDtypeStruct((r, w), d) for w, d in outs],
        compiler_params=_cparams(("parallel",)),
    )(*rows, *[p for _, p in params])
    return tuple(out)


def _rowwise_vjp(fn, rows, params, cts, *, name, row_grads, tm=ROW_TILE, s_rows=None, adds=None):
    r = rows[0].shape[0]
    assert r % tm == 0, (r, tm)
    n_lat = r // tm + 1 if s_rows is None else s_rows // tm
    adds = adds or {}
    n_r, n_p, n_c, n_a = len(rows), len(params), len(cts), len(adds)
    want = [i for i, d in enumerate(row_grads) if d is not None]
    add_at = {idx: k for k, idx in enumerate(sorted(adds))}

    def body(*refs):
        i = pl.program_id(0)
        prim = [x[...].astype(F32) for x in refs[:n_r + n_p]]
        ct = [x[...].astype(F32) for x in refs[n_r + n_p:n_r + n_p + n_c]]
        a_refs = refs[n_r + n_p + n_c:n_r + n_p + n_c + n_a]
        o_refs = refs[n_r + n_p + n_c + n_a:]
        _, vjp = jax.vjp(fn, *prim)
        grads = vjp(tuple(ct))
        for o_ref, idx in zip(o_refs[:len(want)], want):
            g = grads[idx] if idx not in add_at else grads[idx] + a_refs[add_at[idx]][...]
            o_ref[...] = g.astype(o_ref.dtype)
        for o_ref, (kind, _), g in zip(o_refs[len(want):], params, grads[n_r:]):
            first = (i == 0) | (i == n_lat) if kind == "mod" else (i == 0)

            @pl.when(first)
            def _(o_ref=o_ref, g=g):
                o_ref[...] = g

            @pl.when(jnp.logical_not(first))
            def _(o_ref=o_ref, g=g):
                o_ref[...] += g

    out = pl.pallas_call(
        body, name=name, grid=(r // tm,),
        in_specs=[pl.BlockSpec((tm, x.shape[1]), lambda i: (i, 0)) for x in rows]
        + [_param_spec(kind, p, n_lat) for kind, p in params]
        + [pl.BlockSpec((tm, c.shape[1]), lambda i: (i, 0)) for c in cts]
        + [pl.BlockSpec((tm, adds[idx].shape[1]), lambda i: (i, 0)) for idx in sorted(adds)],
        out_specs=[pl.BlockSpec((tm, rows[idx].shape[1]), lambda i: (i, 0)) for idx in want]
        + [_param_spec(kind, p, n_lat) for kind, p in params],
        out_shape=[jax.ShapeDtypeStruct(rows[idx].shape, row_grads[idx]) for idx in want]
        + [jax.ShapeDtypeStruct(p.shape, F32) for _, p in params],
        compiler_params=_cparams(("arbitrary",)),
    )(*rows, *[p for _, p in params], *cts, *[adds[idx] for idx in sorted(adds)])
    return tuple(out[:len(want)]), tuple(out[len(want):])


def _rms_mod(x, g, scale, shift):
    y = x * lax.rsqrt(jnp.mean(x * x, axis=-1, keepdims=True) + EPS)
    return ((y * g) * (1.0 + scale) + shift,)


def _resid(x, o, gate):
    return (x + gate * o,)


def _layer_norm(x, g, b):
    xc = x - jnp.mean(x, axis=-1, keepdims=True)
    var = jnp.mean(xc * xc, axis=-1, keepdims=True)
    return xc * lax.rsqrt(var + EPS) * g + b


def _gmlp(za, ln_g, ln_b, ws, b_full):
    db = za.shape[1] // 2
    gw = db // GMLP_GROUPS
    za = jax.nn.gelu(za)
    u, v = za[:, :db], za[:, db:]
    v = _layer_norm(v, ln_g, ln_b)
    chunks = []
    for n in range(za.shape[0] // CHUNK):
        vn = v[n * CHUNK:(n + 1) * CHUNK]
        cols = [_mm_nn(ws[g], vn[:, g * gw:(g + 1) * gw]) for g in range(GMLP_GROUPS)]
        chunks.append(jnp.concatenate(cols, axis=1) + b_full)
    mixed = chunks[0] if len(chunks) == 1 else jnp.concatenate(chunks, axis=0)
    return (u * mixed,)


def _glu_gate(zb):
    db = zb.shape[1] // 2
    return (zb[:, :db] * jax.nn.sigmoid(zb[:, db:]),)


def _ln_silu(yc, g, b):
    return (jax.nn.silu(_layer_norm(yc, g, b)),)


def _s5_act(ys, u, d_skip):
    return (jax.nn.gelu(ys + d_skip * u),)


def _merge(*args):
    g, ps, b = args[0], args[1:1 + N_BRANCH], args[1 + N_BRANCH]
    d = ps[0].shape[1]
    s = jax.nn.sigmoid(g + b)
    out = s[:, :d] * ps[0]
    for k in range(1, N_BRANCH):
        out = out + s[:, k * d:(k + 1) * d] * ps[k]
    return (out,)


def _loss_rows(x, tgt, g):
    y = x * lax.rsqrt(jnp.mean(x * x, axis=-1, keepdims=True) + EPS) * g
    e = y - tgt
    return (0.5 * jnp.mean(e * e, axis=-1, keepdims=True),)


def _adamw(w, g, m, v):
    m = ADAM_B1 * m + (1.0 - ADAM_B1) * g
    v = ADAM_B2 * v + (1.0 - ADAM_B2) * jnp.square(g)
    m_hat = m / (1.0 - ADAM_B1 ** ADAM_STEP)
    v_hat = v / (1.0 - ADAM_B2 ** ADAM_STEP)
    delta = -ADAM_LR * (m_hat / (jnp.sqrt(v_hat) + ADAM_EPS) + ADAM_WD * w)
    return delta, m, v


def _whole(fn, args, outs, *, name):
    n_a = len(args)

    def body(*refs):
        res = fn(*[x[...] for x in refs[:n_a]])
        for o_ref, o in zip(refs[n_a:], res):
            o_ref[...] = o.astype(o_ref.dtype)

    return tuple(pl.pallas_call(
        body, name=name,
        out_shape=[jax.ShapeDtypeStruct(s, d) for s, d in outs],
        compiler_params=_cparams(),
    )(*args))


def _whole_vjp(fn, args, cts, *, name):
    n_a, n_c = len(args), len(cts)

    def body(*refs):
        prim = [x[...] for x in refs[:n_a]]
        ct = [x[...] for x in refs[n_a:n_a + n_c]]
        _, vjp = jax.vjp(fn, *prim)
        for o_ref, g in zip(refs[n_a + n_c:], vjp(tuple(ct))):
            o_ref[...] = g

    return tuple(pl.pallas_call(
        body, name=name,
        out_shape=[jax.ShapeDtypeStruct(a.shape, F32) for a in args],
        compiler_params=_cparams(),
    )(*args, *cts))


def _conv_flags(i, n_lat, n_tiles):
    has_prev = jnp.logical_and(i != 0, i != n_lat)
    has_next = jnp.logical_and(i != n_lat - 1, i != n_tiles - 1)
    return has_prev, has_next


def _halo_specs(width, tm, n_rows):
    per = tm // CONV_HALO
    last = n_rows // CONV_HALO - 1
    prev = pl.BlockSpec((CONV_HALO, width), lambda i: (jnp.maximum(i * per - 1, 0), 0))
    cur = pl.BlockSpec((tm, width), lambda i: (i, 0))
    nxt = pl.BlockSpec((CONV_HALO, width), lambda i: (jnp.minimum((i + 1) * per, last), 0))
    return [prev, cur, nxt]


def _with_halo(prev, cur, nxt, has_prev, has_next):
    prev = jnp.where(has_prev, prev, 0.0)
    nxt = jnp.where(has_next, nxt, 0.0)
    return jnp.concatenate([prev, cur, nxt], axis=0)


def _conv_fwd(zb, conv_w, conv_b, ln_g, ln_b, *, n_lat, tm=ROW_TILE):
    r, db = zb.shape[0], zb.shape[1] // 2
    n_tiles = r // tm
    half = CONV_W // 2

    def body(zp_ref, zc_ref, zn_ref, w_ref, b_ref, g_ref, lb_ref, out_ref, y_ref, yc_ref):
        i = pl.program_id(0)
        has_prev, has_next = _conv_flags(i, n_lat, n_tiles)
        y = _glu_gate(zc_ref[...])[0]
        y_ext = _with_halo(_glu_gate(zp_ref[...])[0], y, _glu_gate(zn_ref[...])[0], has_prev, has_next)
        w = w_ref[...]
        acc = jnp.zeros((tm, db), F32) + b_ref[...]
        for k in range(CONV_W):
            s = CONV_HALO - half + k
            acc = acc + w[k:k + 1, :] * y_ext[s:s + tm, :]
        y_ref[...] = y
        yc_ref[...] = acc
        out_ref[...] = _ln_silu(acc, g_ref[...], lb_ref[...])[0].astype(out_ref.dtype)

    full = lambda p: pl.BlockSpec(p.shape, lambda i: (0,) * p.ndim)
    return pl.pallas_call(
        body, name="conv_fwd", grid=(n_tiles,),
        in_specs=_halo_specs(2 * db, tm, r) + [full(conv_w), full(conv_b), full(ln_g), full(ln_b)],
        out_specs=[pl.BlockSpec((tm, db), lambda i: (i, 0))] * 3,
        out_shape=[jax.ShapeDtypeStruct((r, db), BF16), jax.ShapeDtypeStruct((r, db), F32),
                   jax.ShapeDtypeStruct((r, db), F32)],
        compiler_params=_cparams(("parallel",)),
    )(zb, zb, zb, conv_w, conv_b, ln_g, ln_b)


def _conv_bwd(d_yc, y, zb, conv_w, *, n_lat, tm=ROW_TILE):
    r, db = y.shape
    n_tiles = r // tm
    half = CONV_W // 2

    def body(gp_ref, gc_ref, gn_ref, yp_ref, yc_ref, yn_ref, z_ref, w_ref, dz_ref, dw_ref, db_ref):
        i = pl.program_id(0)
        has_prev, has_next = _conv_flags(i, n_lat, n_tiles)
        g = gc_ref[...]
        g_ext = _with_halo(gp_ref[...], g, gn_ref[...], has_prev, has_next)
        y_ext = _with_halo(yp_ref[...], yc_ref[...], yn_ref[...], has_prev, has_next)
        w = w_ref[...]

        @pl.when(i == 0)
        def _():
            dw_ref[...] = jnp.zeros_like(dw_ref)
            db_ref[...] = jnp.zeros_like(db_ref)

        d_y = jnp.zeros((tm, db), F32)
        for k in range(CONV_W):
            s = CONV_HALO + half - k
            d_y = d_y + w[k:k + 1, :] * g_ext[s:s + tm, :]
            s = CONV_HALO - half + k
            dw_ref[pl.ds(k, 1), :] += jnp.sum(g * y_ext[s:s + tm, :], axis=0, keepdims=True)
        db_ref[...] += jnp.sum(g, axis=0, keepdims=True)
        _, vjp = jax.vjp(_glu_gate, z_ref[...])
        dz_ref[...] = vjp((d_y,))[0]

    return pl.pallas_call(
        body, name="conv_bwd", grid=(n_tiles,),
        in_specs=_halo_specs(db, tm, r) + _halo_specs(db, tm, r)
        + [pl.BlockSpec((tm, 2 * db), lambda i: (i, 0)), pl.BlockSpec(conv_w.shape, lambda i: (0, 0))],
        out_specs=[pl.BlockSpec((tm, 2 * db), lambda i: (i, 0)), pl.BlockSpec((CONV_W, db), lambda i: (0, 0)),
                   pl.BlockSpec((1, db), lambda i: (0, 0))],
        out_shape=[jax.ShapeDtypeStruct((r, 2 * db), F32), jax.ShapeDtypeStruct((CONV_W, db), F32),
                   jax.ShapeDtypeStruct((1, db), F32)],
        compiler_params=_cparams(("arbitrary",)),
    )(d_yc, d_yc, d_yc, y, y, y, zb, conv_w)


def _rope(x, cos, sin_signed, perm):
    return x * cos + jnp.dot(x, perm, precision=lax.Precision.HIGHEST, preferred_element_type=F32) * sin_signed


def _softmax3(s_loc, s_ctx, sink_col):
    m = sink_col
    if s_loc is not None:
        m = jnp.maximum(m, jnp.max(s_loc, axis=-1, keepdims=True))
    m = lax.stop_gradient(jnp.maximum(m, jnp.max(s_ctx, axis=-1, keepdims=True)))
    e_ctx = jnp.exp(s_ctx - m)
    den = jnp.sum(e_ctx, axis=-1, keepdims=True) + jnp.exp(sink_col - m)
    if s_loc is None:
        return None, e_ctx / den
    e_loc = jnp.exp(s_loc - m)
    den = den + jnp.sum(e_loc, axis=-1, keepdims=True)
    return e_loc / den, e_ctx / den


def _attn_latent(q4, kb, vb, kc, vc, sink_col, cq, sq, ck, sk, perm, allowed):
    qpk = q4.shape[0]
    scale = HEAD_DIM ** -0.5
    q = _rope(q4.reshape(qpk * BLOCK, HEAD_DIM), jnp.concatenate([cq] * qpk, 0), jnp.concatenate([sq] * qpk, 0), perm)
    k = _rope(kb, ck, sk, perm)
    s_loc = jnp.where(allowed, _mm_nt(q, k) * scale, NEG_INF)
    s_ctx = _mm_nt(q, kc) * scale
    p_loc, p_ctx = _softmax3(s_loc, s_ctx, sink_col)
    o = _mm_nn(p_loc, vb) + _mm_nn(p_ctx, vc)
    return o.reshape(qpk, BLOCK, HEAD_DIM)


def _attn_context(q4, kc, vc, sink_col):
    qpk = q4.shape[0]
    scale = HEAD_DIM ** -0.5
    s_ctx = _mm_nt(q4.reshape(qpk * BLOCK, HEAD_DIM), kc) * scale
    _, p_ctx = _softmax3(None, s_ctx, sink_col)
    return _mm_nn(p_ctx, vc).reshape(qpk, BLOCK, HEAD_DIM)


def _attn_specs(nkv, qpk, nq, n_ctx, s_len):
    blk = lambda off: (lambda i: (0, jnp.clip(i + off, 0, nq - 1), 0))
    tab = lambda off: (lambda i: (jnp.clip(i + off, 0, nq - 1), 0))
    q_spec = pl.BlockSpec((nkv, qpk, BLOCK, HEAD_DIM), lambda i: (0, 0, i, 0))
    band = [pl.BlockSpec((nkv, BLOCK, HEAD_DIM), blk(off)) for off in (-1, 0, 1)]
    ctx = pl.BlockSpec((nkv, n_ctx, HEAD_DIM), lambda i: (0, s_len // n_ctx, 0))
    sink = pl.BlockSpec((nkv, qpk * BLOCK, 1), lambda i: (0, 0, 0))
    tabs = [pl.BlockSpec((BLOCK, HEAD_DIM), tab(off)) for off in (-1, 0, 1)]
    perm = pl.BlockSpec((HEAD_DIM, HEAD_DIM), lambda i: (0, 0))
    return q_spec, band, ctx, sink, tabs, perm


def _attn_mask(i, qpk, s_len):
    qpos = i * BLOCK + lax.broadcasted_iota(jnp.int32, (BLOCK, 3 * BLOCK), 0)
    kpos = (i - 1) * BLOCK + lax.broadcasted_iota(jnp.int32, (BLOCK, 3 * BLOCK), 1)
    ok = (jnp.abs(qpos - kpos) <= WINDOW) & (kpos >= 0) & (kpos < s_len)
    return jnp.concatenate([ok] * qpk, axis=0)


def _attn_fwd(q, k, v, sink_rows, cos, sin_signed, perm, *, s_len):
    nkv, qpk, r, _ = q.shape
    nq, n_ctx = s_len // BLOCK, r - s_len
    q_spec, band, ctx, sink, tabs, perm_spec = _attn_specs(nkv, qpk, nq, n_ctx, s_len)

    def body(q_ref, kp, kc_, kn, vp, vc_, vn, kx, vx, sk_ref, cp, cc, cn, sp, sc, sn, perm_ref, o_ref):
        i = pl.program_id(0)

        @pl.when(i < nq)
        def _():
            allowed = _attn_mask(i, qpk, s_len)
            ck = jnp.concatenate([cp[...], cc[...], cn[...]], 0)
            sk = jnp.concatenate([sp[...], sc[...], sn[...]], 0)
            for g in range(nkv):
                kb = jnp.concatenate([kp[g], kc_[g], kn[g]], 0)
                vb = jnp.concatenate([vp[g], vc_[g], vn[g]], 0)
                o = _attn_latent(q_ref[g], kb, vb, kx[g], vx[g], sk_ref[g], cc[...], sc[...], ck, sk, perm_ref[...], allowed)
                o_ref[g] = o.astype(o_ref.dtype)

        @pl.when(i >= nq)
        def _():
            for g in range(nkv):
                o_ref[g] = _attn_context(q_ref[g], kx[g], vx[g], sk_ref[g]).astype(o_ref.dtype)

    return pl.pallas_call(
        body, name="attn_fwd", grid=(r // BLOCK,),
        in_specs=[q_spec] + band + band + [ctx, ctx, sink] + tabs + tabs + [perm_spec],
        out_specs=q_spec,
        out_shape=jax.ShapeDtypeStruct(q.shape, BF16),
        compiler_params=_cparams(("parallel",)),
    )(q, k, k, k, v, v, v, k, v, sink_rows, cos, cos, cos, sin_signed, sin_signed, sin_signed, perm)


def _attn_bwd(q, k, v, sink_rows, cos, sin_signed, perm, d_o, *, s_len):
    nkv, qpk, r, _ = q.shape
    nq, n_ctx = s_len // BLOCK, r - s_len
    n_steps = r // BLOCK
    q_spec, band, ctx, sink, tabs, perm_spec = _attn_specs(nkv, qpk, nq, n_ctx, s_len)

    def body(q_ref, kp, kc_, kn, vp, vc_, vn, kx, vx, sk_ref, cp, cc, cn, sp, sc, sn, perm_ref, do_ref,
             dq_ref, dk_hbm, dv_hbm, dsk_ref, dk_acc, dv_acc):
        i = pl.program_id(0)

        @pl.when(i == 0)
        def _():
            dk_acc[...] = jnp.zeros_like(dk_acc)
            dv_acc[...] = jnp.zeros_like(dv_acc)
            dsk_ref[...] = jnp.zeros_like(dsk_ref)

        ctx_rows = pl.ds(s_len, n_ctx)

        @pl.when(i < nq)
        def _():
            allowed = _attn_mask(i, qpk, s_len)
            ck = jnp.concatenate([cp[...], cc[...], cn[...]], 0)
            sk = jnp.concatenate([sp[...], sc[...], sn[...]], 0)
            for g in range(nkv):
                kb = jnp.concatenate([kp[g], kc_[g], kn[g]], 0)
                vb = jnp.concatenate([vp[g], vc_[g], vn[g]], 0)
                fn = lambda q4, kb_, vb_, kc, vc, s_col: _attn_latent(
                    q4, kb_, vb_, kc, vc, s_col, cc[...], sc[...], ck, sk, perm_ref[...], allowed)
                _, vjp = jax.vjp(fn, q_ref[g], kb, vb, kx[g], vx[g], sk_ref[g])
                dq4, dkb, dvb, dkc, dvc, dsk = vjp(do_ref[g].astype(F32))
                dq_ref[g] = dq4
                for seg, off in enumerate((-1, 0, 1)):
                    rows = pl.ds(pl.multiple_of(jnp.clip(i + off, 0, nq - 1) * BLOCK, BLOCK), BLOCK)
                    dk_acc[g, rows, :] += dkb[seg * BLOCK:(seg + 1) * BLOCK]
                    dv_acc[g, rows, :] += dvb[seg * BLOCK:(seg + 1) * BLOCK]
                dk_acc[g, ctx_rows, :] += dkc
                dv_acc[g, ctx_rows, :] += dvc
                dsk_ref[g] += dsk

        @pl.when(i >= nq)
        def _():
            for g in range(nkv):
                _, vjp = jax.vjp(_attn_context, q_ref[g], kx[g], vx[g], sk_ref[g])
                dq4, dkc, dvc, dsk = vjp(do_ref[g].astype(F32))
                dq_ref[g] = dq4
                dk_acc[g, ctx_rows, :] += dkc
                dv_acc[g, ctx_rows, :] += dvc
                dsk_ref[g] += dsk

        @pl.when(i == n_steps - 1)
        def _():
            pltpu.sync_copy(dk_acc, dk_hbm)
            pltpu.sync_copy(dv_acc, dv_hbm)

    any_spec = pl.BlockSpec(memory_space=pl.ANY)
    return pl.pallas_call(
        body, name="attn_bwd", grid=(n_steps,),
        in_specs=[q_spec] + band + band + [ctx, ctx, sink] + tabs + tabs + [perm_spec, q_spec],
        out_specs=[q_spec, any_spec, any_spec, sink],
        out_shape=[jax.ShapeDtypeStruct(q.shape, F32), jax.ShapeDtypeStruct(k.shape, F32),
                   jax.ShapeDtypeStruct(v.shape, F32), jax.ShapeDtypeStruct(sink_rows.shape, F32)],
        scratch_shapes=[pltpu.VMEM(k.shape, F32), pltpu.VMEM(v.shape, F32)],
        compiler_params=_cparams(("arbitrary",)),
    )(q, k, k, k, v, v, v, k, v, sink_rows, cos, cos, cos, sin_signed, sin_signed, sin_signed, perm, d_o)


def _scan_orders(n_lat, n_ctx):
    fwd = lambda i: jnp.where(i < n_ctx, n_lat + i, i - n_ctx)
    bwd = lambda i: jnp.where(i < n_ctx, n_lat + n_ctx - 1 - i, n_lat - 1 - (i - n_ctx))
    return fwd, bwd


def _s5_scan(bu_f, bu_b, lbar, *, n_lat, tb=ROW_TILE):
    r, ch2, _ = bu_f.shape
    ch = ch2 // 2
    n_tiles = r // tb
    of, ob = _scan_orders(n_lat, n_tiles - n_lat)

    def body(bf_ref, bb_ref, a_ref, sf_ref, sb_ref, st_ref):
        @pl.when(pl.program_id(0) == 0)
        def _():
            st_ref[...] = jnp.zeros_like(st_ref)

        afr, afi, abr, abi = a_ref[0], a_ref[1], a_ref[2], a_ref[3]

        def step(t, carry):
            sfr, sfi, sbr, sbi = carry
            x = bf_ref[t]
            nfr = afr * sfr - afi * sfi + x[:ch]
            nfi = afr * sfi + afi * sfr + x[ch:]
            sf_ref[t] = jnp.concatenate([nfr, nfi], axis=0)
            u = tb - 1 - t
            x = bb_ref[u]
            nbr = abr * sbr - abi * sbi + x[:ch]
            nbi = abr * sbi + abi * sbr + x[ch:]
            sb_ref[u] = jnp.concatenate([nbr, nbi], axis=0)
            return nfr, nfi, nbr, nbi

        out = lax.fori_loop(0, tb, step, (st_ref[0], st_ref[1], st_ref[2], st_ref[3]), unroll=4)
        for n in range(4):
            st_ref[n] = out[n]

    spec = lambda order: pl.BlockSpec((tb, ch2, LANES), lambda i: (order(i), 0, 0))
    return pl.pallas_call(
        body, name="s5_scan", grid=(n_tiles,),
        in_specs=[spec(of), spec(ob), pl.BlockSpec(lbar.shape, lambda i: (0, 0, 0))],
        out_specs=[spec(of), spec(ob)],
        out_shape=[jax.ShapeDtypeStruct(bu_f.shape, F32), jax.ShapeDtypeStruct(bu_b.shape, F32)],
        scratch_shapes=[pltpu.VMEM((4, ch, LANES), F32)],
        compiler_params=_cparams(("arbitrary",)),
    )(bu_f, bu_b, lbar)


def _s5_scan_bwd(ds_f, ds_b, s_f, s_b, lbar, *, n_lat, tb=ROW_TILE):
    r, ch2, _ = ds_f.shape
    ch = ch2 // 2
    n_tiles = r // tb
    of, ob = _scan_orders(n_lat, n_tiles - n_lat)
    rof = lambda i: of(n_tiles - 1 - i)
    rob = lambda i: ob(n_tiles - 1 - i)

    def body(gf_ref, gb_ref, sf_ref, sb_ref, a_ref, df_ref, db_ref, da_ref, st_ref):
        @pl.when(pl.program_id(0) == 0)
        def _():
            st_ref[...] = jnp.zeros_like(st_ref)
            da_ref[...] = jnp.zeros_like(da_ref)

        afr, afi, abr, abi = a_ref[0], a_ref[1], a_ref[2], a_ref[3]

        def one(g_re, g_im, acc_re, acc_im, a_re, a_im, s, ds):
            acc_re = acc_re + s[:ch] * g_re + s[ch:] * g_im
            acc_im = acc_im - s[ch:] * g_re + s[:ch] * g_im
            n_re = ds[:ch] + a_re * g_re + a_im * g_im
            n_im = ds[ch:] - a_im * g_re + a_re * g_im
            return n_re, n_im, acc_re, acc_im

        def step(t, carry):
            gfr, gfi, gbr, gbi, cfr, cfi, cbr, cbi = carry
            u = tb - 1 - t
            gfr, gfi, cfr, cfi = one(gfr, gfi, cfr, cfi, afr, afi, sf_ref[u], gf_ref[u])
            df_ref[u] = jnp.concatenate([gfr, gfi], axis=0)
            gbr, gbi, cbr, cbi = one(gbr, gbi, cbr, cbi, abr, abi, sb_ref[t], gb_ref[t])
            db_ref[t] = jnp.concatenate([gbr, gbi], axis=0)
            return gfr, gfi, gbr, gbi, cfr, cfi, cbr, cbi

        zero = jnp.zeros((ch, LANES), F32)
        out = lax.fori_loop(0, tb, step, (st_ref[0], st_ref[1], st_ref[2], st_ref[3], zero, zero, zero, zero), unroll=4)
        for n in range(4):
            st_ref[n] = out[n]
            da_ref[n] += out[4 + n]

    spec = lambda order: pl.BlockSpec((tb, ch2, LANES), lambda i: (order(i), 0, 0))
    return pl.pallas_call(
        body, name="s5_scan_bwd", grid=(n_tiles,),
        in_specs=[spec(rof), spec(rob), spec(rof), spec(rob), pl.BlockSpec(lbar.shape, lambda i: (0, 0, 0))],
        out_specs=[spec(rof), spec(rob), pl.BlockSpec(lbar.shape, lambda i: (0, 0, 0))],
        out_shape=[jax.ShapeDtypeStruct(ds_f.shape, F32), jax.ShapeDtypeStruct(ds_b.shape, F32),
                   jax.ShapeDtypeStruct(lbar.shape, F32)],
        scratch_shapes=[pltpu.VMEM((4, ch, LANES), F32)],
        compiler_params=_cparams(("arbitrary",)),
    )(ds_f, ds_b, s_f, s_b, lbar)


def _s5_discretise(a_re, a_im, log_step, b_re, b_im):
    dt = jnp.exp(log_step)
    mag = jnp.exp(a_re * dt)
    l_re, l_im = mag * jnp.cos(a_im * dt), mag * jnp.sin(a_im * dt)
    den = a_re * a_re + a_im * a_im
    q_re = ((l_re - 1.0) * a_re + l_im * a_im) / den
    q_im = (l_im * a_re - (l_re - 1.0) * a_im) / den
    bb_re = q_re * b_re[None] - q_im * b_im[None]
    bb_im = q_re * b_im[None] + q_im * b_re[None]
    return l_re, l_im, bb_re, bb_im


N_DEV = 8
N_CHIPS = 4


def _place():
    x, y, c = lax.axis_index("x"), lax.axis_index("y"), lax.axis_index("c")
    chips = [(1 - x, y), (x, 1 - y), (1 - x, 1 - y)]
    return x, y, c, chips


def _allgather8(v, *, reduce=False, name):
    m_per, n = v.shape

    def body(x_ref, out_ref, *scratch):
        if reduce:
            all_ref, send_sems, recv_sems, local_sem = scratch
        else:
            all_ref = out_ref
            send_sems, recv_sems, local_sem = scratch
        x, y, c, chips = _place()
        me, sibling = (x, y, c), (x, y, 1 - c)

        def rows(px, py, pc):
            return all_ref.at[pl.ds((4 * px + 2 * py + pc) * m_per, m_per), :]

        def copy(k, block, to, src=None):
            return pltpu.make_async_remote_copy(
                src_ref=rows(*block) if src is None else src, dst_ref=rows(*block),
                send_sem=send_sems.at[k], recv_sem=recv_sems.at[k], device_id=to, device_id_type=MESH)

        mine = pltpu.make_async_copy(x_ref, rows(*me), local_sem)
        mine.start()
        first = [copy(0, me, sibling, src=x_ref)]
        first += [copy(1 + j, me, (*chip, c), src=x_ref) for j, chip in enumerate(chips)]
        for cp in first:
            cp.start()
        passed = [copy(4 + j, (*chip, c), sibling) for j, chip in enumerate(chips)]
        for j, chip in enumerate(chips):
            copy(1 + j, (*chip, c), me).wait_recv()
            passed[j].start()
        copy(0, sibling, me).wait_recv()
        for j, chip in enumerate(chips):
            copy(4 + j, (*chip, 1 - c), me).wait_recv()
        for cp in first + passed:
            cp.wait_send()
        mine.wait()
        if reduce:
            acc = all_ref[pl.ds(0, m_per), :]
            for d in range(1, N_DEV):
                acc = acc + all_ref[pl.ds(d * m_per, m_per), :]
            out_ref[...] = acc

    sems = [pltpu.SemaphoreType.DMA((7,)), pltpu.SemaphoreType.DMA((7,)), pltpu.SemaphoreType.DMA]
    return pl.pallas_call(
        body, name=name,
        out_shape=jax.ShapeDtypeStruct((m_per if reduce else N_DEV * m_per, n), v.dtype),
        in_specs=[pl.BlockSpec(memory_space=pltpu.VMEM)],
        out_specs=pl.BlockSpec(memory_space=pltpu.VMEM),
        scratch_shapes=([pltpu.VMEM((N_DEV * m_per, n), v.dtype)] if reduce else []) + sems,
        compiler_params=_cparams(),
    )(v)


def _halves(ref, half):
    h = ref.shape[0] // 2
    return ref.at[pl.ds(half * h, h)]


def _gather_shards(stacks):
    n_w = len(stacks)

    def body(*refs):
        out_refs = refs[n_w:2 * n_w]
        send_sems, recv_sems = refs[2 * n_w:]
        x, y, c, chips = _place()
        me, sibling = (x, y, c), (x, y, 1 - c)
        chip_no = lambda px, py: 2 * px + py

        def copy(n, k, chip, half, to):
            blk = _halves(out_refs[n].at[chip_no(*chip)], half)
            return pltpu.make_async_remote_copy(
                src_ref=blk, dst_ref=blk, send_sem=send_sems.at[n, k], recv_sem=recv_sems.at[n, k],
                device_id=to, device_id_type=MESH)

        first = [copy(n, j, (x, y), c, (*chip, c)) for n in range(n_w) for j, chip in enumerate(chips)]
        for cp in first:
            cp.start()
        passed = []
        for n in range(n_w):
            for j, chip in enumerate(chips):
                copy(n, j, chip, c, me).wait_recv()
                passed.append(copy(n, 3 + j, chip, c, sibling))
                passed[-1].start()
        for n in range(n_w):
            for j, chip in enumerate(chips):
                copy(n, 3 + j, chip, 1 - c, me).wait_recv()
        for cp in first + passed:
            cp.wait_send()

    any_spec = pl.BlockSpec(memory_space=pl.ANY)
    return pl.pallas_call(
        body, name="gather_shards",
        out_shape=[jax.ShapeDtypeStruct(s.shape, s.dtype) for s in stacks],
        in_specs=[any_spec] * n_w, out_specs=[any_spec] * n_w,
        input_output_aliases={n: n for n in range(n_w)},
        scratch_shapes=[pltpu.SemaphoreType.DMA((n_w, 6)), pltpu.SemaphoreType.DMA((n_w, 6))],
        compiler_params=_cparams(),
    )(*stacks)


def _rs_swap_in(grads):
    n_w = len(grads)

    def body(*refs):
        x_refs, out_refs = refs[:n_w], refs[n_w:2 * n_w]
        send_sems, recv_sems = refs[2 * n_w:]
        x, y, c, _ = _place()
        cps = []
        for n in range(n_w):
            h = x_refs[n].shape[1] // 2
            cps.append(pltpu.make_async_remote_copy(
                src_ref=x_refs[n].at[:, pl.ds((1 - c) * h, h)], dst_ref=out_refs[n],
                send_sem=send_sems.at[n], recv_sem=recv_sems.at[n], device_id=(x, y, 1 - c), device_id_type=MESH))
        for cp in cps:
            cp.start()
        for cp in cps:
            cp.wait()

    any_spec = pl.BlockSpec(memory_space=pl.ANY)
    return pl.pallas_call(
        body, name="rs_swap_in",
        out_shape=[jax.ShapeDtypeStruct((g.shape[0], g.shape[1] // 2) + g.shape[2:], g.dtype) for g in grads],
        in_specs=[any_spec] * n_w, out_specs=[any_spec] * n_w,
        scratch_shapes=[pltpu.SemaphoreType.DMA((n_w,)), pltpu.SemaphoreType.DMA((n_w,))],
        compiler_params=_cparams(),
    )(*grads)


def _chip_exchange(parts):
    n_w = len(parts)

    def body(*refs):
        x_refs, out_refs = refs[:n_w], refs[n_w:2 * n_w]
        send_sems, recv_sems = refs[2 * n_w:]
        x, y, c, chips = _place()
        cps = [pltpu.make_async_remote_copy(
            src_ref=x_refs[n].at[2 * chip[0] + chip[1]], dst_ref=out_refs[n].at[j],
            send_sem=send_sems.at[n, j], recv_sem=recv_sems.at[n, j], device_id=(*chip, c), device_id_type=MESH)
            for n in range(n_w) for j, chip in enumerate(chips)]
        for cp in cps:
            cp.start()
        for cp in cps:
            cp.wait()

    any_spec = pl.BlockSpec(memory_space=pl.ANY)
    return pl.pallas_call(
        body, name="chip_exchange",
        out_shape=[jax.ShapeDtypeStruct((3,) + v.shape[1:], v.dtype) for v in parts],
        in_specs=[any_spec] * n_w, out_specs=[any_spec] * n_w,
        scratch_shapes=[pltpu.SemaphoreType.DMA((n_w, 3)), pltpu.SemaphoreType.DMA((n_w, 3))],
        compiler_params=_cparams(),
    )(*parts)


def _rs_finish(reduced):
    n_w = len(reduced)

    def body(*refs):
        out_refs = refs[n_w:2 * n_w]
        send_sems, recv_sems = refs[2 * n_w:]
        x, y, c, _ = _place()

        def copy(n, half):
            blk = _halves(out_refs[n], half)
            return pltpu.make_async_remote_copy(
                src_ref=blk, dst_ref=blk, send_sem=send_sems.at[n], recv_sem=recv_sems.at[n],
                device_id=(x, y, 1 - c), device_id_type=MESH)

        for n in range(n_w):
            copy(n, c).start()
        for n in range(n_w):
            copy(n, c).wait_send()
            copy(n, 1 - c).wait_recv()

    any_spec = pl.BlockSpec(memory_space=pl.ANY)
    return pl.pallas_call(
        body, name="rs_finish",
        out_shape=[jax.ShapeDtypeStruct(v.shape, v.dtype) for v in reduced],
        in_specs=[any_spec] * n_w, out_specs=[any_spec] * n_w,
        input_output_aliases={n: n for n in range(n_w)},
        scratch_shapes=[pltpu.SemaphoreType.DMA((n_w,)), pltpu.SemaphoreType.DMA((n_w,))],
        compiler_params=_cparams(),
    )(*reduced)


def _as_rows(shape):
    return (math.prod(shape[:-1]), shape[-1])


def _row_tile(rows, cols):
    return _pick(rows, tuple(t for t in (2048, 1024, 512, 256, 128, 64, 32, 16, 8) if t * cols * 4 <= (1 << 21)))


def _pair_sum(g, t, c_idx):
    rows, cols = _as_rows(t.shape[1:])
    tr = _row_tile(rows, cols)

    def body(c_ref, g_ref, t_ref, o_ref):
        o_ref[...] = g_ref[...] + t_ref[...]

    out = pl.pallas_call(
        body, name="rs_pair_sum",
        grid_spec=pltpu.PrefetchScalarGridSpec(
            num_scalar_prefetch=1, grid=(N_CHIPS, rows // tr),
            in_specs=[pl.BlockSpec((None, None, tr, cols), lambda j, i, c: (j, c[0], i, 0)),
                      pl.BlockSpec((None, tr, cols), lambda j, i, c: (j, i, 0))],
            out_specs=pl.BlockSpec((None, tr, cols), lambda j, i, c: (j, i, 0))),
        out_shape=jax.ShapeDtypeStruct((N_CHIPS, rows, cols), F32),
        compiler_params=_cparams(("parallel", "parallel")),
    )(c_idx, g.reshape(N_CHIPS, 2, rows, cols), t.reshape(N_CHIPS, rows, cols))
    return out.reshape(t.shape)


def _chip_sum(q, u, jc_idx):
    rows, cols = _as_rows(q.shape[1:])
    tr = _row_tile(rows, cols)
    nb = rows // tr

    def body(jc_ref, q_ref, u_ref, o_ref):
        o_ref[...] = ((q_ref[...] + u_ref[0]) + u_ref[1]) + u_ref[2]

    out = pl.pallas_call(
        body, name="rs_chip_sum",
        grid_spec=pltpu.PrefetchScalarGridSpec(
            num_scalar_prefetch=1, grid=(nb,),
            in_specs=[pl.BlockSpec((None, tr, cols), lambda i, jc: (jc[0], i, 0)),
                      pl.BlockSpec((3, tr, cols), lambda i, jc: (0, i, 0))],
            out_specs=pl.BlockSpec((tr, cols), lambda i, jc: (jc[1] * nb + i, 0))),
        out_shape=jax.ShapeDtypeStruct((2 * rows, cols), F32),
        compiler_params=_cparams(("parallel",)),
    )(jc_idx, q.reshape(N_CHIPS, rows, cols), u.reshape(3, rows, cols))
    return out.reshape((2 * q.shape[1],) + q.shape[2:])


def _reduce_scatter(grads):
    x, y, c = lax.axis_index("x"), lax.axis_index("y"), lax.axis_index("c")
    c_idx = jnp.reshape(c, (1,)).astype(jnp.int32)
    jc_idx = jnp.stack([2 * x + y, c]).astype(jnp.int32)
    theirs = _rs_swap_in(grads)
    pair = [_pair_sum(g, t, c_idx) for g, t in zip(grads, theirs)]
    got = _chip_exchange(pair)
    reduced = [_chip_sum(q, u, jc_idx) for q, u in zip(pair, got)]
    return _rs_finish(reduced)


WEIGHT_NAMES = ('c_ctx', 'w_mod', 'b_mod', 'norm1_g', 'norm2_g', 'w_in', 'gmlp_ln_g', 'gmlp_ln_b', 'gmlp_ws',
                'gmlp_bs', 'conv_w', 'conv_b', 'conv_ln_g', 'conv_ln_b', 'attn_sink', 's5_a_re', 's5_a_im',
                's5_log_step', 's5_b_re', 's5_b_im', 's5_c_re', 's5_c_im', 's5_d', 's5_w_glu', 'w_branch',
                'w_gate', 'b_gate', 'w_out', 'w_ff1', 'w_ff2', 'final_g')
BIG = ('w_in', 's5_w_glu', 'w_branch', 'w_gate', 'w_out', 'w_ff1', 'w_ff2')
SMALL = ('norm1_g', 'norm2_g', 'gmlp_ln_g', 'gmlp_ln_b', 'gmlp_ws', 'gmlp_bs', 'conv_w', 'conv_b', 'conv_ln_g',
         'conv_ln_b', 'attn_sink', 's5_a_re', 's5_a_im', 's5_log_step', 's5_b_re', 's5_b_im', 's5_c_re',
         's5_c_im', 's5_d', 'b_gate')
SMALL_SHARDED = ('conv_w', 'b_gate')


def _from_shards(name, sh):
    j = sh.shape[0]
    if name in ('w_in', 's5_w_glu', 'w_ff1'):
        return sh.transpose(1, 0, 2).reshape(sh.shape[1], j * sh.shape[2])
    if name == 'w_branch':
        return sh.transpose(1, 2, 0, 3).reshape(sh.shape[1], sh.shape[2], j * sh.shape[3])
    if name == 'w_gate':
        return sh.transpose(0, 2, 1, 3).reshape(j * sh.shape[2], sh.shape[1] * sh.shape[3])
    return sh.reshape(j * sh.shape[1], sh.shape[2])


def _to_shards(name, dw, j=N_CHIPS):
    if name in ('w_in', 's5_w_glu', 'w_ff1'):
        return dw.reshape(dw.shape[0], j, dw.shape[1] // j).transpose(1, 0, 2)
    if name == 'w_branch':
        return dw.reshape(dw.shape[0], dw.shape[1], j, dw.shape[2] // j).transpose(2, 0, 1, 3)
    if name == 'w_gate':
        d = dw.shape[0]
        return dw.reshape(j, d // j, dw.shape[1] // d, d).transpose(0, 2, 1, 3)
    return dw.reshape(j, dw.shape[0] // j, dw.shape[1])


def _pack_rows(flat_parts, lead):
    flat = jnp.concatenate(flat_parts, axis=-1)
    assert flat.shape[-1] % LANES == 0, flat.shape
    return flat.reshape(lead + (flat.shape[-1] // LANES, LANES))


def _block_diag(blocks):
    g, a, b = blocks.shape
    eye = jnp.eye(g, dtype=blocks.dtype)
    return (blocks[:, :, None, :] * eye[:, None, :, None]).reshape(g * a, g * b)


def _diag_blocks(mat, g):
    a, b = mat.shape[0] // g, mat.shape[1] // g
    eye = jnp.eye(g, dtype=mat.dtype)
    return (mat.reshape(g, a, g, b) * eye[:, None, :, None]).sum(axis=2)


def _rope_tables(s_len):
    rows = s_len // GRID_W
    row = jnp.repeat(jnp.arange(rows), GRID_W).astype(F32)
    col = jnp.tile(jnp.arange(GRID_W), rows).astype(F32)
    d = HEAD_DIM // 2
    inv = ROPE_BASE ** (-jnp.arange(0, d, 2, dtype=F32) / d)
    ar, ac = row[:, None] * inv[None, :], col[:, None] * inv[None, :]
    cos = jnp.concatenate([jnp.cos(ar), jnp.cos(ar), jnp.cos(ac), jnp.cos(ac)], axis=1)
    sin_signed = jnp.concatenate([-jnp.sin(ar), jnp.sin(ar), -jnp.sin(ac), jnp.sin(ac)], axis=1)
    idx = jnp.arange(HEAD_DIM)
    partner = jnp.where(idx % d < d // 2, idx + d // 2, idx - d // 2)
    perm = (idx[:, None] == partner[None, :]).astype(F32)
    return cos, sin_signed, perm


def _loss_and_grad(xa, tgt, g, *, s_len, tm=ROW_TILE):
    d = xa.shape[1]

    def body(x_ref, t_ref, g_ref, l_ref, dx_ref, dg_ref):
        i = pl.program_id(0)
        (rows,), vjp = jax.vjp(_loss_rows, x_ref[...], t_ref[...], g_ref[...])
        dx, _, dg = vjp((jnp.ones_like(rows),))
        dx_ref[...] = dx
        part = jnp.zeros(l_ref.shape, F32) + jnp.sum(rows)

        @pl.when(i == 0)
        def _():
            l_ref[...] = part
            dg_ref[...] = dg

        @pl.when(i > 0)
        def _():
            l_ref[...] += part
            dg_ref[...] += dg

    return pl.pallas_call(
        body, name="loss_head", grid=(s_len // tm,),
        in_specs=[pl.BlockSpec((tm, d), lambda i: (i, 0)), pl.BlockSpec((tm, d), lambda i: (i, 0)),
                  pl.BlockSpec((1, d), lambda i: (0, 0))],
        out_specs=[pl.BlockSpec((SUBLANES, LANES), lambda i: (0, 0)), pl.BlockSpec((tm, d), lambda i: (i, 0)),
                   pl.BlockSpec((1, d), lambda i: (0, 0))],
        out_shape=[jax.ShapeDtypeStruct((SUBLANES, LANES), F32), jax.ShapeDtypeStruct((s_len, d), F32),
                   jax.ShapeDtypeStruct((1, d), F32)],
        compiler_params=_cparams(("arbitrary",)),
    )(xa, tgt, g)


def _adamw_call(w, g, m, v, *, name):
    shape = w.shape
    cols = shape[-1] if w.ndim > 1 else LANES
    two_d = lambda a: a.reshape(-1, cols)
    rows = two_d(w).shape[0]
    tm = _pick(rows, tuple(t for t in (1024, 512, 256, 128, 64, 32, 16, 8) if t * cols * 4 <= (1 << 20)))
    outs = _rowwise(_adamw, [two_d(w), two_d(g), two_d(m), two_d(v)], [], [(cols, F32)] * 3, name=name, tm=tm)
    return tuple(o.reshape(shape) for o in outs)


def kernel(x, c, ctx, c_ctx, w_mod, b_mod, norm1_g, norm2_g, w_in, gmlp_ln_g, gmlp_ln_b, gmlp_ws, gmlp_bs, conv_w, conv_b, conv_ln_g, conv_ln_b, attn_sink, s5_a_re, s5_a_im, s5_log_step, s5_b_re, s5_b_im, s5_c_re, s5_c_im, s5_d, s5_w_glu, w_branch, w_gate, b_gate, w_out, w_ff1, w_ff2, final_g, loss_target, m_c_ctx, m_w_mod, m_b_mod, m_norm1_g, m_norm2_g, m_w_in, m_gmlp_ln_g, m_gmlp_ln_b, m_gmlp_ws, m_gmlp_bs, m_conv_w, m_conv_b, m_conv_ln_g, m_conv_ln_b, m_attn_sink, m_s5_a_re, m_s5_a_im, m_s5_log_step, m_s5_b_re, m_s5_b_im, m_s5_c_re, m_s5_c_im, m_s5_d, m_s5_w_glu, m_w_branch, m_w_gate, m_b_gate, m_w_out, m_w_ff1, m_w_ff2, m_final_g, v_c_ctx, v_w_mod, v_b_mod, v_norm1_g, v_norm2_g, v_w_in, v_gmlp_ln_g, v_gmlp_ln_b, v_gmlp_ws, v_gmlp_bs, v_conv_w, v_conv_b, v_conv_ln_g, v_conv_ln_b, v_attn_sink, v_s5_a_re, v_s5_a_im, v_s5_log_step, v_s5_b_re, v_s5_b_im, v_s5_c_re, v_s5_c_im, v_s5_d, v_s5_w_glu, v_w_branch, v_w_gate, v_b_gate, v_w_out, v_w_ff1, v_w_ff2, v_final_g):
    given = dict(locals())
    p = {n: given[n] for n in WEIGHT_NAMES}
    mom = {n: given["m_" + n] for n in WEIGHT_NAMES}
    var = {n: given["v_" + n] for n in WEIGHT_NAMES}

    s_len, d = x.shape[1], x.shape[2]
    n_ctx = ctx.shape[1]
    r = s_len + n_ctx
    n_layers = w_mod.shape[0]
    db = d // N_BRANCH
    n_q = db // HEAD_DIM
    qpk = n_q // N_KV_HEADS
    g5 = db // S5_GW
    gp = g5 * S5_STATE
    ch = gp // LANES
    gw = db // GMLP_GROUPS
    n_lat = s_len // ROW_TILE
    assert s_len % ROW_TILE == 0 and n_ctx % ROW_TILE == 0 and s_len % n_ctx == 0 and gp % LANES == 0
    ax, ay, ac = lax.axis_index("x"), lax.axis_index("y"), lax.axis_index("c")
    j_me = 2 * ax + ay
    d_me = 4 * ax + 2 * ay + ac
    rw = functools.partial(_rowwise, s_rows=s_len)
    rwv = functools.partial(_rowwise_vjp, s_rows=s_len)
    add_epi = lambda acc, t: (acc + t,)

    c_all = _allgather8(jnp.broadcast_to(c, (SUBLANES, d)), name="gather_c")[::SUBLANES]
    c16 = jnp.concatenate([c_all, jnp.broadcast_to(c_ctx[None], (N_DEV, d))], axis=0)
    silu_fn = lambda a: (jax.nn.silu(a),)
    cond16 = _whole(silu_fn, [c16], [(c16.shape, F32)], name="silu_c")[0]
    ncol = w_mod.shape[2]
    b_mod_sh = lax.dynamic_slice_in_dim(b_mod, j_me * ncol, ncol, axis=1)
    mod_part = jnp.concatenate([
        _matmul(cond16, w_mod[l], tm=2 * N_DEV, rowvecs=[b_mod_sh[l][None]], epi=add_epi, name="mod_proj")
        for l in range(n_layers)], axis=0)
    mod_all = _allgather8(mod_part, name="gather_mod").reshape(N_CHIPS, 2, n_layers, 2 * N_DEV, ncol)[:, 0]
    mod_all = mod_all.transpose(1, 2, 0, 3).reshape(n_layers, 2 * N_DEV, N_CHIPS * ncol)

    def mods_of(l):
        two = jnp.stack([lax.dynamic_index_in_dim(mod_all[l], d_me, 0, keepdims=False), mod_all[l, N_DEV]])
        return [two[:, None, k * d:(k + 1) * d] for k in range(N_MOD)]

    d_sh, f_all = d // N_CHIPS, w_ff1.shape[2] * N_CHIPS
    f_sh = f_all // N_CHIPS

    def gather_layer(l):
        stacks = []
        for n in BIG:
            sh = p[n][l].astype(BF16)
            at = [j_me.astype(jnp.int32)] + [jnp.zeros((), jnp.int32)] * sh.ndim
            stacks.append(lax.dynamic_update_slice(jnp.zeros((N_CHIPS,) + sh.shape, BF16), sh[None], at))
        st = dict(zip(BIG, _gather_shards(stacks)))
        w = {n: _from_shards(n, st[n]) for n in ("w_in", "s5_w_glu", "w_branch")}
        w["w_out"] = st["w_out"].reshape(d, d)
        w["w_ff2"] = st["w_ff2"].reshape(f_all, d)
        w["w_gate"], w["w_ff1"] = st["w_gate"], st["w_ff1"]
        return w

    g_tn = min(1024, d)
    gate_fwd = dict(tk=d, tn=g_tn, b_blocks=(N_BRANCH * d, (N_CHIPS, None, d_sh, g_tn),
                                             lambda i, j, kk: (0, j // (d // g_tn), 0, j % (d // g_tn))))
    gate_bwd = dict(tb=True, tn=2 * d_sh, tk=d, b_blocks=(d, (2, None, d_sh, d), lambda i, j, kk: (j, kk, 0, 0)))
    gw_tn = min(2048, d)
    gate_wgt = dict(ta=True, tm=d_sh, tn=gw_tn, out_blocks=(
        (N_CHIPS, N_BRANCH, d_sh, d), (None, None, d_sh, gw_tn),
        lambda i, j, kk: (i, j // (d // gw_tn), 0, j % (d // gw_tn))))
    f_tn = min(1024, f_sh)
    ff1_fwd = dict(tn=f_tn, tk=d, b_blocks=(f_all, (None, d, f_tn), lambda i, j, kk: (j // (f_sh // f_tn), 0, j % (f_sh // f_tn))))
    f_tk = min(2048, f_sh)
    fb_tn = min(1024, d)
    ff1_bwd = dict(tb=True, tn=fb_tn, tk=f_tk, b_blocks=(
        d, (None, fb_tn, f_tk), lambda i, j, kk: (kk // (f_sh // f_tk), j, kk % (f_sh // f_tk))))
    fw_tn, fw_tm = min(2048, f_sh), min(1024, d)
    ff1_wgt = dict(ta=True, tm=fw_tm, tn=fw_tn, out_blocks=(
        (N_CHIPS, d, f_sh), (None, fw_tm, fw_tn), lambda i, j, kk: (j // (f_sh // fw_tn), i, j % (f_sh // fw_tn))))

    def pad_rows(flat):
        n = -(-flat.shape[0] // (SUBLANES * LANES)) * SUBLANES * LANES
        return jnp.pad(flat, (0, n - flat.shape[0])).reshape(n // LANES, LANES)

    sh_flat = pad_rows(jnp.concatenate([conv_w.reshape(-1), b_gate.reshape(-1)]))
    sh_all = _allgather8(sh_flat, name="gather_small_w").reshape(N_CHIPS, 2, -1)[:, 0]
    conv_w_full = sh_all[:, :conv_w.size].reshape((N_CHIPS,) + conv_w.shape).transpose(1, 2, 0, 3)
    conv_w_full = conv_w_full.reshape(n_layers, CONV_W, db)
    b_gate_full = sh_all[:, conv_w.size:conv_w.size + b_gate.size].reshape((N_CHIPS,) + b_gate.shape)
    b_gate_full = b_gate_full.transpose(1, 2, 0, 3).reshape(n_layers, 1, N_BRANCH * d)

    cos, sin_signed, perm = _rope_tables(s_len)
    xa = jnp.concatenate([x[0], ctx[0]], axis=0)
    col = lambda a: a[None] if a.ndim == 1 else a
    to_heads = lambda a, nh: a.reshape(r, N_KV_HEADS, nh // N_KV_HEADS, HEAD_DIM).transpose(1, 2, 0, 3)
    from_heads = lambda a: a.transpose(2, 0, 1, 3).reshape(r, -1)
    sink_rows_of = lambda l: jnp.repeat(attn_sink[l].reshape(N_KV_HEADS, qpk, 1), BLOCK, axis=1).reshape(
        N_KV_HEADS, qpk * BLOCK, 1)
    slab = lambda a: a.reshape(r, 2 * ch, LANES)
    flat2 = lambda a: a.reshape(r, 2 * gp)

    saved = []
    for l in range(n_layers):
        w = gather_layer(l)
        shift1, scale1, gate1, shift2, scale2, gate2 = mods_of(l)
        n1g, n2g = col(norm1_g[l]), col(norm2_g[l])
        h = rw(_rms_mod, [xa], [("full", n1g), ("mod", scale1), ("mod", shift1)], [(d, BF16)], name="norm1")[0]
        z = _matmul(h, w["w_in"], name="in_proj")
        o1, o2, o3, o4, o5 = 2 * db, 4 * db, 4 * db + n_q * HEAD_DIM, 4 * db + (n_q + N_KV_HEADS) * HEAD_DIM, \
            4 * db + (n_q + 2 * N_KV_HEADS) * HEAD_DIM
        za, zb, zq, zk, zv, zd = z[:, :o1], z[:, o1:o2], z[:, o2:o3], z[:, o3:o4], z[:, o4:o5], z[:, o5:]
        b_full = jnp.repeat(gmlp_bs[l].T, gw, axis=1)
        gmlp_params = [("full", col(gmlp_ln_g[l])), ("full", col(gmlp_ln_b[l])), ("full", gmlp_ws[l]), ("full", b_full)]
        br_a = rw(_gmlp, [za], gmlp_params, [(db, BF16)], name="gmlp")[0]
        conv_params = (conv_w_full[l], col(conv_b[l]), col(conv_ln_g[l]), col(conv_ln_b[l]))
        br_b, y_conv, yc_conv = _conv_fwd(zb, *conv_params, n_lat=n_lat)
        q4, k3, v3 = to_heads(zq, n_q), to_heads(zk, N_KV_HEADS)[:, 0], to_heads(zv, N_KV_HEADS)[:, 0]
        sink_rows = sink_rows_of(l)
        br_c = from_heads(_attn_fwd(q4, k3, v3, sink_rows, cos, sin_signed, perm, s_len=s_len))
        disc_in = [s5_a_re[l].reshape(2, gp, 1), s5_a_im[l].reshape(2, gp, 1),
                   jnp.broadcast_to(s5_log_step[l][:, :, None], (2, g5, S5_STATE)).reshape(2, gp, 1),
                   s5_b_re[l].reshape(gp, S5_GW), s5_b_im[l].reshape(gp, S5_GW)]
        l_re, l_im, bb_re, bb_im = _whole(
            _s5_discretise, disc_in, [((2, gp, 1), F32)] * 2 + [((2, gp, S5_GW), F32)] * 2, name="s5_disc")
        lbar = jnp.stack([l_re[0], l_im[0], l_re[1], l_im[1]]).reshape(4, ch, LANES)
        bd_of = lambda a: _block_diag(a.reshape(g5, S5_STATE, S5_GW).transpose(0, 2, 1))
        cd_of = lambda a: _block_diag(a.transpose(0, 2, 1))
        bd = [jnp.concatenate([bd_of(bb_re[k]), bd_of(bb_im[k])], axis=1).astype(BF16) for k in range(2)]
        cd = [jnp.concatenate([cd_of(s5_c_re[l, k]), -cd_of(s5_c_im[l, k])], axis=0).astype(BF16) for k in range(2)]
        bu_f = _matmul(zd, bd[0], name="s5_in_f")
        bu_b = _matmul(zd, bd[1], name="s5_in_b")
        st_f, st_b = _s5_scan(slab(bu_f), slab(bu_b), lbar, n_lat=n_lat)
        ys = _matmul(flat2(st_f), cd[0], name="s5_out_f")
        ys = _matmul(flat2(st_b), cd[1], tiles=[ys], epi=add_epi, name="s5_out_b")
        d_skip = col(s5_d[l])
        yg = rw(_s5_act, [ys, zd], [("full", d_skip)], [(db, BF16)], name="s5_act")[0]
        glu_pre = _matmul(yg, w["s5_w_glu"], name="s5_glu_proj")
        br_d = rw(_glu_gate, [glu_pre], [], [(db, BF16)], name="s5_glu")[0]
        branches = (br_a, br_b, br_c, br_d)
        gates = _matmul(h, w["w_gate"], name="gate_proj", **gate_fwd)
        projs = [_matmul(branches[k], w["w_branch"][k], name="branch_proj") for k in range(N_BRANCH)]
        merged = rw(_merge, [gates] + projs, [("full", b_gate_full[l])], [(d, BF16)], name="merge", tm=ROW_TILE // 2)[0]
        o = _matmul(merged, w["w_out"], name="out_proj")
        x1 = rw(_resid, [xa, o], [("mod", gate1)], [(d, F32)], name="resid1")[0]
        h2 = rw(_rms_mod, [x1], [("full", n2g), ("mod", scale2), ("mod", shift2)], [(d, BF16)], name="norm2")[0]
        f1, act = _matmul(h2, w["w_ff1"], out_dtypes=(F32, BF16), name="ff1",
                          epi=lambda acc: (acc, jnp.square(jnp.maximum(acc, 0.0))), **ff1_fwd)
        o_ff = _matmul(act, w["w_ff2"], name="ff2")
        x2 = rw(_resid, [x1, o_ff], [("mod", gate2)], [(d, F32)], name="resid2")[0]
        saved.append(dict(
            w=w, mods=(shift1, scale1, gate1, shift2, scale2, gate2), n1g=n1g, n2g=n2g, xa=xa, h=h, z=z,
            gmlp_params=gmlp_params, conv_params=conv_params, y_conv=y_conv, yc_conv=yc_conv, q4=q4, k3=k3, v3=v3,
            sink_rows=sink_rows, disc_in=disc_in, lbar=lbar, bd=bd, cd=cd, st_f=st_f, st_b=st_b, ys=ys,
            d_skip=d_skip, yg=yg, glu_pre=glu_pre, branches=branches, gates=gates, projs=projs, merged=merged,
            o=o, x1=x1, h2=h2, f1=f1, act=act, o_ff=o_ff))
        xa = x2

    loss_sum, dx_lat, d_final_g = _loss_and_grad(xa, loss_target[0], col(final_g), s_len=s_len)
    loss = lax.psum(loss_sum[0, 0], ("x", "y", "c"))
    dxa = jnp.concatenate([dx_lat, jnp.zeros((n_ctx, d), F32)], axis=0)

    big_grads = [None] * n_layers
    small_grads = [None] * n_layers
    d_mods = [None] * n_layers
    o1, o2, o3, o4, o5 = 2 * db, 4 * db, 4 * db + n_q * HEAD_DIM, 4 * db + (n_q + N_KV_HEADS) * HEAD_DIM, \
        4 * db + (n_q + 2 * N_KV_HEADS) * HEAD_DIM
    for l in reversed(range(n_layers)):
        sv = saved[l]
        w = sv["w"]
        shift1, scale1, gate1, shift2, scale2, gate2 = sv["mods"]
        z = sv["z"]
        za, zb, zd = z[:, :o1], z[:, o1:o2], z[:, o5:]
        bg, sg = {}, {}
        (d_x1, d_off), (d_gate2,) = rwv(_resid, [sv["x1"], sv["o_ff"]], [("mod", gate2)], [dxa],
                                        row_grads=[F32, BF16], name="resid2_b")
        d_f1 = _matmul(d_off, w["w_ff2"], tb=True, tiles=[sv["f1"]], out_dtypes=(BF16,), name="ff2_b",
                       epi=lambda acc, f: (acc * (2.0 * jnp.maximum(f, 0.0)),))
        bg["w_ff2"] = _matmul(sv["act"], d_off, ta=True, name="ff2_w").reshape(N_CHIPS, f_sh, d)
        d_h2 = _matmul(d_f1, w["w_ff1"], name="ff1_b", **ff1_bwd)
        bg["w_ff1"] = _matmul(sv["h2"], d_f1, name="ff1_w", **ff1_wgt)
        (d_x1,), (sg["norm2_g"], d_scale2, d_shift2) = rwv(
            _rms_mod, [sv["x1"]], [("full", sv["n2g"]), ("mod", scale2), ("mod", shift2)], [d_h2],
            row_grads=[F32], adds={0: d_x1}, name="norm2_b")
        (d_xa, d_o), (d_gate1,) = rwv(_resid, [sv["xa"], sv["o"]], [("mod", gate1)], [d_x1],
                                      row_grads=[F32, BF16], name="resid1_b")
        d_merged = _matmul(d_o, w["w_out"], tb=True, name="out_b")
        bg["w_out"] = _matmul(sv["merged"], d_o, ta=True, name="out_w").reshape(N_CHIPS, d_sh, d)
        d_parts, (d_bg,) = rwv(_merge, [sv["gates"]] + sv["projs"], [("full", b_gate_full[l])], [d_merged],
                               row_grads=[BF16] * (1 + N_BRANCH), name="merge_b", tm=ROW_TILE // 4)
        sg["b_gate"] = d_bg.reshape(N_BRANCH, d)
        d_gates, d_projs = d_parts[0], d_parts[1:]
        bg["w_gate"] = _matmul(sv["h"], d_gates, name="gate_w", **gate_wgt)
        d_h = _matmul(d_gates, w["w_gate"], name="gate_b", **gate_bwd)
        d_br = [_matmul(d_projs[k], w["w_branch"][k], tb=True, name="branch_b") for k in range(N_BRANCH)]
        bg["w_branch"] = _to_shards("w_branch", jnp.stack([
            _matmul(sv["branches"][k], d_projs[k], ta=True, name="branch_w") for k in range(N_BRANCH)]))
        (d_glu_pre,), _ = rwv(_glu_gate, [sv["glu_pre"]], [], [d_br[3]], row_grads=[BF16], name="s5_glu_b")
        d_yg = _matmul(d_glu_pre, w["s5_w_glu"], tb=True, name="s5_glu_proj_b")
        bg["s5_w_glu"] = _to_shards("s5_w_glu", _matmul(sv["yg"], d_glu_pre, ta=True, name="s5_glu_proj_w"))
        (d_ys, d_zd), (d_dskip,) = rwv(_s5_act, [sv["ys"], zd], [("full", sv["d_skip"])], [d_yg],
                                       row_grads=[BF16, F32], name="s5_act_b")
        sg["s5_d"] = d_dskip.reshape(db)
        cd, bd = sv["cd"], sv["bd"]
        st2 = [flat2(sv["st_f"]), flat2(sv["st_b"])]
        d_st = [_matmul(d_ys, cd[k], tb=True, name="s5_out_b%d" % k) for k in range(2)]
        d_cd = [_matmul(st2[k], d_ys, ta=True, name="s5_out_w%d" % k) for k in range(2)]
        d_bu_f, d_bu_b, d_lbar = _s5_scan_bwd(slab(d_st[0]), slab(d_st[1]), sv["st_f"], sv["st_b"], sv["lbar"],
                                              n_lat=n_lat)
        d_bu = [flat2(d_bu_f), flat2(d_bu_b)]
        d_zd = _matmul(d_bu[0], bd[0], tb=True, tiles=[d_zd], epi=add_epi, name="s5_in_b0")
        d_zd = _matmul(d_bu[1], bd[1], tb=True, tiles=[d_zd], epi=add_epi, name="s5_in_b1")
        d_bd = [_matmul(zd, d_bu[k], ta=True, name="s5_in_w%d" % k) for k in range(2)]
        blk_c = lambda a: _diag_blocks(a, g5).transpose(0, 2, 1)
        sg["s5_c_re"] = jnp.stack([blk_c(d_cd[k][:gp]) for k in range(2)])
        sg["s5_c_im"] = jnp.stack([-blk_c(d_cd[k][gp:]) for k in range(2)])
        blk_b = lambda a: _diag_blocks(a, g5).transpose(0, 2, 1).reshape(gp, S5_GW)
        d_bb_re = jnp.stack([blk_b(d_bd[k][:, :gp]) for k in range(2)])
        d_bb_im = jnp.stack([blk_b(d_bd[k][:, gp:]) for k in range(2)])
        d_lb = d_lbar.reshape(4, gp, 1)
        disc_ct = [jnp.stack([d_lb[0], d_lb[2]]), jnp.stack([d_lb[1], d_lb[3]]), d_bb_re, d_bb_im]
        d_are, d_aim, d_ls, d_bre, d_bim = _whole_vjp(_s5_discretise, sv["disc_in"], disc_ct, name="s5_disc_b")
        sg["s5_a_re"], sg["s5_a_im"] = d_are.reshape(2, g5, S5_STATE), d_aim.reshape(2, g5, S5_STATE)
        sg["s5_log_step"] = d_ls.reshape(2, g5, S5_STATE).sum(axis=-1)
        sg["s5_b_re"], sg["s5_b_im"] = d_bre.reshape(g5, S5_STATE, S5_GW), d_bim.reshape(g5, S5_STATE, S5_GW)
        d_o4 = to_heads(d_br[2], n_q)
        d_q4, d_k3, d_v3, d_sink_rows = _attn_bwd(sv["q4"], sv["k3"], sv["v3"], sv["sink_rows"], cos, sin_signed,
                                                  perm, d_o4, s_len=s_len)
        sg["attn_sink"] = d_sink_rows.reshape(n_q, BLOCK).sum(axis=-1)
        d_zq = from_heads(d_q4)
        d_zk, d_zv = from_heads(d_k3[:, None]), from_heads(d_v3[:, None])
        cw, cb, clg, clb = sv["conv_params"]
        (d_yc,), (sg["conv_ln_g"], sg["conv_ln_b"]) = rwv(
            _ln_silu, [sv["yc_conv"]], [("full", clg), ("full", clb)], [d_br[1]], row_grads=[F32], name="conv_ln_b")
        d_zb, sg["conv_w"], sg["conv_b"] = _conv_bwd(d_yc, sv["y_conv"], zb, cw, n_lat=n_lat)
        (d_za,), (sg["gmlp_ln_g"], sg["gmlp_ln_b"], sg["gmlp_ws"], d_bfull) = rwv(
            _gmlp, [za], sv["gmlp_params"], [d_br[0]], row_grads=[F32], name="gmlp_b")
        sg["gmlp_bs"] = d_bfull.reshape(CHUNK, GMLP_GROUPS, gw).sum(axis=-1).T
        d_z = jnp.concatenate([d_za, d_zb, d_zq, d_zk, d_zv, d_zd], axis=1).astype(BF16)
        d_h = _matmul(d_z, w["w_in"], tb=True, tiles=[d_h], epi=add_epi, name="in_b")
        bg["w_in"] = _to_shards("w_in", _matmul(sv["h"], d_z, ta=True, name="in_w"))
        (dxa,), (sg["norm1_g"], d_scale1, d_shift1) = rwv(
            _rms_mod, [sv["xa"]], [("full", sv["n1g"]), ("mod", scale1), ("mod", shift1)], [d_h],
            row_grads=[F32], adds={0: d_xa}, name="norm1_b")
        d_mods[l] = jnp.concatenate([d_shift1, d_scale1, d_gate1, d_shift2, d_scale2, d_gate2], axis=-1)[:, 0]
        big_grads[l], small_grads[l] = bg, sg
        saved[l] = None

    grad_x = dxa[:s_len][None]

    dm_loc = jnp.stack(d_mods).reshape(n_layers * 2, N_MOD * d)
    dm_pad = -(-dm_loc.shape[0] // SUBLANES) * SUBLANES
    dm_all = _allgather8(jnp.pad(dm_loc, ((0, dm_pad - dm_loc.shape[0]), (0, 0))), name="gather_dmod")
    dm_all = dm_all.reshape(N_DEV, dm_pad, N_MOD * d)[:, :n_layers * 2].reshape(N_DEV, n_layers, 2, N_MOD * d)
    dm16 = dm_all.transpose(1, 2, 0, 3).reshape(n_layers, 2 * N_DEV, N_MOD * d)
    dm16_sh = lax.dynamic_slice_in_dim(dm16, j_me * ncol, ncol, axis=2)
    col_sum = lambda a: (jnp.sum(a, axis=0, keepdims=True),)
    grads = {}
    grads["w_mod"] = jnp.stack([_matmul(cond16, dm16_sh[l], ta=True, tk=2 * N_DEV, name="mod_w")
                                for l in range(n_layers)])
    grads["b_mod"] = jnp.concatenate([
        _whole(col_sum, [dm16[l]], [((1, N_MOD * d), F32)], name="mod_bias_g")[0] for l in range(n_layers)], axis=0)
    d_cond = _matmul(dm16_sh[0], w_mod[0], tb=True, tm=2 * N_DEV, name="mod_b")
    for l in range(1, n_layers):
        d_cond = _matmul(dm16_sh[l], w_mod[l], tb=True, tm=2 * N_DEV, tiles=[d_cond], epi=add_epi, name="mod_b_acc")
    d_c16 = _whole_vjp(silu_fn, [c16], [d_cond], name="silu_c_b")[0]
    d_cctx_part = jnp.where(ac == 0, d_c16[N_DEV:].sum(axis=0), 0.0)

    for n in BIG:
        grads[n] = [None] * n_layers
    for l in range(n_layers):
        for n, g in zip(BIG, _reduce_scatter([big_grads[l][n] for n in BIG])):
            grads[n][l] = g
    for n in BIG:
        grads[n] = jnp.stack(grads[n])

    small_shapes = {n: small_grads[0][n].shape for n in SMALL}
    for n in SMALL:
        grads[n] = [None] * n_layers
    for l in range(n_layers):
        extra = [d_final_g.reshape(-1), d_cctx_part.reshape(-1)] if l == 0 else []
        flat = pad_rows(jnp.concatenate([small_grads[l][n].reshape(-1) for n in SMALL] + extra))
        red = _allgather8(flat, reduce=True, name="reduce_small").reshape(-1)
        off = 0
        for n in SMALL:
            sz = math.prod(small_shapes[n])
            grads[n][l] = red[off:off + sz].reshape(small_shapes[n])
            off += sz
        if l == 0:
            grads["final_g"] = red[off:off + d]
            grads["c_ctx"] = red[off + d:off + 2 * d]
    for n in SMALL:
        g_full = jnp.stack(grads[n])
        if n in SMALL_SHARDED:
            width = p[n].shape[-1]
            g_full = lax.dynamic_slice_in_dim(g_full, j_me * width, width, axis=g_full.ndim - 1)
        grads[n] = g_full.reshape(p[n].shape)

    delta, new_m, new_v = {}, {}, {}
    large = BIG + ("w_mod",)
    for n in large:
        delta[n], new_m[n], new_v[n] = _adamw_call(p[n], grads[n], mom[n], var[n], name="adamw")
    small = [n for n in WEIGHT_NAMES if n not in large]
    pack = lambda src: pad_rows(jnp.concatenate([src[n].reshape(-1) for n in small]))
    outs = _adamw_call(pack(p), pack(grads), pack(mom), pack(var), name="adamw_small")
    off = 0
    for n in small:
        sz = p[n].size
        delta[n], new_m[n], new_v[n] = (o.reshape(-1)[off:off + sz].reshape(p[n].shape) for o in outs)
        off += sz

    return (loss, grad_x, *[grads[n] for n in WEIGHT_NAMES], *[delta[n] for n in WEIGHT_NAMES],
            *[new_m[n] for n in WEIGHT_NAMES], *[new_v[n] for n in WEIGHT_NAMES])
```

```python
import functools
import math

import jax
import jax.numpy as jnp
from jax import lax
from jax.experimental import pallas as pl
from jax.experimental.pallas import tpu as pltpu

F32 = jnp.float32
BF16 = jnp.bfloat16
MESH = pl.DeviceIdType.MESH

V7X_VMEM_BYTES = 64 * 1024 * 1024
VMEM_LIMIT = V7X_VMEM_BYTES - 8 * 1024 * 1024
LANES = 128
SUBLANES = 8

N_BRANCH = 4
CHUNK = 128
GMLP_GROUPS = 4
CONV_W = 31
CONV_HALO = 16
HEAD_DIM = 64
N_KV_HEADS = 2
WINDOW = 128
BLOCK = 128
ROPE_BASE = 10000.0
GRID_W = 64
S5_GW = 16
S5_STATE = 64
N_MOD = 6
EPS = 1e-6
NEG_INF = -1e30
ADAM_LR = 0.001
ADAM_B1 = 0.9
ADAM_B2 = 0.999
ADAM_EPS = 1e-08
ADAM_WD = 0.01
ADAM_STEP = 10

ROW_TILE = 256


def _cparams(sem=None, **kw):
    if sem is not None:
        kw["dimension_semantics"] = sem
    return pltpu.CompilerParams(vmem_limit_bytes=VMEM_LIMIT, **kw)


def _pick(n, cands):
    for c in cands:
        if n % c == 0:
            return c
    return n


def _bdot(a, b, ca, cb):
    return lax.dot_general(a.astype(BF16), b.astype(BF16), (((ca,), (cb,)), ((), ())),
                           preferred_element_type=F32)


@jax.custom_vjp
def _mm_nn(a, b):
    return _bdot(a, b, 1, 0)


def _mm_nn_fwd(a, b):
    return _bdot(a, b, 1, 0), (a, b)


def _mm_nn_bwd(res, g):
    a, b = res
    return _bdot(g, b, 1, 1).astype(a.dtype), _bdot(a, g, 0, 0).astype(b.dtype)


_mm_nn.defvjp(_mm_nn_fwd, _mm_nn_bwd)


@jax.custom_vjp
def _mm_nt(a, b):
    return _bdot(a, b, 1, 1)


def _mm_nt_fwd(a, b):
    return _bdot(a, b, 1, 1), (a, b)


def _mm_nt_bwd(res, g):
    a, b = res
    return _bdot(g, b, 1, 0).astype(a.dtype), _bdot(g, a, 0, 0).astype(b.dtype)


_mm_nt.defvjp(_mm_nt_fwd, _mm_nt_bwd)


def _matmul(a, b, *, ta=False, tb=False, tm=None, tn=None, tk=None, name,
            out_dtypes=(F32,), epi=None, tiles=(), rowvecs=(), b_blocks=None, out_blocks=None):
    m, k = (a.shape[1], a.shape[0]) if ta else a.shape
    if b_blocks is None:
        k2, n = (b.shape[1], b.shape[0]) if tb else b.shape
        assert k == k2, (a.shape, b.shape, ta, tb)
    else:
        n = b_blocks[0]
    tm = tm or _pick(m, (1024, 512, 256, 128) if ta else (768, 512, 384, 256, 128))
    tn = tn or _pick(n, (2048, 1664, 1024, 512, 256, 128) if ta else (1024, 1664, 512, 256, 128))
    tk = tk or _pick(k, (1408, 768, 512, 384, 256, 128) if ta else (2048, 1664, 1024, 768, 512, 384, 256, 128))
    assert m % tm == 0 and n % tn == 0 and k % tk == 0, (m, n, k, tm, tn, tk)
    nk = k // tk
    n_t, n_v, n_o = len(tiles), len(rowvecs), len(out_dtypes)
    ca, cb = (0 if ta else 1), (1 if tb else 0)

    def body(*refs):
        a_ref, b_ref = refs[:2]
        t_refs = refs[2:2 + n_t]
        v_refs = refs[2 + n_t:2 + n_t + n_v]
        o_refs = refs[2 + n_t + n_v:2 + n_t + n_v + n_o]

        def finish(acc):
            outs = (acc,) if epi is None else epi(acc, *[t[...] for t in t_refs], *[v[...] for v in v_refs])
            for o_ref, o in zip(o_refs, outs):
                o_ref[...] = o.astype(o_ref.dtype)

        b_tile = b_ref[...]
        if b_tile.ndim == 3:
            b_tile = b_tile.reshape(b_tile.shape[0] * b_tile.shape[1], b_tile.shape[2])
        part = _bdot(a_ref[...], b_tile, ca, cb)
        if nk == 1:
            finish(part)
        else:
            acc_ref = refs[-1]
            kk = pl.program_id(2)

            @pl.when(kk == 0)
            def _():
                acc_ref[...] = part

            @pl.when(kk > 0)
            def _():
                acc_ref[...] += part

            @pl.when(kk == nk - 1)
            def _():
                finish(acc_ref[...])

    a_spec = pl.BlockSpec((tk, tm), lambda i, j, kk: (kk, i)) if ta else pl.BlockSpec((tm, tk), lambda i, j, kk: (i, kk))
    b_spec = pl.BlockSpec((tn, tk), lambda i, j, kk: (j, kk)) if tb else pl.BlockSpec((tk, tn), lambda i, j, kk: (kk, j))
    if b_blocks is not None:
        b_spec = pl.BlockSpec(b_blocks[1], b_blocks[2])
    in_specs = [a_spec, b_spec]
    in_specs += [pl.BlockSpec((tm, tn), lambda i, j, kk: (i, j)) for _ in tiles]
    in_specs += [pl.BlockSpec((1, tn), lambda i, j, kk: (0, j)) for _ in rowvecs]
    out_specs = [pl.BlockSpec((tm, tn), lambda i, j, kk: (i, j)) for _ in out_dtypes]
    out_shape = [jax.ShapeDtypeStruct((m, n), d) for d in out_dtypes]
    if out_blocks is not None:
        out_specs = [pl.BlockSpec(out_blocks[1], out_blocks[2])]
        out_shape = [jax.ShapeDtypeStruct(out_blocks[0], out_dtypes[0])]
    out = pl.pallas_call(
        body, name=name,
        grid=(m // tm, n // tn, nk),
        in_specs=in_specs,
        out_specs=out_specs,
        out_shape=out_shape,
        scratch_shapes=[pltpu.VMEM((tm, tn), F32)] if nk > 1 else [],
        compiler_params=_cparams(("parallel", "parallel", "arbitrary")),
    )(a, b, *tiles, *rowvecs)
    return out[0] if n_o == 1 else tuple(out)


def _param_spec(kind, p, n_lat):
    if kind == "mod":
        return pl.BlockSpec((None,) + p.shape[1:], lambda i: (i // n_lat,) + (0,) * (p.ndim - 1))
    return pl.BlockSpec(p.shape, lambda i: (0,) * p.ndim)


def _rowwise(fn, rows, params, outs, *, name, tm=ROW_TILE, s_rows=None):
    r = rows[0].shape[0]
    assert r % tm == 0, (r, tm)
    n_lat = r // tm + 1 if s_rows is None else s_rows // tm
    n_r, n_p = len(rows), len(params)

    def body(*refs):
        vals = [x[...] for x in refs[:n_r + n_p]]
        res = fn(*vals)
        for o_ref, o in zip(refs[n_r + n_p:], res):
            o_ref[...] = o.astype(o_ref.dtype)

    out = pl.pallas_call(
        body, name=name, grid=(r // tm,),
        in_specs=[pl.BlockSpec((tm, x.shape[1]), lambda i: (i, 0)) for x in rows]
        + [_param_spec(kind, p, n_lat) for kind, p in params],
        out_specs=[pl.BlockSpec((tm, w), lambda i: (i, 0)) for w, _ in outs],
        out_shape=[jax.ShapeDtypeStruct((r, w), d) for w, d in outs],
        compiler_params=_cparams(("parallel",)),
    )(*rows, *[p for _, p in params])
    return tuple(out)


def _rowwise_vjp(fn, rows, params, cts, *, name, row_grads, tm=ROW_TILE, s_rows=None, adds=None):
    r = rows[0].shape[0]
    assert r % tm == 0, (r, tm)
    n_lat = r // tm + 1 if s_rows is None else s_rows // tm
    adds = adds or {}
    n_r, n_p, n_c, n_a = len(rows), len(params), len(cts), len(adds)
    want = [i for i, d in enumerate(row_grads) if d is not None]
    add_at = {idx: k for k, idx in enumerate(sorted(adds))}

    def body(*refs):
        i = pl.program_id(0)
        prim = [x[...].astype(F32) for x in refs[:n_r + n_p]]
        ct = [x[...].astype(F32) for x in refs[n_r + n_p:n_r + n_p + n_c]]
        a_refs = refs[n_r + n_p + n_c:n_r + n_p + n_c + n_a]
        o_refs = refs[n_r + n_p + n_c + n_a:]
        _, vjp = jax.vjp(fn, *prim)
        grads = vjp(tuple(ct))
        for o_ref, idx in zip(o_refs[:len(want)], want):
            g = grads[idx] if idx not in add_at else grads[idx] + a_refs[add_at[idx]][...]
            o_ref[...] = g.astype(o_ref.dtype)
        for o_ref, (kind, _), g in zip(o_refs[len(want):], params, grads[n_r:]):
            first = (i == 0) | (i == n_lat) if kind == "mod" else (i == 0)

            @pl.when(first)
            def _(o_ref=o_ref, g=g):
                o_ref[...] = g

            @pl.when(jnp.logical_not(first))
            def _(o_ref=o_ref, g=g):
                o_ref[...] += g

    out = pl.pallas_call(
        body, name=name, grid=(r // tm,),
        in_specs=[pl.BlockSpec((tm, x.shape[1]), lambda i: (i, 0)) for x in rows]
        + [_param_spec(kind, p, n_lat) for kind, p in params]
        + [pl.BlockSpec((tm, c.shape[1]), lambda i: (i, 0)) for c in cts]
        + [pl.BlockSpec((tm, adds[idx].shape[1]), lambda i: (i, 0)) for idx in sorted(adds)],
        out_specs=[pl.BlockSpec((tm, rows[idx].shape[1]), lambda i: (i, 0)) for idx in want]
        + [_param_spec(kind, p, n_lat) for kind, p in params],
        out_shape=[jax.ShapeDtypeStruct(rows[idx].shape, row_grads[idx]) for idx in want]
        + [jax.ShapeDtypeStruct(p.shape, F32) for _, p in params],
        compiler_params=_cparams(("arbitrary",)),
    )(*rows, *[p for _, p in params], *cts, *[adds[idx] for idx in sorted(adds)])
    return tuple(out[:len(want)]), tuple(out[len(want):])


def _rms_mod(x, g, scale, shift):
    y = x * lax.rsqrt(jnp.mean(x * x, axis=-1, keepdims=True) + EPS)
    return ((y * g) * (1.0 + scale) + shift,)


def _resid(x, o, gate):
    return (x + gate * o,)


def _layer_norm(x, g, b):
    xc = x - jnp.mean(x, axis=-1, keepdims=True)
    var = jnp.mean(xc * xc, axis=-1, keepdims=True)
    return xc * lax.rsqrt(var + EPS) * g + b


def _gmlp(za, ln_g, ln_b, ws, b_full):
    db = za.shape[1] // 2
    gw = db // GMLP_GROUPS
    za = jax.nn.gelu(za)
    u, v = za[:, :db], za[:, db:]
    v = _layer_norm(v, ln_g, ln_b)
    chunks = []
    for n in range(za.shape[0] // CHUNK):
        vn = v[n * CHUNK:(n + 1) * CHUNK]
        cols = [_mm_nn(ws[g], vn[:, g * gw:(g + 1) * gw]) for g in range(GMLP_GROUPS)]
        chunks.append(jnp.concatenate(cols, axis=1) + b_full)
    mixed = chunks[0] if len(chunks) == 1 else jnp.concatenate(chunks, axis=0)
    return (u * mixed,)


def _glu_gate(zb):
    db = zb.shape[1] // 2
    return (zb[:, :db] * jax.nn.sigmoid(zb[:, db:]),)


def _ln_silu(yc, g, b):
    return (jax.nn.silu(_layer_norm(yc, g, b)),)


def _s5_act(ys, u, d_skip):
    return (jax.nn.gelu(ys + d_skip * u),)


def _merge(*args):
    g, ps, b = args[0], args[1:1 + N_BRANCH], args[1 + N_BRANCH]
    d = ps[0].shape[1]
    s = jax.nn.sigmoid(g + b)
    out = s[:, :d] * ps[0]
    for k in range(1, N_BRANCH):
        out = out + s[:, k * d:(k + 1) * d] * ps[k]
    return (out,)


def _loss_rows(x, tgt, g):
    y = x * lax.rsqrt(jnp.mean(x * x, axis=-1, keepdims=True) + EPS) * g
    e = y - tgt
    return (0.5 * jnp.mean(e * e, axis=-1, keepdims=True),)


def _adamw(w, g, m, v):
    m = ADAM_B1 * m + (1.0 - ADAM_B1) * g
    v = ADAM_B2 * v + (1.0 - ADAM_B2) * jnp.square(g)
    m_hat = m / (1.0 - ADAM_B1 ** ADAM_STEP)
    v_hat = v / (1.0 - ADAM_B2 ** ADAM_STEP)
    delta = -ADAM_LR * (m_hat / (jnp.sqrt(v_hat) + ADAM_EPS) + ADAM_WD * w)
    return delta, m, v


def _whole(fn, args, outs, *, name):
    n_a = len(args)

    def body(*refs):
        res = fn(*[x[...] for x in refs[:n_a]])
        for o_ref, o in zip(refs[n_a:], res):
            o_ref[...] = o.astype(o_ref.dtype)

    return tuple(pl.pallas_call(
        body, name=name,
        out_shape=[jax.ShapeDtypeStruct(s, d) for s, d in outs],
        compiler_params=_cparams(),
    )(*args))


def _whole_vjp(fn, args, cts, *, name):
    n_a, n_c = len(args), len(cts)

    def body(*refs):
        prim = [x[...] for x in refs[:n_a]]
        ct = [x[...] for x in refs[n_a:n_a + n_c]]
        _, vjp = jax.vjp(fn, *prim)
        for o_ref, g in zip(refs[n_a + n_c:], vjp(tuple(ct))):
            o_ref[...] = g

    return tuple(pl.pallas_call(
        body, name=name,
        out_shape=[jax.ShapeDtypeStruct(a.shape, F32) for a in args],
        compiler_params=_cparams(),
    )(*args, *cts))


def _conv_flags(i, n_lat, n_tiles):
    has_prev = jnp.logical_and(i != 0, i != n_lat)
    has_next = jnp.logical_and(i != n_lat - 1, i != n_tiles - 1)
    return has_prev, has_next


def _halo_specs(width, tm, n_rows):
    per = tm // CONV_HALO
    last = n_rows // CONV_HALO - 1
    prev = pl.BlockSpec((CONV_HALO, width), lambda i: (jnp.maximum(i * per - 1, 0), 0))
    cur = pl.BlockSpec((tm, width), lambda i: (i, 0))
    nxt = pl.BlockSpec((CONV_HALO, width), lambda i: (jnp.minimum((i + 1) * per, last), 0))
    return [prev, cur, nxt]


def _with_halo(prev, cur, nxt, has_prev, has_next):
    prev = jnp.where(has_prev, prev, 0.0)
    nxt = jnp.where(has_next, nxt, 0.0)
    return jnp.concatenate([prev, cur, nxt], axis=0)


def _conv_fwd(zb, conv_w, conv_b, ln_g, ln_b, *, n_lat, tm=ROW_TILE):
    r, db = zb.shape[0], zb.shape[1] // 2
    n_tiles = r // tm
    half = CONV_W // 2

    def body(zp_ref, zc_ref, zn_ref, w_ref, b_ref, g_ref, lb_ref, out_ref, y_ref, yc_ref):
        i = pl.program_id(0)
        has_prev, has_next = _conv_flags(i, n_lat, n_tiles)
        y = _glu_gate(zc_ref[...])[0]
        y_ext = _with_halo(_glu_gate(zp_ref[...])[0], y, _glu_gate(zn_ref[...])[0], has_prev, has_next)
        w = w_ref[...]
        acc = jnp.zeros((tm, db), F32) + b_ref[...]
        for k in range(CONV_W):
            s = CONV_HALO - half + k
            acc = acc + w[k:k + 1, :] * y_ext[s:s + tm, :]
        y_ref[...] = y
        yc_ref[...] = acc
        out_ref[...] = _ln_silu(acc, g_ref[...], lb_ref[...])[0].astype(out_ref.dtype)

    full = lambda p: pl.BlockSpec(p.shape, lambda i: (0,) * p.ndim)
    return pl.pallas_call(
        body, name="conv_fwd", grid=(n_tiles,),
        in_specs=_halo_specs(2 * db, tm, r) + [full(conv_w), full(conv_b), full(ln_g), full(ln_b)],
        out_specs=[pl.BlockSpec((tm, db), lambda i: (i, 0))] * 3,
        out_shape=[jax.ShapeDtypeStruct((r, db), BF16), jax.ShapeDtypeStruct((r, db), F32),
                   jax.ShapeDtypeStruct((r, db), F32)],
        compiler_params=_cparams(("parallel",)),
    )(zb, zb, zb, conv_w, conv_b, ln_g, ln_b)


def _conv_bwd(d_yc, y, zb, conv_w, *, n_lat, tm=ROW_TILE):
    r, db = y.shape
    n_tiles = r // tm
    half = CONV_W // 2

    def body(gp_ref, gc_ref, gn_ref, yp_ref, yc_ref, yn_ref, z_ref, w_ref, dz_ref, dw_ref, db_ref):
        i = pl.program_id(0)
        has_prev, has_next = _conv_flags(i, n_lat, n_tiles)
        g = gc_ref[...]
        g_ext = _with_halo(gp_ref[...], g, gn_ref[...], has_prev, has_next)
        y_ext = _with_halo(yp_ref[...], yc_ref[...], yn_ref[...], has_prev, has_next)
        w = w_ref[...]

        @pl.when(i == 0)
        def _():
            dw_ref[...] = jnp.zeros_like(dw_ref)
            db_ref[...] = jnp.zeros_like(db_ref)

        d_y = jnp.zeros((tm, db), F32)
        for k in range(CONV_W):
            s = CONV_HALO + half - k
            d_y = d_y + w[k:k + 1, :] * g_ext[s:s + tm, :]
            s = CONV_HALO - half + k
            dw_ref[pl.ds(k, 1), :] += jnp.sum(g * y_ext[s:s + tm, :], axis=0, keepdims=True)
        db_ref[...] += jnp.sum(g, axis=0, keepdims=True)
        _, vjp = jax.vjp(_glu_gate, z_ref[...])
        dz_ref[...] = vjp((d_y,))[0]

    return pl.pallas_call(
        body, name="conv_bwd", grid=(n_tiles,),
        in_specs=_halo_specs(db, tm, r) + _halo_specs(db, tm, r)
        + [pl.BlockSpec((tm, 2 * db), lambda i: (i, 0)), pl.BlockSpec(conv_w.shape, lambda i: (0, 0))],
        out_specs=[pl.BlockSpec((tm, 2 * db), lambda i: (i, 0)), pl.BlockSpec((CONV_W, db), lambda i: (0, 0)),
                   pl.BlockSpec((1, db), lambda i: (0, 0))],
        out_shape=[jax.ShapeDtypeStruct((r, 2 * db), F32), jax.ShapeDtypeStruct((CONV_W, db), F32),
                   jax.ShapeDtypeStruct((1, db), F32)],
        compiler_params=_cparams(("arbitrary",)),
    )(d_yc, d_yc, d_yc, y, y, y, zb, conv_w)


def _rope(x, cos, sin_signed, perm):
    return x * cos + jnp.dot(x, perm, precision=lax.Precision.HIGHEST, preferred_element_type=F32) * sin_signed


def _softmax3(s_loc, s_ctx, sink_col):
    m = sink_col
    if s_loc is not None:
        m = jnp.maximum(m, jnp.max(s_loc, axis=-1, keepdims=True))
    m = lax.stop_gradient(jnp.maximum(m, jnp.max(s_ctx, axis=-1, keepdims=True)))
    e_ctx = jnp.exp(s_ctx - m)
    den = jnp.sum(e_ctx, axis=-1, keepdims=True) + jnp.exp(sink_col - m)
    if s_loc is None:
        return None, e_ctx / den
    e_loc = jnp.exp(s_loc - m)
    den = den + jnp.sum(e_loc, axis=-1, keepdims=True)
    return e_loc / den, e_ctx / den


def _attn_latent(q4, kb, vb, kc, vc, sink_col, cq, sq, ck, sk, perm, allowed):
    qpk = q4.shape[0]
    scale = HEAD_DIM ** -0.5
    q = _rope(q4.reshape(qpk * BLOCK, HEAD_DIM), jnp.concatenate([cq] * qpk, 0), jnp.concatenate([sq] * qpk, 0), perm)
    k = _rope(kb, ck, sk, perm)
    s_loc = jnp.where(allowed, _mm_nt(q, k) * scale, NEG_INF)
    s_ctx = _mm_nt(q, kc) * scale
    p_loc, p_ctx = _softmax3(s_loc, s_ctx, sink_col)
    o = _mm_nn(p_loc, vb) + _mm_nn(p_ctx, vc)
    return o.reshape(qpk, BLOCK, HEAD_DIM)


def _attn_context(q4, kc, vc, sink_col):
    qpk = q4.shape[0]
    scale = HEAD_DIM ** -0.5
    s_ctx = _mm_nt(q4.reshape(qpk * BLOCK, HEAD_DIM), kc) * scale
    _, p_ctx = _softmax3(None, s_ctx, sink_col)
    return _mm_nn(p_ctx, vc).reshape(qpk, BLOCK, HEAD_DIM)


def _attn_specs(nkv, qpk, nq, n_ctx, s_len):
    blk = lambda off: (lambda i: (0, jnp.clip(i + off, 0, nq - 1), 0))
    tab = lambda off: (lambda i: (jnp.clip(i + off, 0, nq - 1), 0))
    q_spec = pl.BlockSpec((nkv, qpk, BLOCK, HEAD_DIM), lambda i: (0, 0, i, 0))
    band = [pl.BlockSpec((nkv, BLOCK, HEAD_DIM), blk(off)) for off in (-1, 0, 1)]
    ctx = pl.BlockSpec((nkv, n_ctx, HEAD_DIM), lambda i: (0, s_len // n_ctx, 0))
    sink = pl.BlockSpec((nkv, qpk * BLOCK, 1), lambda i: (0, 0, 0))
    tabs = [pl.BlockSpec((BLOCK, HEAD_DIM), tab(off)) for off in (-1, 0, 1)]
    perm = pl.BlockSpec((HEAD_DIM, HEAD_DIM), lambda i: (0, 0))
    return q_spec, band, ctx, sink, tabs, perm


def _attn_mask(i, qpk, s_len):
    qpos = i * BLOCK + lax.broadcasted_iota(jnp.int32, (BLOCK, 3 * BLOCK), 0)
    kpos = (i - 1) * BLOCK + lax.broadcasted_iota(jnp.int32, (BLOCK, 3 * BLOCK), 1)
    ok = (jnp.abs(qpos - kpos) <= WINDOW) & (kpos >= 0) & (kpos < s_len)
    return jnp.concatenate([ok] * qpk, axis=0)


def _attn_fwd(q, k, v, sink_rows, cos, sin_signed, perm, *, s_len):
    nkv, qpk, r, _ = q.shape
    nq, n_ctx = s_len // BLOCK, r - s_len
    q_spec, band, ctx, sink, tabs, perm_spec = _attn_specs(nkv, qpk, nq, n_ctx, s_len)

    def body(q_ref, kp, kc_, kn, vp, vc_, vn, kx, vx, sk_ref, cp, cc, cn, sp, sc, sn, perm_ref, o_ref):
        i = pl.program_id(0)

        @pl.when(i < nq)
        def _():
            allowed = _attn_mask(i, qpk, s_len)
            ck = jnp.concatenate([cp[...], cc[...], cn[...]], 0)
            sk = jnp.concatenate([sp[...], sc[...], sn[...]], 0)
            for g in range(nkv):
                kb = jnp.concatenate([kp[g], kc_[g], kn[g]], 0)
                vb = jnp.concatenate([vp[g], vc_[g], vn[g]], 0)
                o = _attn_latent(q_ref[g], kb, vb, kx[g], vx[g], sk_ref[g], cc[...], sc[...], ck, sk, perm_ref[...], allowed)
                o_ref[g] = o.astype(o_ref.dtype)

        @pl.when(i >= nq)
        def _():
            for g in range(nkv):
                o_ref[g] = _attn_context(q_ref[g], kx[g], vx[g], sk_ref[g]).astype(o_ref.dtype)

    return pl.pallas_call(
        body, name="attn_fwd", grid=(r // BLOCK,),
        in_specs=[q_spec] + band + band + [ctx, ctx, sink] + tabs + tabs + [perm_spec],
        out_specs=q_spec,
        out_shape=jax.ShapeDtypeStruct(q.shape, BF16),
        compiler_params=_cparams(("parallel",)),
    )(q, k, k, k, v, v, v, k, v, sink_rows, cos, cos, cos, sin_signed, sin_signed, sin_signed, perm)


def _attn_bwd(q, k, v, sink_rows, cos, sin_signed, perm, d_o, *, s_len):
    nkv, qpk, r, _ = q.shape
    nq, n_ctx = s_len // BLOCK, r - s_len
    n_steps = r // BLOCK
    q_spec, band, ctx, sink, tabs, perm_spec = _attn_specs(nkv, qpk, nq, n_ctx, s_len)

    def body(q_ref, kp, kc_, kn, vp, vc_, vn, kx, vx, sk_ref, cp, cc, cn, sp, sc, sn, perm_ref, do_ref,
             dq_ref, dk_hbm, dv_hbm, dsk_ref, dk_acc, dv_acc):
        i = pl.program_id(0)

        @pl.when(i == 0)
        def _():
            dk_acc[...] = jnp.zeros_like(dk_acc)
            dv_acc[...] = jnp.zeros_like(dv_acc)
            dsk_ref[...] = jnp.zeros_like(dsk_ref)

        ctx_rows = pl.ds(s_len, n_ctx)

        @pl.when(i < nq)
        def _():
            allowed = _attn_mask(i, qpk, s_len)
            ck = jnp.concatenate([cp[...], cc[...], cn[...]], 0)
            sk = jnp.concatenate([sp[...], sc[...], sn[...]], 0)
            for g in range(nkv):
                kb = jnp.concatenate([kp[g], kc_[g], kn[g]], 0)
                vb = jnp.concatenate([vp[g], vc_[g], vn[g]], 0)
                fn = lambda q4, kb_, vb_, kc, vc, s_col: _attn_latent(
                    q4, kb_, vb_, kc, vc, s_col, cc[...], sc[...], ck, sk, perm_ref[...], allowed)
                _, vjp = jax.vjp(fn, q_ref[g], kb, vb, kx[g], vx[g], sk_ref[g])
                dq4, dkb, dvb, dkc, dvc, dsk = vjp(do_ref[g].astype(F32))
                dq_ref[g] = dq4
                for seg, off in enumerate((-1, 0, 1)):
                    rows = pl.ds(pl.multiple_of(jnp.clip(i + off, 0, nq - 1) * BLOCK, BLOCK), BLOCK)
                    dk_acc[g, rows, :] += dkb[seg * BLOCK:(seg + 1) * BLOCK]
                    dv_acc[g, rows, :] += dvb[seg * BLOCK:(seg + 1) * BLOCK]
                dk_acc[g, ctx_rows, :] += dkc
                dv_acc[g, ctx_rows, :] += dvc
                dsk_ref[g] += dsk

        @pl.when(i >= nq)
        def _():
            for g in range(nkv):
                _, vjp = jax.vjp(_attn_context, q_ref[g], kx[g], vx[g], sk_ref[g])
                dq4, dkc, dvc, dsk = vjp(do_ref[g].astype(F32))
                dq_ref[g] = dq4
                dk_acc[g, ctx_rows, :] += dkc
                dv_acc[g, ctx_rows, :] += dvc
                dsk_ref[g] += dsk

        @pl.when(i == n_steps - 1)
        def _():
            pltpu.sync_copy(dk_acc, dk_hbm)
            pltpu.sync_copy(dv_acc, dv_hbm)

    any_spec = pl.BlockSpec(memory_space=pl.ANY)
    return pl.pallas_call(
        body, name="attn_bwd", grid=(n_steps,),
        in_specs=[q_spec] + band + band + [ctx, ctx, sink] + tabs + tabs + [perm_spec, q_spec],
        out_specs=[q_spec, any_spec, any_spec, sink],
        out_shape=[jax.ShapeDtypeStruct(q.shape, F32), jax.ShapeDtypeStruct(k.shape, F32),
                   jax.ShapeDtypeStruct(v.shape, F32), jax.ShapeDtypeStruct(sink_rows.shape, F32)],
        scratch_shapes=[pltpu.VMEM(k.shape, F32), pltpu.VMEM(v.shape, F32)],
        compiler_params=_cparams(("arbitrary",)),
    )(q, k, k, k, v, v, v, k, v, sink_rows, cos, cos, cos, sin_signed, sin_signed, sin_signed, perm, d_o)


def _scan_orders(n_lat, n_ctx):
    fwd = lambda i: jnp.where(i < n_ctx, n_lat + i, i - n_ctx)
    bwd = lambda i: jnp.where(i < n_ctx, n_lat + n_ctx - 1 - i, n_lat - 1 - (i - n_ctx))
    return fwd, bwd


def _s5_scan(bu_f, bu_b, lbar, *, n_lat, tb=ROW_TILE):
    r, ch2, _ = bu_f.shape
    ch = ch2 // 2
    n_tiles = r // tb
    of, ob = _scan_orders(n_lat, n_tiles - n_lat)

    def body(bf_ref, bb_ref, a_ref, sf_ref, sb_ref, st_ref):
        @pl.when(pl.program_id(0) == 0)
        def _():
            st_ref[...] = jnp.zeros_like(st_ref)

        afr, afi, abr, abi = a_ref[0], a_ref[1], a_ref[2], a_ref[3]

        def step(t, carry):
            sfr, sfi, sbr, sbi = carry
            x = bf_ref[t]
            nfr = afr * sfr - afi * sfi + x[:ch]
            nfi = afr * sfi + afi * sfr + x[ch:]
            sf_ref[t] = jnp.concatenate([nfr, nfi], axis=0)
            u = tb - 1 - t
            x = bb_ref[u]
            nbr = abr * sbr - abi * sbi + x[:ch]
            nbi = abr * sbi + abi * sbr + x[ch:]
            sb_ref[u] = jnp.concatenate([nbr, nbi], axis=0)
            return nfr, nfi, nbr, nbi

        out = lax.fori_loop(0, tb, step, (st_ref[0], st_ref[1], st_ref[2], st_ref[3]), unroll=4)
        for n in range(4):
            st_ref[n] = out[n]

    spec = lambda order: pl.BlockSpec((tb, ch2, LANES), lambda i: (order(i), 0, 0))
    return pl.pallas_call(
        body, name="s5_scan", grid=(n_tiles,),
        in_specs=[spec(of), spec(ob), pl.BlockSpec(lbar.shape, lambda i: (0, 0, 0))],
        out_specs=[spec(of), spec(ob)],
        out_shape=[jax.ShapeDtypeStruct(bu_f.shape, F32), jax.ShapeDtypeStruct(bu_b.shape, F32)],
        scratch_shapes=[pltpu.VMEM((4, ch, LANES), F32)],
        compiler_params=_cparams(("arbitrary",)),
    )(bu_f, bu_b, lbar)


def _s5_scan_bwd(ds_f, ds_b, s_f, s_b, lbar, *, n_lat, tb=ROW_TILE):
    r, ch2, _ = ds_f.shape
    ch = ch2 // 2
    n_tiles = r // tb
    of, ob = _scan_orders(n_lat, n_tiles - n_lat)
    rof = lambda i: of(n_tiles - 1 - i)
    rob = lambda i: ob(n_tiles - 1 - i)

    def body(gf_ref, gb_ref, sf_ref, sb_ref, a_ref, df_ref, db_ref, da_ref, st_ref):
        @pl.when(pl.program_id(0) == 0)
        def _():
            st_ref[...] = jnp.zeros_like(st_ref)
            da_ref[...] = jnp.zeros_like(da_ref)

        afr, afi, abr, abi = a_ref[0], a_ref[1], a_ref[2], a_ref[3]

        def one(g_re, g_im, acc_re, acc_im, a_re, a_im, s, ds):
            acc_re = acc_re + s[:ch] * g_re + s[ch:] * g_im
            acc_im = acc_im - s[ch:] * g_re + s[:ch] * g_im
            n_re = ds[:ch] + a_re * g_re + a_im * g_im
            n_im = ds[ch:] - a_im * g_re + a_re * g_im
            return n_re, n_im, acc_re, acc_im

        def step(t, carry):
            gfr, gfi, gbr, gbi, cfr, cfi, cbr, cbi = carry
            u = tb - 1 - t
            gfr, gfi, cfr, cfi = one(gfr, gfi, cfr, cfi, afr, afi, sf_ref[u], gf_ref[u])
            df_ref[u] = jnp.concatenate([gfr, gfi], axis=0)
            gbr, gbi, cbr, cbi = one(gbr, gbi, cbr, cbi, abr, abi, sb_ref[t], gb_ref[t])
            db_ref[t] = jnp.concatenate([gbr, gbi], axis=0)
            return gfr, gfi, gbr, gbi, cfr, cfi, cbr, cbi

        zero = jnp.zeros((ch, LANES), F32)
        out = lax.fori_loop(0, tb, step, (st_ref[0], st_ref[1], st_ref[2], st_ref[3], zero, zero, zero, zero), unroll=4)
        for n in range(4):
            st_ref[n] = out[n]
            da_ref[n] += out[4 + n]

    spec = lambda order: pl.BlockSpec((tb, ch2, LANES), lambda i: (order(i), 0, 0))
    return pl.pallas_call(
        body, name="s5_scan_bwd", grid=(n_tiles,),
        in_specs=[spec(rof), spec(rob), spec(rof), spec(rob), pl.BlockSpec(lbar.shape, lambda i: (0, 0, 0))],
        out_specs=[spec(rof), spec(rob), pl.BlockSpec(lbar.shape, lambda i: (0, 0, 0))],
        out_shape=[jax.ShapeDtypeStruct(ds_f.shape, F32), jax.ShapeDtypeStruct(ds_b.shape, F32),
                   jax.ShapeDtypeStruct(lbar.shape, F32)],
        scratch_shapes=[pltpu.VMEM((4, ch, LANES), F32)],
        compiler_params=_cparams(("arbitrary",)),
    )(ds_f, ds_b, s_f, s_b, lbar)


def _s5_discretise(a_re, a_im, log_step, b_re, b_im):
    dt = jnp.exp(log_step)
    mag = jnp.exp(a_re * dt)
    l_re, l_im = mag * jnp.cos(a_im * dt), mag * jnp.sin(a_im * dt)
    den = a_re * a_re + a_im * a_im
    q_re = ((l_re - 1.0) * a_re + l_im * a_im) / den
    q_im = (l_im * a_re - (l_re - 1.0) * a_im) / den
    bb_re = q_re * b_re[None] - q_im * b_im[None]
    bb_im = q_re * b_im[None] + q_im * b_re[None]
    return l_re, l_im, bb_re, bb_im


N_DEV = 8
N_CHIPS = 4


def _place():
    x, y, c = lax.axis_index("x"), lax.axis_index("y"), lax.axis_index("c")
    chips = [(1 - x, y), (x, 1 - y), (1 - x, 1 - y)]
    return x, y, c, chips


def _allgather8(v, *, reduce=False, name):
    m_per, n = v.shape

    def body(x_ref, out_ref, *scratch):
        if reduce:
            all_ref, send_sems, recv_sems, local_sem = scratch
        else:
            all_ref = out_ref
            send_sems, recv_sems, local_sem = scratch
        x, y, c, chips = _place()
        me, sibling = (x, y, c), (x, y, 1 - c)

        def rows(px, py, pc):
            return all_ref.at[pl.ds((4 * px + 2 * py + pc) * m_per, m_per), :]

        def copy(k, block, to, src=None):
            return pltpu.make_async_remote_copy(
                src_ref=rows(*block) if src is None else src, dst_ref=rows(*block),
                send_sem=send_sems.at[k], recv_sem=recv_sems.at[k], device_id=to, device_id_type=MESH)

        mine = pltpu.make_async_copy(x_ref, rows(*me), local_sem)
        mine.start()
        first = [copy(0, me, sibling, src=x_ref)]
        first += [copy(1 + j, me, (*chip, c), src=x_ref) for j, chip in enumerate(chips)]
        for cp in first:
            cp.start()
        passed = [copy(4 + j, (*chip, c), sibling) for j, chip in enumerate(chips)]
        for j, chip in enumerate(chips):
            copy(1 + j, (*chip, c), me).wait_recv()
            passed[j].start()
        copy(0, sibling, me).wait_recv()
        for j, chip in enumerate(chips):
            copy(4 + j, (*chip, 1 - c), me).wait_recv()
        for cp in first + passed:
            cp.wait_send()
        mine.wait()
        if reduce:
            acc = all_ref[pl.ds(0, m_per), :]
            for d in range(1, N_DEV):
                acc = acc + all_ref[pl.ds(d * m_per, m_per), :]
            out_ref[...] = acc

    sems = [pltpu.SemaphoreType.DMA((7,)), pltpu.SemaphoreType.DMA((7,)), pltpu.SemaphoreType.DMA]
    return pl.pallas_call(
        body, name=name,
        out_shape=jax.ShapeDtypeStruct((m_per if reduce else N_DEV * m_per, n), v.dtype),
        in_specs=[pl.BlockSpec(memory_space=pltpu.VMEM)],
        out_specs=pl.BlockSpec(memory_space=pltpu.VMEM),
        scratch_shapes=([pltpu.VMEM((N_DEV * m_per, n), v.dtype)] if reduce else []) + sems,
        compiler_params=_cparams(),
    )(v)


def _halves(ref, half):
    h = ref.shape[0] // 2
    return ref.at[pl.ds(half * h, h)]


def _gather_shards(stacks):
    n_w = len(stacks)

    def body(*refs):
        out_refs = refs[n_w:2 * n_w]
        send_sems, recv_sems = refs[2 * n_w:]
        x, y, c, chips = _place()
        me, sibling = (x, y, c), (x, y, 1 - c)
        chip_no = lambda px, py: 2 * px + py

        def copy(n, k, chip, half, to):
            blk = _halves(out_refs[n].at[chip_no(*chip)], half)
            return pltpu.make_async_remote_copy(
                src_ref=blk, dst_ref=blk, send_sem=send_sems.at[n, k], recv_sem=recv_sems.at[n, k],
                device_id=to, device_id_type=MESH)

        first = [copy(n, j, (x, y), c, (*chip, c)) for n in range(n_w) for j, chip in enumerate(chips)]
        for cp in first:
            cp.start()
        passed = []
        for n in range(n_w):
            for j, chip in enumerate(chips):
                copy(n, j, chip, c, me).wait_recv()
                passed.append(copy(n, 3 + j, chip, c, sibling))
                passed[-1].start()
        for n in range(n_w):
            for j, chip in enumerate(chips):
                copy(n, 3 + j, chip, 1 - c, me).wait_recv()
        for cp in first + passed:
            cp.wait_send()

    any_spec = pl.BlockSpec(memory_space=pl.ANY)
    return pl.pallas_call(
        body, name="gather_shards",
        out_shape=[jax.ShapeDtypeStruct(s.shape, s.dtype) for s in stacks],
        in_specs=[any_spec] * n_w, out_specs=[any_spec] * n_w,
        input_output_aliases={n: n for n in range(n_w)},
        scratch_shapes=[pltpu.SemaphoreType.DMA((n_w, 6)), pltpu.SemaphoreType.DMA((n_w, 6))],
        compiler_params=_cparams(),
    )(*stacks)


def _rs_swap_in(grads):
    n_w = len(grads)

    def body(*refs):
        x_refs, out_refs = refs[:n_w], refs[n_w:2 * n_w]
        send_sems, recv_sems = refs[2 * n_w:]
        x, y, c, _ = _place()
        cps = []
        for n in range(n_w):
            h = x_refs[n].shape[1] // 2
            cps.append(pltpu.make_async_remote_copy(
                src_ref=x_refs[n].at[:, pl.ds((1 - c) * h, h)], dst_ref=out_refs[n],
                send_sem=send_sems.at[n], recv_sem=recv_sems.at[n], device_id=(x, y, 1 - c), device_id_type=MESH))
        for cp in cps:
            cp.start()
        for cp in cps:
            cp.wait()

    any_spec = pl.BlockSpec(memory_space=pl.ANY)
    return pl.pallas_call(
        body, name="rs_swap_in",
        out_shape=[jax.ShapeDtypeStruct((g.shape[0], g.shape[1] // 2) + g.shape[2:], g.dtype) for g in grads],
        in_specs=[any_spec] * n_w, out_specs=[any_spec] * n_w,
        scratch_shapes=[pltpu.SemaphoreType.DMA((n_w,)), pltpu.SemaphoreType.DMA((n_w,))],
        compiler_params=_cparams(),
    )(*grads)


def _chip_exchange(parts):
    n_w = len(parts)

    def body(*refs):
        x_refs, out_refs = refs[:n_w], refs[n_w:2 * n_w]
        send_sems, recv_sems = refs[2 * n_w:]
        x, y, c, chips = _place()
        cps = [pltpu.make_async_remote_copy(
            src_ref=x_refs[n].at[2 * chip[0] + chip[1]], dst_ref=out_refs[n].at[j],
            send_sem=send_sems.at[n, j], recv_sem=recv_sems.at[n, j], device_id=(*chip, c), device_id_type=MESH)
            for n in range(n_w) for j, chip in enumerate(chips)]
        for cp in cps:
            cp.start()
        for cp in cps:
            cp.wait()

    any_spec = pl.BlockSpec(memory_space=pl.ANY)
    return pl.pallas_call(
        body, name="chip_exchange",
        out_shape=[jax.ShapeDtypeStruct((3,) + v.shape[1:], v.dtype) for v in parts],
        in_specs=[any_spec] * n_w, out_specs=[any_spec] * n_w,
        scratch_shapes=[pltpu.SemaphoreType.DMA((n_w, 3)), pltpu.SemaphoreType.DMA((n_w, 3))],
        compiler_params=_cparams(),
    )(*parts)


def _rs_finish(reduced):
    n_w = len(reduced)

    def body(*refs):
        out_refs = refs[n_w:2 * n_w]
        send_sems, recv_sems = refs[2 * n_w:]
        x, y, c, _ = _place()

        def copy(n, half):
            blk = _halves(out_refs[n], half)
            return pltpu.make_async_remote_copy(
                src_ref=blk, dst_ref=blk, send_sem=send_sems.at[n], recv_sem=recv_sems.at[n],
                device_id=(x, y, 1 - c), device_id_type=MESH)

        for n in range(n_w):
            copy(n, c).start()
        for n in range(n_w):
            copy(n, c).wait_send()
            copy(n, 1 - c).wait_recv()

    any_spec = pl.BlockSpec(memory_space=pl.ANY)
    return pl.pallas_call(
        body, name="rs_finish",
        out_shape=[jax.ShapeDtypeStruct(v.shape, v.dtype) for v in reduced],
        in_specs=[any_spec] * n_w, out_specs=[any_spec] * n_w,
        input_output_aliases={n: n for n in range(n_w)},
        scratch_shapes=[pltpu.SemaphoreType.DMA((n_w,)), pltpu.SemaphoreType.DMA((n_w,))],
        compiler_params=_cparams(),
    )(*reduced)


def _as_rows(shape):
    return (math.prod(shape[:-1]), shape[-1])


def _row_tile(rows, cols):
    return _pick(rows, tuple(t for t in (2048, 1024, 512, 256, 128, 64, 32, 16, 8) if t * cols * 4 <= (1 << 21)))


def _pair_sum(g, t, c_idx):
    rows, cols = _as_rows(t.shape[1:])
    tr = _row_tile(rows, cols)

    def body(c_ref, g_ref, t_ref, o_ref):
        o_ref[...] = g_ref[...] + t_ref[...]

    out = pl.pallas_call(
        body, name="rs_pair_sum",
        grid_spec=pltpu.PrefetchScalarGridSpec(
            num_scalar_prefetch=1, grid=(N_CHIPS, rows // tr),
            in_specs=[pl.BlockSpec((None, None, tr, cols), lambda j, i, c: (j, c[0], i, 0)),
                      pl.BlockSpec((None, tr, cols), lambda j, i, c: (j, i, 0))],
            out_specs=pl.BlockSpec((None, tr, cols), lambda j, i, c: (j, i, 0))),
        out_shape=jax.ShapeDtypeStruct((N_CHIPS, rows, cols), F32),
        compiler_params=_cparams(("parallel", "parallel")),
    )(c_idx, g.reshape(N_CHIPS, 2, rows, cols), t.reshape(N_CHIPS, rows, cols))
    return out.reshape(t.shape)


def _chip_sum(q, u, jc_idx):
    rows, cols = _as_rows(q.shape[1:])
    tr = _row_tile(rows, cols)
    nb = rows // tr

    def body(jc_ref, q_ref, u_ref, o_ref):
        o_ref[...] = ((q_ref[...] + u_ref[0]) + u_ref[1]) + u_ref[2]

    out = pl.pallas_call(
        body, name="rs_chip_sum",
        grid_spec=pltpu.PrefetchScalarGridSpec(
            num_scalar_prefetch=1, grid=(nb,),
            in_specs=[pl.BlockSpec((None, tr, cols), lambda i, jc: (jc[0], i, 0)),
                      pl.BlockSpec((3, tr, cols), lambda i, jc: (0, i, 0))],
            out_specs=pl.BlockSpec((tr, cols), lambda i, jc: (jc[1] * nb + i, 0))),
        out_shape=jax.ShapeDtypeStruct((2 * rows, cols), F32),
        compiler_params=_cparams(("parallel",)),
    )(jc_idx, q.reshape(N_CHIPS, rows, cols), u.reshape(3, rows, cols))
    return out.reshape((2 * q.shape[1],) + q.shape[2:])


def _reduce_scatter(grads):
    x, y, c = lax.axis_index("x"), lax.axis_index("y"), lax.axis_index("c")
    c_idx = jnp.reshape(c, (1,)).astype(jnp.int32)
    jc_idx = jnp.stack([2 * x + y, c]).astype(jnp.int32)
    theirs = _rs_swap_in(grads)
    pair = [_pair_sum(g, t, c_idx) for g, t in zip(grads, theirs)]
    got = _chip_exchange(pair)
    reduced = [_chip_sum(q, u, jc_idx) for q, u in zip(pair, got)]
    return _rs_finish(reduced)


WEIGHT_NAMES = ('c_ctx', 'w_mod', 'b_mod', 'norm1_g', 'norm2_g', 'w_in', 'gmlp_ln_g', 'gmlp_ln_b', 'gmlp_ws',
                'gmlp_bs', 'conv_w', 'conv_b', 'conv_ln_g', 'conv_ln_b', 'attn_sink', 's5_a_re', 's5_a_im',
                's5_log_step', 's5_b_re', 's5_b_im', 's5_c_re', 's5_c_im', 's5_d', 's5_w_glu', 'w_branch',
                'w_gate', 'b_gate', 'w_out', 'w_ff1', 'w_ff2', 'final_g')
BIG = ('w_in', 's5_w_glu', 'w_branch', 'w_gate', 'w_out', 'w_ff1', 'w_ff2')
SMALL = ('norm1_g', 'norm2_g', 'gmlp_ln_g', 'gmlp_ln_b', 'gmlp_ws', 'gmlp_bs', 'conv_w', 'conv_b', 'conv_ln_g',
         'conv_ln_b', 'attn_sink', 's5_a_re', 's5_a_im', 's5_log_step', 's5_b_re', 's5_b_im', 's5_c_re',
         's5_c_im', 's5_d', 'b_gate')
SMALL_SHARDED = ('conv_w', 'b_gate')


def _from_shards(name, sh):
    j = sh.shape[0]
    if name in ('w_in', 's5_w_glu', 'w_ff1'):
        return sh.transpose(1, 0, 2).reshape(sh.shape[1], j * sh.shape[2])
    if name == 'w_branch':
        return sh.transpose(1, 2, 0, 3).reshape(sh.shape[1], sh.shape[2], j * sh.shape[3])
    if name == 'w_gate':
        return sh.transpose(0, 2, 1, 3).reshape(j * sh.shape[2], sh.shape[1] * sh.shape[3])
    return sh.reshape(j * sh.shape[1], sh.shape[2])


def _to_shards(name, dw, j=N_CHIPS):
    if name in ('w_in', 's5_w_glu', 'w_ff1'):
        return dw.reshape(dw.shape[0], j, dw.shape[1] // j).transpose(1, 0, 2)
    if name == 'w_branch':
        return dw.reshape(dw.shape[0], dw.shape[1], j, dw.shape[2] // j).transpose(2, 0, 1, 3)
    if name == 'w_gate':
        d = dw.shape[0]
        return dw.reshape(j, d // j, dw.shape[1] // d, d).transpose(0, 2, 1, 3)
    return dw.reshape(j, dw.shape[0] // j, dw.shape[1])


def _pack_rows(flat_parts, lead):
    flat = jnp.concatenate(flat_parts, axis=-1)
    assert flat.shape[-1] % LANES == 0, flat.shape
    return flat.reshape(lead + (flat.shape[-1] // LANES, LANES))


def _block_diag(blocks):
    g, a, b = blocks.shape
    eye = jnp.eye(g, dtype=blocks.dtype)
    return (blocks[:, :, None, :] * eye[:, None, :, None]).reshape(g * a, g * b)


def _diag_blocks(mat, g):
    a, b = mat.shape[0] // g, mat.shape[1] // g
    eye = jnp.eye(g, dtype=mat.dtype)
    return (mat.reshape(g, a, g, b) * eye[:, None, :, None]).sum(axis=2)


def _rope_tables(s_len):
    rows = s_len // GRID_W
    row = jnp.repeat(jnp.arange(rows), GRID_W).astype(F32)
    col = jnp.tile(jnp.arange(GRID_W), rows).astype(F32)
    d = HEAD_DIM // 2
    inv = ROPE_BASE ** (-jnp.arange(0, d, 2, dtype=F32) / d)
    ar, ac = row[:, None] * inv[None, :], col[:, None] * inv[None, :]
    cos = jnp.concatenate([jnp.cos(ar), jnp.cos(ar), jnp.cos(ac), jnp.cos(ac)], axis=1)
    sin_signed = jnp.concatenate([-jnp.sin(ar), jnp.sin(ar), -jnp.sin(ac), jnp.sin(ac)], axis=1)
    idx = jnp.arange(HEAD_DIM)
    partner = jnp.where(idx % d < d // 2, idx + d // 2, idx - d // 2)
    perm = (idx[:, None] == partner[None, :]).astype(F32)
    return cos, sin_signed, perm


def _loss_and_grad(xa, tgt, g, *, s_len, tm=ROW_TILE):
    d = xa.shape[1]

    def body(x_ref, t_ref, g_ref, l_ref, dx_ref, dg_ref):
        i = pl.program_id(0)
        (rows,), vjp = jax.vjp(_loss_rows, x_ref[...], t_ref[...], g_ref[...])
        dx, _, dg = vjp((jnp.ones_like(rows),))
        dx_ref[...] = dx
        part = jnp.zeros(l_ref.shape, F32) + jnp.sum(rows)

        @pl.when(i == 0)
        def _():
            l_ref[...] = part
            dg_ref[...] = dg

        @pl.when(i > 0)
        def _():
            l_ref[...] += part
            dg_ref[...] += dg

    return pl.pallas_call(
        body, name="loss_head", grid=(s_len // tm,),
        in_specs=[pl.BlockSpec((tm, d), lambda i: (i, 0)), pl.BlockSpec((tm, d), lambda i: (i, 0)),
                  pl.BlockSpec((1, d), lambda i: (0, 0))],
        out_specs=[pl.BlockSpec((SUBLANES, LANES), lambda i: (0, 0)), pl.BlockSpec((tm, d), lambda i: (i, 0)),
                   pl.BlockSpec((1, d), lambda i: (0, 0))],
        out_shape=[jax.ShapeDtypeStruct((SUBLANES, LANES), F32), jax.ShapeDtypeStruct((s_len, d), F32),
                   jax.ShapeDtypeStruct((1, d), F32)],
        compiler_params=_cparams(("arbitrary",)),
    )(xa, tgt, g)


def _adamw_call(w, g, m, v, *, name):
    shape = w.shape
    cols = shape[-1] if w.ndim > 1 else LANES
    two_d = lambda a: a.reshape(-1, cols)
    rows = two_d(w).shape[0]
    tm = _pick(rows, tuple(t for t in (1024, 512, 256, 128, 64, 32, 16, 8) if t * cols * 4 <= (1 << 20)))
    outs = _rowwise(_adamw, [two_d(w), two_d(g), two_d(m), two_d(v)], [], [(cols, F32)] * 3, name=name, tm=tm)
    return tuple(o.reshape(shape) for o in outs)


def kernel(x, c, ctx, c_ctx, w_mod, b_mod, norm1_g, norm2_g, w_in, gmlp_ln_g, gmlp_ln_b, gmlp_ws, gmlp_bs, conv_w, conv_b, conv_ln_g, conv_ln_b, attn_sink, s5_a_re, s5_a_im, s5_log_step, s5_b_re, s5_b_im, s5_c_re, s5_c_im, s5_d, s5_w_glu, w_branch, w_gate, b_gate, w_out, w_ff1, w_ff2, final_g, loss_target, m_c_ctx, m_w_mod, m_b_mod, m_norm1_g, m_norm2_g, m_w_in, m_gmlp_ln_g, m_gmlp_ln_b, m_gmlp_ws, m_gmlp_bs, m_conv_w, m_conv_b, m_conv_ln_g, m_conv_ln_b, m_attn_sink, m_s5_a_re, m_s5_a_im, m_s5_log_step, m_s5_b_re, m_s5_b_im, m_s5_c_re, m_s5_c_im, m_s5_d, m_s5_w_glu, m_w_branch, m_w_gate, m_b_gate, m_w_out, m_w_ff1, m_w_ff2, m_final_g, v_c_ctx, v_w_mod, v_b_mod, v_norm1_g, v_norm2_g, v_w_in, v_gmlp_ln_g, v_gmlp_ln_b, v_gmlp_ws, v_gmlp_bs, v_conv_w, v_conv_b, v_conv_ln_g, v_conv_ln_b, v_attn_sink, v_s5_a_re, v_s5_a_im, v_s5_log_step, v_s5_b_re, v_s5_b_im, v_s5_c_re, v_s5_c_im, v_s5_d, v_s5_w_glu, v_w_branch, v_w_gate, v_b_gate, v_w_out, v_w_ff1, v_w_ff2, v_final_g):
    given = dict(locals())
    p = {n: given[n] for n in WEIGHT_NAMES}
    mom = {n: given["m_" + n] for n in WEIGHT_NAMES}
    var = {n: given["v_" + n] for n in WEIGHT_NAMES}

    s_len, d = x.shape[1], x.shape[2]
    n_ctx = ctx.shape[1]
    r = s_len + n_ctx
    n_layers = w_mod.shape[0]
    db = d // N_BRANCH
    n_q = db // HEAD_DIM
    qpk = n_q // N_KV_HEADS
    g5 = db // S5_GW
    gp = g5 * S5_STATE
    ch = gp // LANES
    gw = db // GMLP_GROUPS
    n_lat = s_len // ROW_TILE
    assert s_len % ROW_TILE == 0 and n_ctx % ROW_TILE == 0 and s_len % n_ctx == 0 and gp % LANES == 0
    ax, ay, ac = lax.axis_index("x"), lax.axis_index("y"), lax.axis_index("c")
    j_me = 2 * ax + ay
    d_me = 4 * ax + 2 * ay + ac
    rw = functools.partial(_rowwise, s_rows=s_len)
    rwv = functools.partial(_rowwise_vjp, s_rows=s_len)
    add_epi = lambda acc, t: (acc + t,)

    c_all = _allgather8(jnp.broadcast_to(c, (SUBLANES, d)), name="gather_c")[::SUBLANES]
    c16 = jnp.concatenate([c_all, jnp.broadcast_to(c_ctx[None], (N_DEV, d))], axis=0)
    silu_fn = lambda a: (jax.nn.silu(a),)
    cond16 = _whole(silu_fn, [c16], [(c16.shape, F32)], name="silu_c")[0]
    ncol = w_mod.shape[2]
    b_mod_sh = lax.dynamic_slice_in_dim(b_mod, j_me * ncol, ncol, axis=1)
    mod_part = jnp.concatenate([
        _matmul(cond16, w_mod[l], tm=2 * N_DEV, rowvecs=[b_mod_sh[l][None]], epi=add_epi, name="mod_proj")
        for l in range(n_layers)], axis=0)
    mod_all = _allgather8(mod_part, name="gather_mod").reshape(N_CHIPS, 2, n_layers, 2 * N_DEV, ncol)[:, 0]
    mod_all = mod_all.transpose(1, 2, 0, 3).reshape(n_layers, 2 * N_DEV, N_CHIPS * ncol)

    def mods_of(l):
        two = jnp.stack([lax.dynamic_index_in_dim(mod_all[l], d_me, 0, keepdims=False), mod_all[l, N_DEV]])
        return [two[:, None, k * d:(k + 1) * d] for k in range(N_MOD)]

    d_sh, f_all = d // N_CHIPS, w_ff1.shape[2] * N_CHIPS
    f_sh = f_all // N_CHIPS

    def gather_layer(l):
        stacks = []
        for n in BIG:
            sh = p[n][l].astype(BF16)
            at = [j_me.astype(jnp.int32)] + [jnp.zeros((), jnp.int32)] * sh.ndim
            stacks.append(lax.dynamic_update_slice(jnp.zeros((N_CHIPS,) + sh.shape, BF16), sh[None], at))
        st = dict(zip(BIG, _gather_shards(stacks)))
        w = {n: _from_shards(n, st[n]) for n in ("w_in", "s5_w_glu", "w_branch")}
        w["w_out"] = st["w_out"].reshape(d, d)
        w["w_ff2"] = st["w_ff2"].reshape(f_all, d)
        w["w_gate"], w["w_ff1"] = st["w_gate"], st["w_ff1"]
        return w

    g_tn = min(1024, d)
    gate_fwd = dict(tk=d, tn=g_tn, b_blocks=(N_BRANCH * d, (N_CHIPS, None, d_sh, g_tn),
                                             lambda i, j, kk: (0, j // (d // g_tn), 0, j % (d // g_tn))))
    gate_bwd = dict(tb=True, tn=2 * d_sh, tk=d, b_blocks=(d, (2, None, d_sh, d), lambda i, j, kk: (j, kk, 0, 0)))
    gw_tn = min(2048, d)
    gate_wgt = dict(ta=True, tm=d_sh, tn=gw_tn, out_blocks=(
        (N_CHIPS, N_BRANCH, d_sh, d), (None, None, d_sh, gw_tn),
        lambda i, j, kk: (i, j // (d // gw_tn), 0, j % (d // gw_tn))))
    f_tn = min(1024, f_sh)
    ff1_fwd = dict(tn=f_tn, tk=d, b_blocks=(f_all, (None, d, f_tn), lambda i, j, kk: (j // (f_sh // f_tn), 0, j % (f_sh // f_tn))))
    f_tk = min(2048, f_sh)
    fb_tn = min(1024, d)
    ff1_bwd = dict(tb=True, tn=fb_tn, tk=f_tk, b_blocks=(
        d, (None, fb_tn, f_tk), lambda i, j, kk: (kk // (f_sh // f_tk), j, kk % (f_sh // f_tk))))
    fw_tn, fw_tm = min(2048, f_sh), min(1024, d)
    ff1_wgt = dict(ta=True, tm=fw_tm, tn=fw_tn, out_blocks=(
        (N_CHIPS, d, f_sh), (None, fw_tm, fw_tn), lambda i, j, kk: (j // (f_sh // fw_tn), i, j % (f_sh // fw_tn))))

    def pad_rows(flat):
        n = -(-flat.shape[0] // (SUBLANES * LANES)) * SUBLANES * LANES
        return jnp.pad(flat, (0, n - flat.shape[0])).reshape(n // LANES, LANES)

    sh_flat = pad_rows(jnp.concatenate([conv_w.reshape(-1), b_gate.reshape(-1)]))
    sh_all = _allgather8(sh_flat, name="gather_small_w").reshape(N_CHIPS, 2, -1)[:, 0]
    conv_w_full = sh_all[:, :conv_w.size].reshape((N_CHIPS,) + conv_w.shape).transpose(1, 2, 0, 3)
    conv_w_full = conv_w_full.reshape(n_layers, CONV_W, db)
    b_gate_full = sh_all[:, conv_w.size:conv_w.size + b_gate.size].reshape((N_CHIPS,) + b_gate.shape)
    b_gate_full = b_gate_full.transpose(1, 2, 0, 3).reshape(n_layers, 1, N_BRANCH * d)

    cos, sin_signed, perm = _rope_tables(s_len)
    xa = jnp.concatenate([x[0], ctx[0]], axis=0)
    col = lambda a: a[None] if a.ndim == 1 else a
    to_heads = lambda a, nh: a.reshape(r, N_KV_HEADS, nh // N_KV_HEADS, HEAD_DIM).transpose(1, 2, 0, 3)
    from_heads = lambda a: a.transpose(2, 0, 1, 3).reshape(r, -1)
    sink_rows_of = lambda l: jnp.repeat(attn_sink[l].reshape(N_KV_HEADS, qpk, 1), BLOCK, axis=1).reshape(
        N_KV_HEADS, qpk * BLOCK, 1)
    slab = lambda a: a.reshape(r, 2 * ch, LANES)
    flat2 = lambda a: a.reshape(r, 2 * gp)

    saved = []
    for l in range(n_layers):
        w = gather_layer(l)
        shift1, scale1, gate1, shift2, scale2, gate2 = mods_of(l)
        n1g, n2g = col(norm1_g[l]), col(norm2_g[l])
        h = rw(_rms_mod, [xa], [("full", n1g), ("mod", scale1), ("mod", shift1)], [(d, BF16)], name="norm1")[0]
        z = _matmul(h, w["w_in"], name="in_proj")
        o1, o2, o3, o4, o5 = 2 * db, 4 * db, 4 * db + n_q * HEAD_DIM, 4 * db + (n_q + N_KV_HEADS) * HEAD_DIM, \
            4 * db + (n_q + 2 * N_KV_HEADS) * HEAD_DIM
        za, zb, zq, zk, zv, zd = z[:, :o1], z[:, o1:o2], z[:, o2:o3], z[:, o3:o4], z[:, o4:o5], z[:, o5:]
        b_full = jnp.repeat(gmlp_bs[l].T, gw, axis=1)
        gmlp_params = [("full", col(gmlp_ln_g[l])), ("full", col(gmlp_ln_b[l])), ("full", gmlp_ws[l]), ("full", b_full)]
        br_a = rw(_gmlp, [za], gmlp_params, [(db, BF16)], name="gmlp")[0]
        conv_params = (conv_w_full[l], col(conv_b[l]), col(conv_ln_g[l]), col(conv_ln_b[l]))
        br_b, y_conv, yc_conv = _conv_fwd(zb, *conv_params, n_lat=n_lat)
        q4, k3, v3 = to_heads(zq, n_q), to_heads(zk, N_KV_HEADS)[:, 0], to_heads(zv, N_KV_HEADS)[:, 0]
        sink_rows = sink_rows_of(l)
        br_c = from_heads(_attn_fwd(q4, k3, v3, sink_rows, cos, sin_signed, perm, s_len=s_len))
        disc_in = [s5_a_re[l].reshape(2, gp, 1), s5_a_im[l].reshape(2, gp, 1),
                   jnp.broadcast_to(s5_log_step[l][:, :, None], (2, g5, S5_STATE)).reshape(2, gp, 1),
                   s5_b_re[l].reshape(gp, S5_GW), s5_b_im[l].reshape(gp, S5_GW)]
        l_re, l_im, bb_re, bb_im = _whole(
            _s5_discretise, disc_in, [((2, gp, 1), F32)] * 2 + [((2, gp, S5_GW), F32)] * 2, name="s5_disc")
        lbar = jnp.stack([l_re[0], l_im[0], l_re[1], l_im[1]]).reshape(4, ch, LANES)
        bd_of = lambda a: _block_diag(a.reshape(g5, S5_STATE, S5_GW).transpose(0, 2, 1))
        cd_of = lambda a: _block_diag(a.transpose(0, 2, 1))
        bd = [jnp.concatenate([bd_of(bb_re[k]), bd_of(bb_im[k])], axis=1).astype(BF16) for k in range(2)]
        cd = [jnp.concatenate([cd_of(s5_c_re[l, k]), -cd_of(s5_c_im[l, k])], axis=0).astype(BF16) for k in range(2)]
        bu_f = _matmul(zd, bd[0], name="s5_in_f")
        bu_b = _matmul(zd, bd[1], name="s5_in_b")
        st_f, st_b = _s5_scan(slab(bu_f), slab(bu_b), lbar, n_lat=n_lat)
        ys = _matmul(flat2(st_f), cd[0], name="s5_out_f")
        ys = _matmul(flat2(st_b), cd[1], tiles=[ys], epi=add_epi, name="s5_out_b")
        d_skip = col(s5_d[l])
        yg = rw(_s5_act, [ys, zd], [("full", d_skip)], [(db, BF16)], name="s5_act")[0]
        glu_pre = _matmul(yg, w["s5_w_glu"], name="s5_glu_proj")
        br_d = rw(_glu_gate, [glu_pre], [], [(db, BF16)], name="s5_glu")[0]
        branches = (br_a, br_b, br_c, br_d)
        gates = _matmul(h, w["w_gate"], name="gate_proj", **gate_fwd)
        projs = [_matmul(branches[k], w["w_branch"][k], name="branch_proj") for k in range(N_BRANCH)]
        merged = rw(_merge, [gates] + projs, [("full", b_gate_full[l])], [(d, BF16)], name="merge", tm=ROW_TILE // 2)[0]
        o = _matmul(merged, w["w_out"], name="out_proj")
        x1 = rw(_resid, [xa, o], [("mod", gate1)], [(d, F32)], name="resid1")[0]
        h2 = rw(_rms_mod, [x1], [("full", n2g), ("mod", scale2), ("mod", shift2)], [(d, BF16)], name="norm2")[0]
        f1, act = _matmul(h2, w["w_ff1"], out_dtypes=(F32, BF16), name="ff1",
                          epi=lambda acc: (acc, jnp.square(jnp.maximum(acc, 0.0))), **ff1_fwd)
        o_ff = _matmul(act, w["w_ff2"], name="ff2")
        x2 = rw(_resid, [x1, o_ff], [("mod", gate2)], [(d, F32)], name="resid2")[0]
        saved.append(dict(
            w=w, mods=(shift1, scale1, gate1, shift2, scale2, gate2), n1g=n1g, n2g=n2g, xa=xa, h=h, z=z,
            gmlp_params=gmlp_params, conv_params=conv_params, y_conv=y_conv, yc_conv=yc_conv, q4=q4, k3=k3, v3=v3,
            sink_rows=sink_rows, disc_in=disc_in, lbar=lbar, bd=bd, cd=cd, st_f=st_f, st_b=st_b, ys=ys,
            d_skip=d_skip, yg=yg, glu_pre=glu_pre, branches=branches, gates=gates, projs=projs, merged=merged,
            o=o, x1=x1, h2=h2, f1=f1, act=act, o_ff=o_ff))
        xa = x2

    loss_sum, dx_lat, d_final_g = _loss_and_grad(xa, loss_target[0], col(final_g), s_len=s_len)
    loss = lax.psum(loss_sum[0, 0], ("x", "y", "c"))
    dxa = jnp.concatenate([dx_lat, jnp.zeros((n_ctx, d), F32)], axis=0)

    big_grads = [None] * n_layers
    small_grads = [None] * n_layers
    d_mods = [None] * n_layers
    o1, o2, o3, o4, o5 = 2 * db, 4 * db, 4 * db + n_q * HEAD_DIM, 4 * db + (n_q + N_KV_HEADS) * HEAD_DIM, \
        4 * db + (n_q + 2 * N_KV_HEADS) * HEAD_DIM
    for l in reversed(range(n_layers)):
        sv = saved[l]
        w = sv["w"]
        shift1, scale1, gate1, shift2, scale2, gate2 = sv["mods"]
        z = sv["z"]
        za, zb, zd = z[:, :o1], z[:, o1:o2], z[:, o5:]
        bg, sg = {}, {}
        (d_x1, d_off), (d_gate2,) = rwv(_resid, [sv["x1"], sv["o_ff"]], [("mod", gate2)], [dxa],
                                        row_grads=[F32, BF16], name="resid2_b")
        d_f1 = _matmul(d_off, w["w_ff2"], tb=True, tiles=[sv["f1"]], out_dtypes=(BF16,), name="ff2_b",
                       epi=lambda acc, f: (acc * (2.0 * jnp.maximum(f, 0.0)),))
        bg["w_ff2"] = _matmul(sv["act"], d_off, ta=True, name="ff2_w").reshape(N_CHIPS, f_sh, d)
        d_h2 = _matmul(d_f1, w["w_ff1"], name="ff1_b", **ff1_bwd)
        bg["w_ff1"] = _matmul(sv["h2"], d_f1, name="ff1_w", **ff1_wgt)
        (d_x1,), (sg["norm2_g"], d_scale2, d_shift2) = rwv(
            _rms_mod, [sv["x1"]], [("full", sv["n2g"]), ("mod", scale2), ("mod", shift2)], [d_h2],
            row_grads=[F32], adds={0: d_x1}, name="norm2_b")
        (d_xa, d_o), (d_gate1,) = rwv(_resid, [sv["xa"], sv["o"]], [("mod", gate1)], [d_x1],
                                      row_grads=[F32, BF16], name="resid1_b")
        d_merged = _matmul(d_o, w["w_out"], tb=True, name="out_b")
        bg["w_out"] = _matmul(sv["merged"], d_o, ta=True, name="out_w").reshape(N_CHIPS, d_sh, d)
        d_parts, (d_bg,) = rwv(_merge, [sv["gates"]] + sv["projs"], [("full", b_gate_full[l])], [d_merged],
                               row_grads=[BF16] * (1 + N_BRANCH), name="merge_b", tm=ROW_TILE // 4)
        sg["b_gate"] = d_bg.reshape(N_BRANCH, d)
        d_gates, d_projs = d_parts[0], d_parts[1:]
        bg["w_gate"] = _matmul(sv["h"], d_gates, name="gate_w", **gate_wgt)
        d_h = _matmul(d_gates, w["w_gate"], name="gate_b", **gate_bwd)
        d_br = [_matmul(d_projs[k], w["w_branch"][k], tb=True, name="branch_b") for k in range(N_BRANCH)]
        bg["w_branch"] = _to_shards("w_branch", jnp.stack([
            _matmul(sv["branches"][k], d_projs[k], ta=True, name="branch_w") for k in range(N_BRANCH)]))
        (d_glu_pre,), _ = rwv(_glu_gate, [sv["glu_pre"]], [], [d_br[3]], row_grads=[BF16], name="s5_glu_b")
        d_yg = _matmul(d_glu_pre, w["s5_w_glu"], tb=True, name="s5_glu_proj_b")
        bg["s5_w_glu"] = _to_shards("s5_w_glu", _matmul(sv["yg"], d_glu_pre, ta=True, name="s5_glu_proj_w"))
        (d_ys, d_zd), (d_dskip,) = rwv(_s5_act, [sv["ys"], zd], [("full", sv["d_skip"])], [d_yg],
                                       row_grads=[BF16, F32], name="s5_act_b")
        sg["s5_d"] = d_dskip.reshape(db)
        cd, bd = sv["cd"], sv["bd"]
        st2 = [flat2(sv["st_f"]), flat2(sv["st_b"])]
        d_st = [_matmul(d_ys, cd[k], tb=True, name="s5_out_b%d" % k) for k in range(2)]
        d_cd = [_matmul(st2[k], d_ys, ta=True, name="s5_out_w%d" % k) for k in range(2)]
        d_bu_f, d_bu_b, d_lbar = _s5_scan_bwd(slab(d_st[0]), slab(d_st[1]), sv["st_f"], sv["st_b"], sv["lbar"],
                                              n_lat=n_lat)
        d_bu = [flat2(d_bu_f), flat2(d_bu_b)]
        d_zd = _matmul(d_bu[0], bd[0], tb=True, tiles=[d_zd], epi=add_epi, name="s5_in_b0")
        d_zd = _matmul(d_bu[1], bd[1], tb=True, tiles=[d_zd], epi=add_epi, name="s5_in_b1")
        d_bd = [_matmul(zd, d_bu[k], ta=True, name="s5_in_w%d" % k) for k in range(2)]
        blk_c = lambda a: _diag_blocks(a, g5).transpose(0, 2, 1)
        sg["s5_c_re"] = jnp.stack([blk_c(d_cd[k][:gp]) for k in range(2)])
        sg["s5_c_im"] = jnp.stack([-blk_c(d_cd[k][gp:]) for k in range(2)])
        blk_b = lambda a: _diag_blocks(a, g5).transpose(0, 2, 1).reshape(gp, S5_GW)
        d_bb_re = jnp.stack([blk_b(d_bd[k][:, :gp]) for k in range(2)])
        d_bb_im = jnp.stack([blk_b(d_bd[k][:, gp:]) for k in range(2)])
        d_lb = d_lbar.reshape(4, gp, 1)
        disc_ct = [jnp.stack([d_lb[0], d_lb[2]]), jnp.stack([d_lb[1], d_lb[3]]), d_bb_re, d_bb_im]
        d_are, d_aim, d_ls, d_bre, d_bim = _whole_vjp(_s5_discretise, sv["disc_in"], disc_ct, name="s5_disc_b")
        sg["s5_a_re"], sg["s5_a_im"] = d_are.reshape(2, g5, S5_STATE), d_aim.reshape(2, g5, S5_STATE)
        sg["s5_log_step"] = d_ls.reshape(2, g5, S5_STATE).sum(axis=-1)
        sg["s5_b_re"], sg["s5_b_im"] = d_bre.reshape(g5, S5_STATE, S5_GW), d_bim.reshape(g5, S5_STATE, S5_GW)
        d_o4 = to_heads(d_br[2], n_q)
        d_q4, d_k3, d_v3, d_sink_rows = _attn_bwd(sv["q4"], sv["k3"], sv["v3"], sv["sink_rows"], cos, sin_signed,
                                                  perm, d_o4, s_len=s_len)
        sg["attn_sink"] = d_sink_rows.reshape(n_q, BLOCK).sum(axis=-1)
        d_zq = from_heads(d_q4)
        d_zk, d_zv = from_heads(d_k3[:, None]), from_heads(d_v3[:, None])
        cw, cb, clg, clb = sv["conv_params"]
        (d_yc,), (sg["conv_ln_g"], sg["conv_ln_b"]) = rwv(
            _ln_silu, [sv["yc_conv"]], [("full", clg), ("full", clb)], [d_br[1]], row_grads=[F32], name="conv_ln_b")
        d_zb, sg["conv_w"], sg["conv_b"] = _conv_bwd(d_yc, sv["y_conv"], zb, cw, n_lat=n_lat)
        (d_za,), (sg["gmlp_ln_g"], sg["gmlp_ln_b"], sg["gmlp_ws"], d_bfull) = rwv(
            _gmlp, [za], sv["gmlp_params"], [d_br[0]], row_grads=[F32], name="gmlp_b")
        sg["gmlp_bs"] = d_bfull.reshape(CHUNK, GMLP_GROUPS, gw).sum(axis=-1).T
        d_z = jnp.concatenate([d_za, d_zb, d_zq, d_zk, d_zv, d_zd], axis=1).astype(BF16)
        d_h = _matmul(d_z, w["w_in"], tb=True, tiles=[d_h], epi=add_epi, name="in_b")
        bg["w_in"] = _to_shards("w_in", _matmul(sv["h"], d_z, ta=True, name="in_w"))
        (dxa,), (sg["norm1_g"], d_scale1, d_shift1) = rwv(
            _rms_mod, [sv["xa"]], [("full", sv["n1g"]), ("mod", scale1), ("mod", shift1)], [d_h],
            row_grads=[F32], adds={0: d_xa}, name="norm1_b")
        d_mods[l] = jnp.concatenate([d_shift1, d_scale1, d_gate1, d_shift2, d_scale2, d_gate2], axis=-1)[:, 0]
        big_grads[l], small_grads[l] = bg, sg
        saved[l] = None

    grad_x = dxa[:s_len][None]

    dm_loc = jnp.stack(d_mods).reshape(n_layers * 2, N_MOD * d)
    dm_pad = -(-dm_loc.shape[0] // SUBLANES) * SUBLANES
    dm_all = _allgather8(jnp.pad(dm_loc, ((0, dm_pad - dm_loc.shape[0]), (0, 0))), name="gather_dmod")
    dm_all = dm_all.reshape(N_DEV, dm_pad, N_MOD * d)[:, :n_layers * 2].reshape(N_DEV, n_layers, 2, N_MOD * d)
    dm16 = dm_all.transpose(1, 2, 0, 3).reshape(n_layers, 2 * N_DEV, N_MOD * d)
    dm16_sh = lax.dynamic_slice_in_dim(dm16, j_me * ncol, ncol, axis=2)
    col_sum = lambda a: (jnp.sum(a, axis=0, keepdims=True),)
    grads = {}
    grads["w_mod"] = jnp.stack([_matmul(cond16, dm16_sh[l], ta=True, tk=2 * N_DEV, name="mod_w")
                                for l in range(n_layers)])
    grads["b_mod"] = jnp.concatenate([
        _whole(col_sum, [dm16[l]], [((1, N_MOD * d), F32)], name="mod_bias_g")[0] for l in range(n_layers)], axis=0)
    d_cond = _matmul(dm16_sh[0], w_mod[0], tb=True, tm=2 * N_DEV, name="mod_b")
    for l in range(1, n_layers):
        d_cond = _matmul(dm16_sh[l], w_mod[l], tb=True, tm=2 * N_DEV, tiles=[d_cond], epi=add_epi, name="mod_b_acc")
    d_c16 = _whole_vjp(silu_fn, [c16], [d_cond], name="silu_c_b")[0]
    d_cctx_part = jnp.where(ac == 0, d_c16[N_DEV:].sum(axis=0), 0.0)

    for n in BIG:
        grads[n] = [None] * n_layers
    for l in range(n_layers):
        for n, g in zip(BIG, _reduce_scatter([big_grads[l][n] for n in BIG])):
            grads[n][l] = g
    for n in BIG:
        grads[n] = jnp.stack(grads[n])

    small_shapes = {n: small_grads[0][n].shape for n in SMALL}
    for n in SMALL:
        grads[n] = [None] * n_layers
    for l in range(n_layers):
        extra = [d_final_g.reshape(-1), d_cctx_part.reshape(-1)] if l == 0 else []
        flat = pad_rows(jnp.concatenate([small_grads[l][n].reshape(-1) for n in SMALL] + extra))
        red = _allgather8(flat, reduce=True, name="reduce_small").reshape(-1)
        off = 0
        for n in SMALL:
            sz = math.prod(small_shapes[n])
            grads[n][l] = red[off:off + sz].reshape(small_shapes[n])
            off += sz
        if l == 0:
            grads["final_g"] = red[off:off + d]
            grads["c_ctx"] = red[off + d:off + 2 * d]
    for n in SMALL:
        g_full = jnp.stack(grads[n])
        if n in SMALL_SHARDED:
            width = p[n].shape[-1]
            g_full = lax.dynamic_slice_in_dim(g_full, j_me * width, width, axis=g_full.ndim - 1)
        grads[n] = g_full.reshape(p[n].shape)

    delta, new_m, new_v = {}, {}, {}
    large = BIG + ("w_mod",)
    for n in large:
        delta[n], new_m[n], new_v[n] = _adamw_call(p[n], grads[n], mom[n], var[n], name="adamw")
    small = [n for n in WEIGHT_NAMES if n not in large]
    pack = lambda src: pad_rows(jnp.concatenate([src[n].reshape(-1) for n in small]))
    outs = _adamw_call(pack(p), pack(grads), pack(mom), pack(var), name="adamw_small")
    off = 0
    for n in small:
        sz = p[n].size
        delta[n], new_m[n], new_v[n] = (o.reshape(-1)[off:off + sz].reshape(p[n].shape) for o in outs)
        off += sz

    return (loss, grad_x, *[grads[n] for n in WEIGHT_NAMES], *[delta[n] for n in WEIGHT_NAMES],
            *[new_m[n] for n in WEIGHT_NAMES], *[new_v[n] for n in WEIGHT_NAMES])
```

```python
import functools
import math

import jax
import jax.numpy as jnp
from jax import lax
from jax.experimental import pallas as pl
from jax.experimental.pallas import tpu as pltpu

F32 = jnp.float32
BF16 = jnp.bfloat16
MESH = pl.DeviceIdType.MESH

V7X_VMEM_BYTES = 64 * 1024 * 1024
VMEM_LIMIT = V7X_VMEM_BYTES - 8 * 1024 * 1024
LANES = 128
SUBLANES = 8

N_BRANCH = 4
CHUNK = 128
GMLP_GROUPS = 4
CONV_W = 31
CONV_HALO = 16
HEAD_DIM = 64
N_KV_HEADS = 2
WINDOW = 128
BLOCK = 128
ROPE_BASE = 10000.0
GRID_W = 64
S5_GW = 16
S5_STATE = 64
N_MOD = 6
EPS = 1e-6
NEG_INF = -1e30
ADAM_LR = 0.001
ADAM_B1 = 0.9
ADAM_B2 = 0.999
ADAM_EPS = 1e-08
ADAM_WD = 0.01
ADAM_STEP = 10

ROW_TILE = 256


def _cparams(sem=None, **kw):
    if sem is not None:
        kw["dimension_semantics"] = sem
    return pltpu.CompilerParams(vmem_limit_bytes=VMEM_LIMIT, **kw)


def _pick(n, cands):
    for c in cands:
        if n % c == 0:
            return c
    return n


def _bdot(a, b, ca, cb):
    return lax.dot_general(a.astype(BF16), b.astype(BF16), (((ca,), (cb,)), ((), ())),
                           preferred_element_type=F32)


@jax.custom_vjp
def _mm_nn(a, b):
    return _bdot(a, b, 1, 0)


def _mm_nn_fwd(a, b):
    return _bdot(a, b, 1, 0), (a, b)


def _mm_nn_bwd(res, g):
    a, b = res
    return _bdot(g, b, 1, 1).astype(a.dtype), _bdot(a, g, 0, 0).astype(b.dtype)


_mm_nn.defvjp(_mm_nn_fwd, _mm_nn_bwd)


@jax.custom_vjp
def _mm_nt(a, b):
    return _bdot(a, b, 1, 1)


def _mm_nt_fwd(a, b):
    return _bdot(a, b, 1, 1), (a, b)


def _mm_nt_bwd(res, g):
    a, b = res
    return _bdot(g, b, 1, 0).astype(a.dtype), _bdot(g, a, 0, 0).astype(b.dtype)


_mm_nt.defvjp(_mm_nt_fwd, _mm_nt_bwd)


def _matmul(a, b, *, ta=False, tb=False, tm=None, tn=None, tk=None, name,
            out_dtypes=(F32,), epi=None, tiles=(), rowvecs=(), b_blocks=None, out_blocks=None):
    m, k = (a.shape[1], a.shape[0]) if ta else a.shape
    if b_blocks is None:
        k2, n = (b.shape[1], b.shape[0]) if tb else b.shape
        assert k == k2, (a.shape, b.shape, ta, tb)
    else:
        n = b_blocks[0]
    tm = tm or _pick(m, (1024, 512, 256, 128) if ta else (768, 512, 384, 256, 128))
    tn = tn or _pick(n, (2048, 1664, 1024, 512, 256, 128) if ta else (1024, 1664, 512, 256, 128))
    tk = tk or _pick(k, (1408, 768, 512, 384, 256, 128) if ta else (2048, 1664, 1024, 768, 512, 384, 256, 128))
    assert m % tm == 0 and n % tn == 0 and k % tk == 0, (m, n, k, tm, tn, tk)
    nk = k // tk
    n_t, n_v, n_o = len(tiles), len(rowvecs), len(out_dtypes)
    ca, cb = (0 if ta else 1), (1 if tb else 0)

    def body(*refs):
        a_ref, b_ref = refs[:2]
        t_refs = refs[2:2 + n_t]
        v_refs = refs[2 + n_t:2 + n_t + n_v]
        o_refs = refs[2 + n_t + n_v:2 + n_t + n_v + n_o]

        def finish(acc):
            outs = (acc,) if epi is None else epi(acc, *[t[...] for t in t_refs], *[v[...] for v in v_refs])
            for o_ref, o in zip(o_refs, outs):
                o_ref[...] = o.astype(o_ref.dtype)

        b_tile = b_ref[...]
        if b_tile.ndim == 3:
            b_tile = b_tile.reshape(b_tile.shape[0] * b_tile.shape[1], b_tile.shape[2])
        part = _bdot(a_ref[...], b_tile, ca, cb)
        if nk == 1:
            finish(part)
        else:
            acc_ref = refs[-1]
            kk = pl.program_id(2)

            @pl.when(kk == 0)
            def _():
                acc_ref[...] = part

            @pl.when(kk > 0)
            def _():
                acc_ref[...] += part

            @pl.when(kk == nk - 1)
            def _():
                finish(acc_ref[...])

    a_spec = pl.BlockSpec((tk, tm), lambda i, j, kk: (kk, i)) if ta else pl.BlockSpec((tm, tk), lambda i, j, kk: (i, kk))
    b_spec = pl.BlockSpec((tn, tk), lambda i, j, kk: (j, kk)) if tb else pl.BlockSpec((tk, tn), lambda i, j, kk: (kk, j))
    if b_blocks is not None:
        b_spec = pl.BlockSpec(b_blocks[1], b_blocks[2])
    in_specs = [a_spec, b_spec]
    in_specs += [pl.BlockSpec((tm, tn), lambda i, j, kk: (i, j)) for _ in tiles]
    in_specs += [pl.BlockSpec((1, tn), lambda i, j, kk: (0, j)) for _ in rowvecs]
    out_specs = [pl.BlockSpec((tm, tn), lambda i, j, kk: (i, j)) for _ in out_dtypes]
    out_shape = [jax.ShapeDtypeStruct((m, n), d) for d in out_dtypes]
    if out_blocks is not None:
        out_specs = [pl.BlockSpec(out_blocks[1], out_blocks[2])]
        out_shape = [jax.ShapeDtypeStruct(out_blocks[0], out_dtypes[0])]
    out = pl.pallas_call(
        body, name=name,
        grid=(m // tm, n // tn, nk),
        in_specs=in_specs,
        out_specs=out_specs,
        out_shape=out_shape,
        scratch_shapes=[pltpu.VMEM((tm, tn), F32)] if nk > 1 else [],
        compiler_params=_cparams(("parallel", "parallel", "arbitrary")),
    )(a, b, *tiles, *rowvecs)
    return out[0] if n_o == 1 else tuple(out)


def _param_spec(kind, p, n_lat):
    if kind == "mod":
        return pl.BlockSpec((None,) + p.shape[1:], lambda i: (i // n_lat,) + (0,) * (p.ndim - 1))
    return pl.BlockSpec(p.shape, lambda i: (0,) * p.ndim)


def _rowwise(fn, rows, params, outs, *, name, tm=ROW_TILE, s_rows=None):
    r = rows[0].shape[0]
    assert r % tm == 0, (r, tm)
    n_lat = r // tm + 1 if s_rows is None else s_rows // tm
    n_r, n_p = len(rows), len(params)

    def body(*refs):
        vals = [x[...] for x in refs[:n_r + n_p]]
        res = fn(*vals)
        for o_ref, o in zip(refs[n_r + n_p:], res):
            o_ref[...] = o.astype(o_ref.dtype)

    out = pl.pallas_call(
        body, name=name, grid=(r // tm,),
        in_specs=[pl.BlockSpec((tm, x.shape[1]), lambda i: (i, 0)) for x in rows]
        + [_param_spec(kind, p, n_lat) for kind, p in params],
        out_specs=[pl.BlockSpec((tm, w), lambda i: (i, 0)) for w, _ in outs],
        out_shape=[jax.ShapeDtypeStruct((r, w), d) for w, d in outs],
        compiler_params=_cparams(("parallel",)),
    )(*rows, *[p for _, p in params])
    return tuple(out)


def _rowwise_vjp(fn, rows, params, cts, *, name, row_grads, tm=ROW_TILE, s_rows=None, adds=None):
    r = rows[0].shape[0]
    assert r % tm == 0, (r, tm)
    n_lat = r // tm + 1 if s_rows is None else s_rows // tm
    adds = adds or {}
    n_r, n_p, n_c, n_a = len(rows), len(params), len(cts), len(adds)
    want = [i for i, d in enumerate(row_grads) if d is not None]
    add_at = {idx: k for k, idx in enumerate(sorted(adds))}

    def body(*refs):
        i = pl.program_id(0)
        prim = [x[...].astype(F32) for x in refs[:n_r + n_p]]
        ct = [x[...].astype(F32) for x in refs[n_r + n_p:n_r + n_p + n_c]]
        a_refs = refs[n_r + n_p + n_c:n_r + n_p + n_c + n_a]
        o_refs = refs[n_r + n_p + n_c + n_a:]
        _, vjp = jax.vjp(fn, *prim)
        grads = vjp(tuple(ct))
        for o_ref, idx in zip(o_refs[:len(want)], want):
            g = grads[idx] if idx not in add_at else grads[idx] + a_refs[add_at[idx]][...]
            o_ref[...] = g.astype(o_ref.dtype)
        for o_ref, (kind, _), g in zip(o_refs[len(want):], params, grads[n_r:]):
            first = (i == 0) | (i == n_lat) if kind == "mod" else (i == 0)

            @pl.when(first)
            def _(o_ref=o_ref, g=g):
                o_ref[...] = g

            @pl.when(jnp.logical_not(first))
            def _(o_ref=o_ref, g=g):
                o_ref[...] += g

    out = pl.pallas_call(
        body, name=name, grid=(r // tm,),
        in_specs=[pl.BlockSpec((tm, x.shape[1]), lambda i: (i, 0)) for x in rows]
        + [_param_spec(kind, p, n_lat) for kind, p in params]
        + [pl.BlockSpec((tm, c.shape[1]), lambda i: (i, 0)) for c in cts]
        + [pl.BlockSpec((tm, adds[idx].shape[1]), lambda i: (i, 0)) for idx in sorted(adds)],
        out_specs=[pl.BlockSpec((tm, rows[idx].shape[1]), lambda i: (i, 0)) for idx in want]
        + [_param_spec(kind, p, n_lat) for kind, p in params],
        out_shape=[jax.ShapeDtypeStruct(rows[idx].shape, row_grads[idx]) for idx in want]
        + [jax.ShapeDtypeStruct(p.shape, F32) for _, p in params],
        compiler_params=_cparams(("arbitrary",)),
    )(*rows, *[p for _, p in params], *cts, *[adds[idx] for idx in sorted(adds)])
    return tuple(out[:len(want)]), tuple(out[len(want):])


def _rms_mod(x, g, scale, shift):
    y = x * lax.rsqrt(jnp.mean(x * x, axis=-1, keepdims=True) + EPS)
    return ((y * g) * (1.0 + scale) + shift,)


def _resid(x, o, gate):
    return (x + gate * o,)


def _layer_norm(x, g, b):
    xc = x - jnp.mean(x, axis=-1, keepdims=True)
    var = jnp.mean(xc * xc, axis=-1, keepdims=True)
    return xc * lax.rsqrt(var + EPS) * g + b


def _gmlp(za, ln_g, ln_b, ws, b_full):
    db = za.shape[1] // 2
    gw = db // GMLP_GROUPS
    za = jax.nn.gelu(za)
    u, v = za[:, :db], za[:, db:]
    v = _layer_norm(v, ln_g, ln_b)
    chunks = []
    for n in range(za.shape[0] // CHUNK):
        vn = v[n * CHUNK:(n + 1) * CHUNK]
        cols = [_mm_nn(ws[g], vn[:, g * gw:(g + 1) * gw]) for g in range(GMLP_GROUPS)]
        chunks.append(jnp.concatenate(cols, axis=1) + b_full)
    mixed = chunks[0] if len(chunks) == 1 else jnp.concatenate(chunks, axis=0)
    return (u * mixed,)


def _glu_gate(zb):
    db = zb.shape[1] // 2
    return (zb[:, :db] * jax.nn.sigmoid(zb[:, db:]),)


def _ln_silu(yc, g, b):
    return (jax.nn.silu(_layer_norm(yc, g, b)),)


def _s5_act(ys, u, d_skip):
    return (jax.nn.gelu(ys + d_skip * u),)


def _merge(*args):
    g, ps, b = args[0], args[1:1 + N_BRANCH], args[1 + N_BRANCH]
    d = ps[0].shape[1]
    s = jax.nn.sigmoid(g + b)
    out = s[:, :d] * ps[0]
    for k in range(1, N_BRANCH):
        out = out + s[:, k * d:(k + 1) * d] * ps[k]
    return (out,)


def _loss_rows(x, tgt, g):
    y = x * lax.rsqrt(jnp.mean(x * x, axis=-1, keepdims=True) + EPS) * g
    e = y - tgt
    return (0.5 * jnp.mean(e * e, axis=-1, keepdims=True),)


def _adamw(w, g, m, v):
    m = ADAM_B1 * m + (1.0 - ADAM_B1) * g
    v = ADAM_B2 * v + (1.0 - ADAM_B2) * jnp.square(g)
    m_hat = m / (1.0 - ADAM_B1 ** ADAM_STEP)
    v_hat = v / (1.0 - ADAM_B2 ** ADAM_STEP)
    delta = -ADAM_LR * (m_hat / (jnp.sqrt(v_hat) + ADAM_EPS) + ADAM_WD * w)
    return delta, m, v


def _whole(fn, args, outs, *, name):
    n_a = len(args)

    def body(*refs):
        res = fn(*[x[...] for x in refs[:n_a]])
        for o_ref, o in zip(refs[n_a:], res):
            o_ref[...] = o.astype(o_ref.dtype)

    return tuple(pl.pallas_call(
        body, name=name,
        out_shape=[jax.ShapeDtypeStruct(s, d) for s, d in outs],
        compiler_params=_cparams(),
    )(*args))


def _whole_vjp(fn, args, cts, *, name):
    n_a, n_c = len(args), len(cts)

    def body(*refs):
        prim = [x[...] for x in refs[:n_a]]
        ct = [x[...] for x in refs[n_a:n_a + n_c]]
        _, vjp = jax.vjp(fn, *prim)
        for o_ref, g in zip(refs[n_a + n_c:], vjp(tuple(ct))):
            o_ref[...] = g

    return tuple(pl.pallas_call(
        body, name=name,
        out_shape=[jax.ShapeDtypeStruct(a.shape, F32) for a in args],
        compiler_params=_cparams(),
    )(*args, *cts))


def _conv_flags(i, n_lat, n_tiles):
    has_prev = jnp.logical_and(i != 0, i != n_lat)
    has_next = jnp.logical_and(i != n_lat - 1, i != n_tiles - 1)
    return has_prev, has_next


def _halo_specs(width, tm, n_rows):
    per = tm // CONV_HALO
    last = n_rows // CONV_HALO - 1
    prev = pl.BlockSpec((CONV_HALO, width), lambda i: (jnp.maximum(i * per - 1, 0), 0))
    cur = pl.BlockSpec((tm, width), lambda i: (i, 0))
    nxt = pl.BlockSpec((CONV_HALO, width), lambda i: (jnp.minimum((i + 1) * per, last), 0))
    return [prev, cur, nxt]


def _with_halo(prev, cur, nxt, has_prev, has_next):
    prev = jnp.where(has_prev, prev, 0.0)
    nxt = jnp.where(has_next, nxt, 0.0)
    return jnp.concatenate([prev, cur, nxt], axis=0)


def _conv_fwd(zb, conv_w, conv_b, ln_g, ln_b, *, n_lat, tm=ROW_TILE):
    r, db = zb.shape[0], zb.shape[1] // 2
    n_tiles = r // tm
    half = CONV_W // 2

    def body(zp_ref, zc_ref, zn_ref, w_ref, b_ref, g_ref, lb_ref, out_ref, y_ref, yc_ref):
        i = pl.program_id(0)
        has_prev, has_next = _conv_flags(i, n_lat, n_tiles)
        y = _glu_gate(zc_ref[...])[0]
        y_ext = _with_halo(_glu_gate(zp_ref[...])[0], y, _glu_gate(zn_ref[...])[0], has_prev, has_next)
        w = w_ref[...]
        acc = jnp.zeros((tm, db), F32) + b_ref[...]
        for k in range(CONV_W):
            s = CONV_HALO - half + k
            acc = acc + w[k:k + 1, :] * y_ext[s:s + tm, :]
        y_ref[...] = y
        yc_ref[...] = acc
        out_ref[...] = _ln_silu(acc, g_ref[...], lb_ref[...])[0].astype(out_ref.dtype)

    full = lambda p: pl.BlockSpec(p.shape, lambda i: (0,) * p.ndim)
    return pl.pallas_call(
        body, name="conv_fwd", grid=(n_tiles,),
        in_specs=_halo_specs(2 * db, tm, r) + [full(conv_w), full(conv_b), full(ln_g), full(ln_b)],
        out_specs=[pl.BlockSpec((tm, db), lambda i: (i, 0))] * 3,
        out_shape=[jax.ShapeDtypeStruct((r, db), BF16), jax.ShapeDtypeStruct((r, db), F32),
                   jax.ShapeDtypeStruct((r, db), F32)],
        compiler_params=_cparams(("parallel",)),
    )(zb, zb, zb, conv_w, conv_b, ln_g, ln_b)


def _conv_bwd(d_yc, y, zb, conv_w, *, n_lat, tm=ROW_TILE):
    r, db = y.shape
    n_tiles = r // tm
    half = CONV_W // 2

    def body(gp_ref, gc_ref, gn_ref, yp_ref, yc_ref, yn_ref, z_ref, w_ref, dz_ref, dw_ref, db_ref):
        i = pl.program_id(0)
        has_prev, has_next = _conv_flags(i, n_lat, n_tiles)
        g = gc_ref[...]
        g_ext = _with_halo(gp_ref[...], g, gn_ref[...], has_prev, has_next)
        y_ext = _with_halo(yp_ref[...], yc_ref[...], yn_ref[...], has_prev, has_next)
        w = w_ref[...]

        @pl.when(i == 0)
        def _():
            dw_ref[...] = jnp.zeros_like(dw_ref)
            db_ref[...] = jnp.zeros_like(db_ref)

        d_y = jnp.zeros((tm, db), F32)
        for k in range(CONV_W):
            s = CONV_HALO + half - k
            d_y = d_y + w[k:k + 1, :] * g_ext[s:s + tm, :]
            s = CONV_HALO - half + k
            dw_ref[pl.ds(k, 1), :] += jnp.sum(g * y_ext[s:s + tm, :], axis=0, keepdims=True)
        db_ref[...] += jnp.sum(g, axis=0, keepdims=True)
        _, vjp = jax.vjp(_glu_gate, z_ref[...])
        dz_ref[...] = vjp((d_y,))[0]

    return pl.pallas_call(
        body, name="conv_bwd", grid=(n_tiles,),
        in_specs=_halo_specs(db, tm, r) + _halo_specs(db, tm, r)
        + [pl.BlockSpec((tm, 2 * db), lambda i: (i, 0)), pl.BlockSpec(conv_w.shape, lambda i: (0, 0))],
        out_specs=[pl.BlockSpec((tm, 2 * db), lambda i: (i, 0)), pl.BlockSpec((CONV_W, db), lambda i: (0, 0)),
                   pl.BlockSpec((1, db), lambda i: (0, 0))],
        out_shape=[jax.ShapeDtypeStruct((r, 2 * db), F32), jax.ShapeDtypeStruct((CONV_W, db), F32),
                   jax.ShapeDtypeStruct((1, db), F32)],
        compiler_params=_cparams(("arbitrary",)),
    )(d_yc, d_yc, d_yc, y, y, y, zb, conv_w)


def _rope(x, cos, sin_signed, perm):
    return x * cos + jnp.dot(x, perm, precision=lax.Precision.HIGHEST, preferred_element_type=F32) * sin_signed


def _softmax3(s_loc, s_ctx, sink_col):
    m = sink_col
    if s_loc is not None:
        m = jnp.maximum(m, jnp.max(s_loc, axis=-1, keepdims=True))
    m = lax.stop_gradient(jnp.maximum(m, jnp.max(s_ctx, axis=-1, keepdims=True)))
    e_ctx = jnp.exp(s_ctx - m)
    den = jnp.sum(e_ctx, axis=-1, keepdims=True) + jnp.exp(sink_col - m)
    if s_loc is None:
        return None, e_ctx / den
    e_loc = jnp.exp(s_loc - m)
    den = den + jnp.sum(e_loc, axis=-1, keepdims=True)
    return e_loc / den, e_ctx / den


def _attn_latent(q4, kb, vb, kc, vc, sink_col, cq, sq, ck, sk, perm, allowed):
    qpk = q4.shape[0]
    scale = HEAD_DIM ** -0.5
    q = _rope(q4.reshape(qpk * BLOCK, HEAD_DIM), jnp.concatenate([cq] * qpk, 0), jnp.concatenate([sq] * qpk, 0), perm)
    k = _rope(kb, ck, sk, perm)
    s_loc = jnp.where(allowed, _mm_nt(q, k) * scale, NEG_INF)
    s_ctx = _mm_nt(q, kc) * scale
    p_loc, p_ctx = _softmax3(s_loc, s_ctx, sink_col)
    o = _mm_nn(p_loc, vb) + _mm_nn(p_ctx, vc)
    return o.reshape(qpk, BLOCK, HEAD_DIM)


def _attn_context(q4, kc, vc, sink_col):
    qpk = q4.shape[0]
    scale = HEAD_DIM ** -0.5
    s_ctx = _mm_nt(q4.reshape(qpk * BLOCK, HEAD_DIM), kc) * scale
    _, p_ctx = _softmax3(None, s_ctx, sink_col)
    return _mm_nn(p_ctx, vc).reshape(qpk, BLOCK, HEAD_DIM)


def _attn_specs(nkv, qpk, nq, n_ctx, s_len):
    blk = lambda off: (lambda i: (0, jnp.clip(i + off, 0, nq - 1), 0))
    tab = lambda off: (lambda i: (jnp.clip(i + off, 0, nq - 1), 0))
    q_spec = pl.BlockSpec((nkv, qpk, BLOCK, HEAD_DIM), lambda i: (0, 0, i, 0))
    band = [pl.BlockSpec((nkv, BLOCK, HEAD_DIM), blk(off)) for off in (-1, 0, 1)]
    ctx = pl.BlockSpec((nkv, n_ctx, HEAD_DIM), lambda i: (0, s_len // n_ctx, 0))
    sink = pl.BlockSpec((nkv, qpk * BLOCK, 1), lambda i: (0, 0, 0))
    tabs = [pl.BlockSpec((BLOCK, HEAD_DIM), tab(off)) for off in (-1, 0, 1)]
    perm = pl.BlockSpec((HEAD_DIM, HEAD_DIM), lambda i: (0, 0))
    return q_spec, band, ctx, sink, tabs, perm


def _attn_mask(i, qpk, s_len):
    qpos = i * BLOCK + lax.broadcasted_iota(jnp.int32, (BLOCK, 3 * BLOCK), 0)
    kpos = (i - 1) * BLOCK + lax.broadcasted_iota(jnp.int32, (BLOCK, 3 * BLOCK), 1)
    ok = (jnp.abs(qpos - kpos) <= WINDOW) & (kpos >= 0) & (kpos < s_len)
    return jnp.concatenate([ok] * qpk, axis=0)


def _attn_fwd(q, k, v, sink_rows, cos, sin_signed, perm, *, s_len):
    nkv, qpk, r, _ = q.shape
    nq, n_ctx = s_len // BLOCK, r - s_len
    q_spec, band, ctx, sink, tabs, perm_spec = _attn_specs(nkv, qpk, nq, n_ctx, s_len)

    def body(q_ref, kp, kc_, kn, vp, vc_, vn, kx, vx, sk_ref, cp, cc, cn, sp, sc, sn, perm_ref, o_ref):
        i = pl.program_id(0)

        @pl.when(i < nq)
        def _():
            allowed = _attn_mask(i, qpk, s_len)
            ck = jnp.concatenate([cp[...], cc[...], cn[...]], 0)
            sk = jnp.concatenate([sp[...], sc[...], sn[...]], 0)
            for g in range(nkv):
                kb = jnp.concatenate([kp[g], kc_[g], kn[g]], 0)
                vb = jnp.concatenate([vp[g], vc_[g], vn[g]], 0)
                o = _attn_latent(q_ref[g], kb, vb, kx[g], vx[g], sk_ref[g], cc[...], sc[...], ck, sk, perm_ref[...], allowed)
                o_ref[g] = o.astype(o_ref.dtype)

        @pl.when(i >= nq)
        def _():
            for g in range(nkv):
                o_ref[g] = _attn_context(q_ref[g], kx[g], vx[g], sk_ref[g]).astype(o_ref.dtype)

    return pl.pallas_call(
        body, name="attn_fwd", grid=(r // BLOCK,),
        in_specs=[q_spec] + band + band + [ctx, ctx, sink] + tabs + tabs + [perm_spec],
        out_specs=q_spec,
        out_shape=jax.ShapeDtypeStruct(q.shape, BF16),
        compiler_params=_cparams(("parallel",)),
    )(q, k, k, k, v, v, v, k, v, sink_rows, cos, cos, cos, sin_signed, sin_signed, sin_signed, perm)


def _attn_bwd(q, k, v, sink_rows, cos, sin_signed, perm, d_o, *, s_len):
    nkv, qpk, r, _ = q.shape
    nq, n_ctx = s_len // BLOCK, r - s_len
    n_steps = r // BLOCK
    q_spec, band, ctx, sink, tabs, perm_spec = _attn_specs(nkv, qpk, nq, n_ctx, s_len)

    def body(q_ref, kp, kc_, kn, vp, vc_, vn, kx, vx, sk_ref, cp, cc, cn, sp, sc, sn, perm_ref, do_ref,
             dq_ref, dk_hbm, dv_hbm, dsk_ref, dk_acc, dv_acc):
        i = pl.program_id(0)

        @pl.when(i == 0)
        def _():
            dk_acc[...] = jnp.zeros_like(dk_acc)
            dv_acc[...] = jnp.zeros_like(dv_acc)
            dsk_ref[...] = jnp.zeros_like(dsk_ref)

        ctx_rows = pl.ds(s_len, n_ctx)

        @pl.when(i < nq)
        def _():
            allowed = _attn_mask(i, qpk, s_len)
            ck = jnp.concatenate([cp[...], cc[...], cn[...]], 0)
            sk = jnp.concatenate([sp[...], sc[...], sn[...]], 0)
            for g in range(nkv):
                kb = jnp.concatenate([kp[g], kc_[g], kn[g]], 0)
                vb = jnp.concatenate([vp[g], vc_[g], vn[g]], 0)
                fn = lambda q4, kb_, vb_, kc, vc, s_col: _attn_latent(
                    q4, kb_, vb_, kc, vc, s_col, cc[...], sc[...], ck, sk, perm_ref[...], allowed)
                _, vjp = jax.vjp(fn, q_ref[g], kb, vb, kx[g], vx[g], sk_ref[g])
                dq4, dkb, dvb, dkc, dvc, dsk = vjp(do_ref[g].astype(F32))
                dq_ref[g] = dq4
                for seg, off in enumerate((-1, 0, 1)):
                    rows = pl.ds(pl.multiple_of(jnp.clip(i + off, 0, nq - 1) * BLOCK, BLOCK), BLOCK)
                    dk_acc[g, rows, :] += dkb[seg * BLOCK:(seg + 1) * BLOCK]
                    dv_acc[g, rows, :] += dvb[seg * BLOCK:(seg + 1) * BLOCK]
                dk_acc[g, ctx_rows, :] += dkc
                dv_acc[g, ctx_rows, :] += dvc
                dsk_ref[g] += dsk

        @pl.when(i >= nq)
        def _():
            for g in range(nkv):
                _, vjp = jax.vjp(_attn_context, q_ref[g], kx[g], vx[g], sk_ref[g])
                dq4, dkc, dvc, dsk = vjp(do_ref[g].astype(F32))
                dq_ref[g] = dq4
                dk_acc[g, ctx_rows, :] += dkc
                dv_acc[g, ctx_rows, :] += dvc
                dsk_ref[g] += dsk

        @pl.when(i == n_steps - 1)
        def _():
            pltpu.sync_copy(dk_acc, dk_hbm)
            pltpu.sync_copy(dv_acc, dv_hbm)

    any_spec = pl.BlockSpec(memory_space=pl.ANY)
    return pl.pallas_call(
        body, name="attn_bwd", grid=(n_steps,),
        in_specs=[q_spec] + band + band + [ctx, ctx, sink] + tabs + tabs + [perm_spec, q_spec],
        out_specs=[q_spec, any_spec, any_spec, sink],
        out_shape=[jax.ShapeDtypeStruct(q.shape, F32), jax.ShapeDtypeStruct(k.shape, F32),
                   jax.ShapeDtypeStruct(v.shape, F32), jax.ShapeDtypeStruct(sink_rows.shape, F32)],
        scratch_shapes=[pltpu.VMEM(k.shape, F32), pltpu.VMEM(v.shape, F32)],
        compiler_params=_cparams(("arbitrary",)),
    )(q, k, k, k, v, v, v, k, v, sink_rows, cos, cos, cos, sin_signed, sin_signed, sin_signed, perm, d_o)


def _scan_orders(n_lat, n_ctx):
    fwd = lambda i: jnp.where(i < n_ctx, n_lat + i, i - n_ctx)
    bwd = lambda i: jnp.where(i < n_ctx, n_lat + n_ctx - 1 - i, n_lat - 1 - (i - n_ctx))
    return fwd, bwd


def _s5_scan(bu_f, bu_b, lbar, *, n_lat, tb=ROW_TILE):
    r, ch2, _ = bu_f.shape
    ch = ch2 // 2
    n_tiles = r // tb
    of, ob = _scan_orders(n_lat, n_tiles - n_lat)

    def body(bf_ref, bb_ref, a_ref, sf_ref, sb_ref, st_ref):
        @pl.when(pl.program_id(0) == 0)
        def _():
            st_ref[...] = jnp.zeros_like(st_ref)

        afr, afi, abr, abi = a_ref[0], a_ref[1], a_ref[2], a_ref[3]

        def step(t, carry):
            sfr, sfi, sbr, sbi = carry
            x = bf_ref[t]
            nfr = afr * sfr - afi * sfi + x[:ch]
            nfi = afr * sfi + afi * sfr + x[ch:]
            sf_ref[t] = jnp.concatenate([nfr, nfi], axis=0)
            u = tb - 1 - t
            x = bb_ref[u]
            nbr = abr * sbr - abi * sbi + x[:ch]
            nbi = abr * sbi + abi * sbr + x[ch:]
            sb_ref[u] = jnp.concatenate([nbr, nbi], axis=0)
            return nfr, nfi, nbr, nbi

        out = lax.fori_loop(0, tb, step, (st_ref[0], st_ref[1], st_ref[2], st_ref[3]), unroll=4)
        for n in range(4):
            st_ref[n] = out[n]

    spec = lambda order: pl.BlockSpec((tb, ch2, LANES), lambda i: (order(i), 0, 0))
    return pl.pallas_call(
        body, name="s5_scan", grid=(n_tiles,),
        in_specs=[spec(of), spec(ob), pl.BlockSpec(lbar.shape, lambda i: (0, 0, 0))],
        out_specs=[spec(of), spec(ob)],
        out_shape=[jax.ShapeDtypeStruct(bu_f.shape, F32), jax.ShapeDtypeStruct(bu_b.shape, F32)],
        scratch_shapes=[pltpu.VMEM((4, ch, LANES), F32)],
        compiler_params=_cparams(("arbitrary",)),
    )(bu_f, bu_b, lbar)


def _s5_scan_bwd(ds_f, ds_b, s_f, s_b, lbar, *, n_lat, tb=ROW_TILE):
    r, ch2, _ = ds_f.shape
    ch = ch2 // 2
    n_tiles = r // tb
    of, ob = _scan_orders(n_lat, n_tiles - n_lat)
    rof = lambda i: of(n_tiles - 1 - i)
    rob = lambda i: ob(n_tiles - 1 - i)

    def body(gf_ref, gb_ref, sf_ref, sb_ref, a_ref, df_ref, db_ref, da_ref, st_ref):
        @pl.when(pl.program_id(0) == 0)
        def _():
            st_ref[...] = jnp.zeros_like(st_ref)
            da_ref[...] = jnp.zeros_like(da_ref)

        afr, afi, abr, abi = a_ref[0], a_ref[1], a_ref[2], a_ref[3]

        def one(g_re, g_im, acc_re, acc_im, a_re, a_im, s, ds):
            acc_re = acc_re + s[:ch] * g_re + s[ch:] * g_im
            acc_im = acc_im - s[ch:] * g_re + s[:ch] * g_im
            n_re = ds[:ch] + a_re * g_re + a_im * g_im
            n_im = ds[ch:] - a_im * g_re + a_re * g_im
            return n_re, n_im, acc_re, acc_im

        def step(t, carry):
            gfr, gfi, gbr, gbi, cfr, cfi, cbr, cbi = carry
            u = tb - 1 - t
            gfr, gfi, cfr, cfi = one(gfr, gfi, cfr, cfi, afr, afi, sf_ref[u], gf_ref[u])
            df_ref[u] = jnp.concatenate([gfr, gfi], axis=0)
            gbr, gbi, cbr, cbi = one(gbr, gbi, cbr, cbi, abr, abi, sb_ref[t], gb_ref[t])
            db_ref[t] = jnp.concatenate([gbr, gbi], axis=0)
            return gfr, gfi, gbr, gbi, cfr, cfi, cbr, cbi

        zero = jnp.zeros((ch, LANES), F32)
        out = lax.fori_loop(0, tb, step, (st_ref[0], st_ref[1], st_ref[2], st_ref[3], zero, zero, zero, zero), unroll=4)
        for n in range(4):
            st_ref[n] = out[n]
            da_ref[n] += out[4 + n]

    spec = lambda order: pl.BlockSpec((tb, ch2, LANES), lambda i: (order(i), 0, 0))
    return pl.pallas_call(
        body, name="s5_scan_bwd", grid=(n_tiles,),
        in_specs=[spec(rof), spec(rob), spec(rof), spec(rob), pl.BlockSpec(lbar.shape, lambda i: (0, 0, 0))],
        out_specs=[spec(rof), spec(rob), pl.BlockSpec(lbar.shape, lambda i: (0, 0, 0))],
        out_shape=[jax.ShapeDtypeStruct(ds_f.shape, F32), jax.ShapeDtypeStruct(ds_b.shape, F32),
                   jax.ShapeDtypeStruct(lbar.shape, F32)],
        scratch_shapes=[pltpu.VMEM((4, ch, LANES), F32)],
        compiler_params=_cparams(("arbitrary",)),
    )(ds_f, ds_b, s_f, s_b, lbar)


def _s5_discretise(a_re, a_im, log_step, b_re, b_im):
    dt = jnp.exp(log_step)
    mag = jnp.exp(a_re * dt)
    l_re, l_im = mag * jnp.cos(a_im * dt), mag * jnp.sin(a_im * dt)
    den = a_re * a_re + a_im * a_im
    q_re = ((l_re - 1.0) * a_re + l_im * a_im) / den
    q_im = (l_im * a_re - (l_re - 1.0) * a_im) / den
    bb_re = q_re * b_re[None] - q_im * b_im[None]
    bb_im = q_re * b_im[None] + q_im * b_re[None]
    return l_re, l_im, bb_re, bb_im


N_DEV = 8
N_CHIPS = 4


def _place():
    x, y, c = lax.axis_index("x"), lax.axis_index("y"), lax.axis_index("c")
    chips = [(1 - x, y), (x, 1 - y), (1 - x, 1 - y)]
    return x, y, c, chips


def _allgather8(v, *, reduce=False, name):
    m_per, n = v.shape

    def body(x_ref, out_ref, *scratch):
        if reduce:
            all_ref, send_sems, recv_sems, local_sem = scratch
        else:
            all_ref = out_ref
            send_sems, recv_sems, local_sem = scratch
        x, y, c, chips = _place()
        me, sibling = (x, y, c), (x, y, 1 - c)

        def rows(px, py, pc):
            return all_ref.at[pl.ds((4 * px + 2 * py + pc) * m_per, m_per), :]

        def copy(k, block, to, src=None):
            return pltpu.make_async_remote_copy(
                src_ref=rows(*block) if src is None else src, dst_ref=rows(*block),
                send_sem=send_sems.at[k], recv_sem=recv_sems.at[k], device_id=to, device_id_type=MESH)

        mine = pltpu.make_async_copy(x_ref, rows(*me), local_sem)
        mine.start()
        first = [copy(0, me, sibling, src=x_ref)]
        first += [copy(1 + j, me, (*chip, c), src=x_ref) for j, chip in enumerate(chips)]
        for cp in first:
            cp.start()
        passed = [copy(4 + j, (*chip, c), sibling) for j, chip in enumerate(chips)]
        for j, chip in enumerate(chips):
            copy(1 + j, (*chip, c), me).wait_recv()
            passed[j].start()
        copy(0, sibling, me).wait_recv()
        for j, chip in enumerate(chips):
            copy(4 + j, (*chip, 1 - c), me).wait_recv()
        for cp in first + passed:
            cp.wait_send()
        mine.wait()
        if reduce:
            acc = all_ref[pl.ds(0, m_per), :]
            for d in range(1, N_DEV):
                acc = acc + all_ref[pl.ds(d * m_per, m_per), :]
            out_ref[...] = acc

    sems = [pltpu.SemaphoreType.DMA((7,)), pltpu.SemaphoreType.DMA((7,)), pltpu.SemaphoreType.DMA]
    return pl.pallas_call(
        body, name=name,
        out_shape=jax.ShapeDtypeStruct((m_per if reduce else N_DEV * m_per, n), v.dtype),
        in_specs=[pl.BlockSpec(memory_space=pltpu.VMEM)],
        out_specs=pl.BlockSpec(memory_space=pltpu.VMEM),
        scratch_shapes=([pltpu.VMEM((N_DEV * m_per, n), v.dtype)] if reduce else []) + sems,
        compiler_params=_cparams(),
    )(v)


def _halves(ref, half):
    h = ref.shape[0] // 2
    return ref.at[pl.ds(half * h, h)]


def _gather_shards(stacks):
    n_w = len(stacks)

    def body(*refs):
        out_refs = refs[n_w:2 * n_w]
        send_sems, recv_sems = refs[2 * n_w:]
        x, y, c, chips = _place()
        me, sibling = (x, y, c), (x, y, 1 - c)
        chip_no = lambda px, py: 2 * px + py

        def copy(n, k, chip, half, to):
            blk = _halves(out_refs[n].at[chip_no(*chip)], half)
            return pltpu.make_async_remote_copy(
                src_ref=blk, dst_ref=blk, send_sem=send_sems.at[n, k], recv_sem=recv_sems.at[n, k],
                device_id=to, device_id_type=MESH)

        first = [copy(n, j, (x, y), c, (*chip, c)) for n in range(n_w) for j, chip in enumerate(chips)]
        for cp in first:
            cp.start()
        passed = []
        for n in range(n_w):
            for j, chip in enumerate(chips):
                copy(n, j, chip, c, me).wait_recv()
                passed.append(copy(n, 3 + j, chip, c, sibling))
                passed[-1].start()
        for n in range(n_w):
            for j, chip in enumerate(chips):
                copy(n, 3 + j, chip, 1 - c, me).wait_recv()
        for cp in first + passed:
            cp.wait_send()

    any_spec = pl.BlockSpec(memory_space=pl.ANY)
    return pl.pallas_call(
        body, name="gather_shards",
        out_shape=[jax.ShapeDtypeStruct(s.shape, s.dtype) for s in stacks],
        in_specs=[any_spec] * n_w, out_specs=[any_spec] * n_w,
        input_output_aliases={n: n for n in range(n_w)},
        scratch_shapes=[pltpu.SemaphoreType.DMA((n_w, 6)), pltpu.SemaphoreType.DMA((n_w, 6))],
        compiler_params=_cparams(),
    )(*stacks)


def _rs_swap_in(grads):
    n_w = len(grads)

    def body(*refs):
        x_refs, out_refs = refs[:n_w], refs[n_w:2 * n_w]
        send_sems, recv_sems = refs[2 * n_w:]
        x, y, c, _ = _place()
        cps = []
        for n in range(n_w):
            h = x_refs[n].shape[1] // 2
            cps.append(pltpu.make_async_remote_copy(
                src_ref=x_refs[n].at[:, pl.ds((1 - c) * h, h)], dst_ref=out_refs[n],
                send_sem=send_sems.at[n], recv_sem=recv_sems.at[n], device_id=(x, y, 1 - c), device_id_type=MESH))
        for cp in cps:
            cp.start()
        for cp in cps:
            cp.wait()

    any_spec = pl.BlockSpec(memory_space=pl.ANY)
    return pl.pallas_call(
        body, name="rs_swap_in",
        out_shape=[jax.ShapeDtypeStruct((g.shape[0], g.shape[1] // 2) + g.shape[2:], g.dtype) for g in grads],
        in_specs=[any_spec] * n_w, out_specs=[any_spec] * n_w,
        scratch_shapes=[pltpu.SemaphoreType.DMA((n_w,)), pltpu.SemaphoreType.DMA((n_w,))],
        compiler_params=_cparams(),
    )(*grads)


def _stage_peers():
    x, y, c = lax.axis_index("x"), lax.axis_index("y"), lax.axis_index("c")
    fx, fy = 1 - x, 1 - y
    first = (x + (1 - c) * (fx - x), y + c * (fy - y))
    second = (x + c * (fx - x), y + (1 - c) * (fy - y))
    return first, second, (fx, fy)


def _chip_exchange_1(parts):
    n_w = len(parts)

    def body(*refs):
        x_refs, out_refs = refs[:n_w], refs[n_w:2 * n_w]
        send_sems, recv_sems = refs[2 * n_w:]
        c = lax.axis_index("c")
        first, _, diag = _stage_peers()
        cps = [pltpu.make_async_remote_copy(
            src_ref=x_refs[n].at[2 * chip[0] + chip[1]], dst_ref=out_refs[n].at[k],
            send_sem=send_sems.at[n, k], recv_sem=recv_sems.at[n, k], device_id=(*first, c), device_id_type=MESH)
            for n in range(n_w) for k, chip in enumerate((first, diag))]
        for cp in cps:
            cp.start()
        for cp in cps:
            cp.wait()

    any_spec = pl.BlockSpec(memory_space=pl.ANY)
    return pl.pallas_call(
        body, name="chip_exchange_1",
        out_shape=[jax.ShapeDtypeStruct((2,) + v.shape[1:], v.dtype) for v in parts],
        in_specs=[any_spec] * n_w, out_specs=[any_spec] * n_w,
        scratch_shapes=[pltpu.SemaphoreType.DMA((n_w, 2)), pltpu.SemaphoreType.DMA((n_w, 2))],
        compiler_params=_cparams(),
    )(*parts)


def _chip_exchange_2(parts):
    n_w = len(parts)

    def body(*refs):
        x_refs, out_refs = refs[:n_w], refs[n_w:2 * n_w]
        send_sems, recv_sems = refs[2 * n_w:]
        c = lax.axis_index("c")
        _, second, _ = _stage_peers()
        cps = [pltpu.make_async_remote_copy(
            src_ref=x_refs[n].at[1], dst_ref=out_refs[n], send_sem=send_sems.at[n], recv_sem=recv_sems.at[n],
            device_id=(*second, c), device_id_type=MESH) for n in range(n_w)]
        for cp in cps:
            cp.start()
        for cp in cps:
            cp.wait()

    any_spec = pl.BlockSpec(memory_space=pl.ANY)
    return pl.pallas_call(
        body, name="chip_exchange_2",
        out_shape=[jax.ShapeDtypeStruct(v.shape[1:], v.dtype) for v in parts],
        in_specs=[any_spec] * n_w, out_specs=[any_spec] * n_w,
        scratch_shapes=[pltpu.SemaphoreType.DMA((n_w,)), pltpu.SemaphoreType.DMA((n_w,))],
        compiler_params=_cparams(),
    )(*parts)


def _rs_finish(reduced):
    n_w = len(reduced)

    def body(*refs):
        out_refs = refs[n_w:2 * n_w]
        send_sems, recv_sems = refs[2 * n_w:]
        x, y, c, _ = _place()

        def copy(n, half):
            blk = _halves(out_refs[n], half)
            return pltpu.make_async_remote_copy(
                src_ref=blk, dst_ref=blk, send_sem=send_sems.at[n], recv_sem=recv_sems.at[n],
                device_id=(x, y, 1 - c), device_id_type=MESH)

        for n in range(n_w):
            copy(n, c).start()
        for n in range(n_w):
            copy(n, c).wait_send()
            copy(n, 1 - c).wait_recv()

    any_spec = pl.BlockSpec(memory_space=pl.ANY)
    return pl.pallas_call(
        body, name="rs_finish",
        out_shape=[jax.ShapeDtypeStruct(v.shape, v.dtype) for v in reduced],
        in_specs=[any_spec] * n_w, out_specs=[any_spec] * n_w,
        input_output_aliases={n: n for n in range(n_w)},
        scratch_shapes=[pltpu.SemaphoreType.DMA((n_w,)), pltpu.SemaphoreType.DMA((n_w,))],
        compiler_params=_cparams(),
    )(*reduced)


def _as_rows(shape):
    return (math.prod(shape[:-1]), shape[-1])


def _row_tile(rows, cols):
    return _pick(rows, tuple(t for t in (2048, 1024, 512, 256, 128, 64, 32, 16, 8) if t * cols * 4 <= (1 << 21)))


def _pair_sum(g, t, c_idx):
    rows, cols = _as_rows(t.shape[1:])
    tr = _row_tile(rows, cols)

    def body(c_ref, g_ref, t_ref, o_ref):
        o_ref[...] = g_ref[...] + t_ref[...]

    out = pl.pallas_call(
        body, name="rs_pair_sum",
        grid_spec=pltpu.PrefetchScalarGridSpec(
            num_scalar_prefetch=1, grid=(N_CHIPS, rows // tr),
            in_specs=[pl.BlockSpec((None, None, tr, cols), lambda j, i, c: (j, c[0], i, 0)),
                      pl.BlockSpec((None, tr, cols), lambda j, i, c: (j, i, 0))],
            out_specs=pl.BlockSpec((None, tr, cols), lambda j, i, c: (j, i, 0))),
        out_shape=jax.ShapeDtypeStruct((N_CHIPS, rows, cols), F32),
        compiler_params=_cparams(("parallel", "parallel")),
    )(c_idx, g.reshape(N_CHIPS, 2, rows, cols), t.reshape(N_CHIPS, rows, cols))
    return out.reshape(t.shape)


def _chip_sum_1(q, u, idx):
    rows, cols = _as_rows(q.shape[1:])
    tr = _row_tile(rows, cols)

    def body(idx_ref, q_ref, u_ref, o_ref):
        o_ref[...] = q_ref[...] + u_ref[...]

    out = pl.pallas_call(
        body, name="rs_chip_sum_1",
        grid_spec=pltpu.PrefetchScalarGridSpec(
            num_scalar_prefetch=1, grid=(2, rows // tr),
            in_specs=[pl.BlockSpec((None, tr, cols), lambda s, i, idx: (idx[s], i, 0)),
                      pl.BlockSpec((None, tr, cols), lambda s, i, idx: (s, i, 0))],
            out_specs=pl.BlockSpec((None, tr, cols), lambda s, i, idx: (s, i, 0))),
        out_shape=jax.ShapeDtypeStruct((2, rows, cols), F32),
        compiler_params=_cparams(("parallel", "parallel")),
    )(idx, q.reshape(N_CHIPS, rows, cols), u.reshape(2, rows, cols))
    return out.reshape((2,) + q.shape[1:])


def _chip_sum_2(s, u, idx):
    rows, cols = _as_rows(u.shape)
    tr = _row_tile(rows, cols)
    nb = rows // tr

    def body(idx_ref, s_ref, u_ref, o_ref):
        o_ref[...] = s_ref[...] + u_ref[...]

    out = pl.pallas_call(
        body, name="rs_chip_sum_2",
        grid_spec=pltpu.PrefetchScalarGridSpec(
            num_scalar_prefetch=1, grid=(nb,),
            in_specs=[pl.BlockSpec((None, tr, cols), lambda i, idx: (0, i, 0)),
                      pl.BlockSpec((tr, cols), lambda i, idx: (i, 0))],
            out_specs=pl.BlockSpec((tr, cols), lambda i, idx: (idx[2] * nb + i, 0))),
        out_shape=jax.ShapeDtypeStruct((2 * rows, cols), F32),
        compiler_params=_cparams(("parallel",)),
    )(idx, s.reshape(2, rows, cols), u.reshape(rows, cols))
    return out.reshape((2 * u.shape[0],) + u.shape[1:])


def _reduce_scatter(grads):
    x, y, c = lax.axis_index("x"), lax.axis_index("y"), lax.axis_index("c")
    c_idx = jnp.reshape(c, (1,)).astype(jnp.int32)
    _, second, _ = _stage_peers()
    idx = jnp.stack([2 * x + y, 2 * second[0] + second[1], c]).astype(jnp.int32)
    theirs = _rs_swap_in(grads)
    pair = [_pair_sum(g, t, c_idx) for g, t in zip(grads, theirs)]
    got = _chip_exchange_1(pair)
    part = [_chip_sum_1(q, u, idx) for q, u in zip(pair, got)]
    got = _chip_exchange_2(part)
    reduced = [_chip_sum_2(s, u, idx) for s, u in zip(part, got)]
    return _rs_finish(reduced)


WEIGHT_NAMES = ('c_ctx', 'w_mod', 'b_mod', 'norm1_g', 'norm2_g', 'w_in', 'gmlp_ln_g', 'gmlp_ln_b', 'gmlp_ws',
                'gmlp_bs', 'conv_w', 'conv_b', 'conv_ln_g', 'conv_ln_b', 'attn_sink', 's5_a_re', 's5_a_im',
                's5_log_step', 's5_b_re', 's5_b_im', 's5_c_re', 's5_c_im', 's5_d', 's5_w_glu', 'w_branch',
                'w_gate', 'b_gate', 'w_out', 'w_ff1', 'w_ff2', 'final_g')
BIG = ('w_in', 's5_w_glu', 'w_branch', 'w_gate', 'w_out', 'w_ff1', 'w_ff2')
SMALL = ('norm1_g', 'norm2_g', 'gmlp_ln_g', 'gmlp_ln_b', 'gmlp_ws', 'gmlp_bs', 'conv_w', 'conv_b', 'conv_ln_g',
         'conv_ln_b', 'attn_sink', 's5_a_re', 's5_a_im', 's5_log_step', 's5_b_re', 's5_b_im', 's5_c_re',
         's5_c_im', 's5_d', 'b_gate')
SMALL_SHARDED = ('conv_w', 'b_gate')


def _from_shards(name, sh):
    j = sh.shape[0]
    if name in ('w_in', 's5_w_glu', 'w_ff1'):
        return sh.transpose(1, 0, 2).reshape(sh.shape[1], j * sh.shape[2])
    if name == 'w_branch':
        return sh.transpose(1, 2, 0, 3).reshape(sh.shape[1], sh.shape[2], j * sh.shape[3])
    if name == 'w_gate':
        return sh.transpose(0, 2, 1, 3).reshape(j * sh.shape[2], sh.shape[1] * sh.shape[3])
    return sh.reshape(j * sh.shape[1], sh.shape[2])


def _to_shards(name, dw, j=N_CHIPS):
    if name in ('w_in', 's5_w_glu', 'w_ff1'):
        return dw.reshape(dw.shape[0], j, dw.shape[1] // j).transpose(1, 0, 2)
    if name == 'w_branch':
        return dw.reshape(dw.shape[0], dw.shape[1], j, dw.shape[2] // j).transpose(2, 0, 1, 3)
    if name == 'w_gate':
        d = dw.shape[0]
        return dw.reshape(j, d // j, dw.shape[1] // d, d).transpose(0, 2, 1, 3)
    return dw.reshape(j, dw.shape[0] // j, dw.shape[1])


def _pack_rows(flat_parts, lead):
    flat = jnp.concatenate(flat_parts, axis=-1)
    assert flat.shape[-1] % LANES == 0, flat.shape
    return flat.reshape(lead + (flat.shape[-1] // LANES, LANES))


def _block_diag(blocks):
    g, a, b = blocks.shape
    eye = jnp.eye(g, dtype=blocks.dtype)
    return (blocks[:, :, None, :] * eye[:, None, :, None]).reshape(g * a, g * b)


def _diag_blocks(mat, g):
    a, b = mat.shape[0] // g, mat.shape[1] // g
    eye = jnp.eye(g, dtype=mat.dtype)
    return (mat.reshape(g, a, g, b) * eye[:, None, :, None]).sum(axis=2)


def _rope_tables(s_len):
    rows = s_len // GRID_W
    row = jnp.repeat(jnp.arange(rows), GRID_W).astype(F32)
    col = jnp.tile(jnp.arange(GRID_W), rows).astype(F32)
    d = HEAD_DIM // 2
    inv = ROPE_BASE ** (-jnp.arange(0, d, 2, dtype=F32) / d)
    ar, ac = row[:, None] * inv[None, :], col[:, None] * inv[None, :]
    cos = jnp.concatenate([jnp.cos(ar), jnp.cos(ar), jnp.cos(ac), jnp.cos(ac)], axis=1)
    sin_signed = jnp.concatenate([-jnp.sin(ar), jnp.sin(ar), -jnp.sin(ac), jnp.sin(ac)], axis=1)
    idx = jnp.arange(HEAD_DIM)
    partner = jnp.where(idx % d < d // 2, idx + d // 2, idx - d // 2)
    perm = (idx[:, None] == partner[None, :]).astype(F32)
    return cos, sin_signed, perm


def _loss_and_grad(xa, tgt, g, *, s_len, tm=ROW_TILE):
    d = xa.shape[1]

    def body(x_ref, t_ref, g_ref, l_ref, dx_ref, dg_ref):
        i = pl.program_id(0)
        (rows,), vjp = jax.vjp(_loss_rows, x_ref[...], t_ref[...], g_ref[...])
        dx, _, dg = vjp((jnp.ones_like(rows),))
        dx_ref[...] = dx
        part = jnp.zeros(l_ref.shape, F32) + jnp.sum(rows)

        @pl.when(i == 0)
        def _():
            l_ref[...] = part
            dg_ref[...] = dg

        @pl.when(i > 0)
        def _():
            l_ref[...] += part
            dg_ref[...] += dg

    return pl.pallas_call(
        body, name="loss_head", grid=(s_len // tm,),
        in_specs=[pl.BlockSpec((tm, d), lambda i: (i, 0)), pl.BlockSpec((tm, d), lambda i: (i, 0)),
                  pl.BlockSpec((1, d), lambda i: (0, 0))],
        out_specs=[pl.BlockSpec((SUBLANES, LANES), lambda i: (0, 0)), pl.BlockSpec((tm, d), lambda i: (i, 0)),
                   pl.BlockSpec((1, d), lambda i: (0, 0))],
        out_shape=[jax.ShapeDtypeStruct((SUBLANES, LANES), F32), jax.ShapeDtypeStruct((s_len, d), F32),
                   jax.ShapeDtypeStruct((1, d), F32)],
        compiler_params=_cparams(("arbitrary",)),
    )(xa, tgt, g)


def _adamw_call(w, g, m, v, *, name):
    shape = w.shape
    cols = shape[-1] if w.ndim > 1 else LANES
    two_d = lambda a: a.reshape(-1, cols)
    rows = two_d(w).shape[0]
    tm = _pick(rows, tuple(t for t in (1024, 512, 256, 128, 64, 32, 16, 8) if t * cols * 4 <= (1 << 20)))
    outs = _rowwise(_adamw, [two_d(w), two_d(g), two_d(m), two_d(v)], [], [(cols, F32)] * 3, name=name, tm=tm)
    return tuple(o.reshape(shape) for o in outs)


def kernel(x, c, ctx, c_ctx, w_mod, b_mod, norm1_g, norm2_g, w_in, gmlp_ln_g, gmlp_ln_b, gmlp_ws, gmlp_bs, conv_w, conv_b, conv_ln_g, conv_ln_b, attn_sink, s5_a_re, s5_a_im, s5_log_step, s5_b_re, s5_b_im, s5_c_re, s5_c_im, s5_d, s5_w_glu, w_branch, w_gate, b_gate, w_out, w_ff1, w_ff2, final_g, loss_target, m_c_ctx, m_w_mod, m_b_mod, m_norm1_g, m_norm2_g, m_w_in, m_gmlp_ln_g, m_gmlp_ln_b, m_gmlp_ws, m_gmlp_bs, m_conv_w, m_conv_b, m_conv_ln_g, m_conv_ln_b, m_attn_sink, m_s5_a_re, m_s5_a_im, m_s5_log_step, m_s5_b_re, m_s5_b_im, m_s5_c_re, m_s5_c_im, m_s5_d, m_s5_w_glu, m_w_branch, m_w_gate, m_b_gate, m_w_out, m_w_ff1, m_w_ff2, m_final_g, v_c_ctx, v_w_mod, v_b_mod, v_norm1_g, v_norm2_g, v_w_in, v_gmlp_ln_g, v_gmlp_ln_b, v_gmlp_ws, v_gmlp_bs, v_conv_w, v_conv_b, v_conv_ln_g, v_conv_ln_b, v_attn_sink, v_s5_a_re, v_s5_a_im, v_s5_log_step, v_s5_b_re, v_s5_b_im, v_s5_c_re, v_s5_c_im, v_s5_d, v_s5_w_glu, v_w_branch, v_w_gate, v_b_gate, v_w_out, v_w_ff1, v_w_ff2, v_final_g):
    given = dict(locals())
    p = {n: given[n] for n in WEIGHT_NAMES}
    mom = {n: given["m_" + n] for n in WEIGHT_NAMES}
    var = {n: given["v_" + n] for n in WEIGHT_NAMES}

    s_len, d = x.shape[1], x.shape[2]
    n_ctx = ctx.shape[1]
    r = s_len + n_ctx
    n_layers = w_mod.shape[0]
    db = d // N_BRANCH
    n_q = db // HEAD_DIM
    qpk = n_q // N_KV_HEADS
    g5 = db // S5_GW
    gp = g5 * S5_STATE
    ch = gp // LANES
    gw = db // GMLP_GROUPS
    n_lat = s_len // ROW_TILE
    assert s_len % ROW_TILE == 0 and n_ctx % ROW_TILE == 0 and s_len % n_ctx == 0 and gp % LANES == 0
    ax, ay, ac = lax.axis_index("x"), lax.axis_index("y"), lax.axis_index("c")
    j_me = 2 * ax + ay
    d_me = 4 * ax + 2 * ay + ac
    rw = functools.partial(_rowwise, s_rows=s_len)
    rwv = functools.partial(_rowwise_vjp, s_rows=s_len)
    add_epi = lambda acc, t: (acc + t,)

    c_all = _allgather8(jnp.broadcast_to(c, (SUBLANES, d)), name="gather_c")[::SUBLANES]
    c16 = jnp.concatenate([c_all, jnp.broadcast_to(c_ctx[None], (N_DEV, d))], axis=0)
    silu_fn = lambda a: (jax.nn.silu(a),)
    cond16 = _whole(silu_fn, [c16], [(c16.shape, F32)], name="silu_c")[0]
    ncol = w_mod.shape[2]
    b_mod_sh = lax.dynamic_slice_in_dim(b_mod, j_me * ncol, ncol, axis=1)
    mod_part = jnp.concatenate([
        _matmul(cond16, w_mod[l], tm=2 * N_DEV, rowvecs=[b_mod_sh[l][None]], epi=add_epi, name="mod_proj")
        for l in range(n_layers)], axis=0)
    mod_all = _allgather8(mod_part, name="gather_mod").reshape(N_CHIPS, 2, n_layers, 2 * N_DEV, ncol)[:, 0]
    mod_all = mod_all.transpose(1, 2, 0, 3).reshape(n_layers, 2 * N_DEV, N_CHIPS * ncol)

    def mods_of(l):
        two = jnp.stack([lax.dynamic_index_in_dim(mod_all[l], d_me, 0, keepdims=False), mod_all[l, N_DEV]])
        return [two[:, None, k * d:(k + 1) * d] for k in range(N_MOD)]

    d_sh, f_all = d // N_CHIPS, w_ff1.shape[2] * N_CHIPS
    f_sh = f_all // N_CHIPS

    def gather_layer(l):
        stacks = []
        for n in BIG:
            sh = p[n][l].astype(BF16)
            at = [j_me.astype(jnp.int32)] + [jnp.zeros((), jnp.int32)] * sh.ndim
            stacks.append(lax.dynamic_update_slice(jnp.zeros((N_CHIPS,) + sh.shape, BF16), sh[None], at))
        st = dict(zip(BIG, _gather_shards(stacks)))
        w = {n: _from_shards(n, st[n]) for n in ("w_in", "s5_w_glu", "w_branch")}
        w["w_out"] = st["w_out"].reshape(d, d)
        w["w_ff2"] = st["w_ff2"].reshape(f_all, d)
        w["w_gate"], w["w_ff1"] = st["w_gate"], st["w_ff1"]
        return w

    g_tn = min(1024, d)
    gate_fwd = dict(tk=d, tn=g_tn, b_blocks=(N_BRANCH * d, (N_CHIPS, None, d_sh, g_tn),
                                             lambda i, j, kk: (0, j // (d // g_tn), 0, j % (d // g_tn))))
    gate_bwd = dict(tb=True, tn=2 * d_sh, tk=d, b_blocks=(d, (2, None, d_sh, d), lambda i, j, kk: (j, kk, 0, 0)))
    gw_tn = min(2048, d)
    gate_wgt = dict(ta=True, tm=d_sh, tn=gw_tn, out_blocks=(
        (N_CHIPS, N_BRANCH, d_sh, d), (None, None, d_sh, gw_tn),
        lambda i, j, kk: (i, j // (d // gw_tn), 0, j % (d // gw_tn))))
    f_tn = min(1024, f_sh)
    ff1_fwd = dict(tn=f_tn, tk=d, b_blocks=(f_all, (None, d, f_tn), lambda i, j, kk: (j // (f_sh // f_tn), 0, j % (f_sh // f_tn))))
    f_tk = min(2048, f_sh)
    fb_tn = min(1024, d)
    ff1_bwd = dict(tb=True, tn=fb_tn, tk=f_tk, b_blocks=(
        d, (None, fb_tn, f_tk), lambda i, j, kk: (kk // (f_sh // f_tk), j, kk % (f_sh // f_tk))))
    fw_tn, fw_tm = min(2048, f_sh), min(1024, d)
    ff1_wgt = dict(ta=True, tm=fw_tm, tn=fw_tn, out_blocks=(
        (N_CHIPS, d, f_sh), (None, fw_tm, fw_tn), lambda i, j, kk: (j // (f_sh // fw_tn), i, j % (f_sh // fw_tn))))

    def pad_rows(flat):
        n = -(-flat.shape[0] // (SUBLANES * LANES)) * SUBLANES * LANES
        return jnp.pad(flat, (0, n - flat.shape[0])).reshape(n // LANES, LANES)

    sh_flat = pad_rows(jnp.concatenate([conv_w.reshape(-1), b_gate.reshape(-1)]))
    sh_all = _allgather8(sh_flat, name="gather_small_w").reshape(N_CHIPS, 2, -1)[:, 0]
    conv_w_full = sh_all[:, :conv_w.size].reshape((N_CHIPS,) + conv_w.shape).transpose(1, 2, 0, 3)
    conv_w_full = conv_w_full.reshape(n_layers, CONV_W, db)
    b_gate_full = sh_all[:, conv_w.size:conv_w.size + b_gate.size].reshape((N_CHIPS,) + b_gate.shape)
    b_gate_full = b_gate_full.transpose(1, 2, 0, 3).reshape(n_layers, 1, N_BRANCH * d)

    cos, sin_signed, perm = _rope_tables(s_len)
    xa = jnp.concatenate([x[0], ctx[0]], axis=0)
    col = lambda a: a[None] if a.ndim == 1 else a
    to_heads = lambda a, nh: a.reshape(r, N_KV_HEADS, nh // N_KV_HEADS, HEAD_DIM).transpose(1, 2, 0, 3)
    from_heads = lambda a: a.transpose(2, 0, 1, 3).reshape(r, -1)
    sink_rows_of = lambda l: jnp.repeat(attn_sink[l].reshape(N_KV_HEADS, qpk, 1), BLOCK, axis=1).reshape(
        N_KV_HEADS, qpk * BLOCK, 1)
    slab = lambda a: a.reshape(r, 2 * ch, LANES)
    flat2 = lambda a: a.reshape(r, 2 * gp)

    saved = []
    for l in range(n_layers):
        w = gather_layer(l)
        shift1, scale1, gate1, shift2, scale2, gate2 = mods_of(l)
        n1g, n2g = col(norm1_g[l]), col(norm2_g[l])
        h = rw(_rms_mod, [xa], [("full", n1g), ("mod", scale1), ("mod", shift1)], [(d, BF16)], name="norm1")[0]
        z = _matmul(h, w["w_in"], name="in_proj")
        o1, o2, o3, o4, o5 = 2 * db, 4 * db, 4 * db + n_q * HEAD_DIM, 4 * db + (n_q + N_KV_HEADS) * HEAD_DIM, \
            4 * db + (n_q + 2 * N_KV_HEADS) * HEAD_DIM
        za, zb, zq, zk, zv, zd = z[:, :o1], z[:, o1:o2], z[:, o2:o3], z[:, o3:o4], z[:, o4:o5], z[:, o5:]
        b_full = jnp.repeat(gmlp_bs[l].T, gw, axis=1)
        gmlp_params = [("full", col(gmlp_ln_g[l])), ("full", col(gmlp_ln_b[l])), ("full", gmlp_ws[l]), ("full", b_full)]
        br_a = rw(_gmlp, [za], gmlp_params, [(db, BF16)], name="gmlp")[0]
        conv_params = (conv_w_full[l], col(conv_b[l]), col(conv_ln_g[l]), col(conv_ln_b[l]))
        br_b, y_conv, yc_conv = _conv_fwd(zb, *conv_params, n_lat=n_lat)
        q4, k3, v3 = to_heads(zq, n_q), to_heads(zk, N_KV_HEADS)[:, 0], to_heads(zv, N_KV_HEADS)[:, 0]
        sink_rows = sink_rows_of(l)
        br_c = from_heads(_attn_fwd(q4, k3, v3, sink_rows, cos, sin_signed, perm, s_len=s_len))
        disc_in = [s5_a_re[l].reshape(2, gp, 1), s5_a_im[l].reshape(2, gp, 1),
                   jnp.broadcast_to(s5_log_step[l][:, :, None], (2, g5, S5_STATE)).reshape(2, gp, 1),
                   s5_b_re[l].reshape(gp, S5_GW), s5_b_im[l].reshape(gp, S5_GW)]
        l_re, l_im, bb_re, bb_im = _whole(
            _s5_discretise, disc_in, [((2, gp, 1), F32)] * 2 + [((2, gp, S5_GW), F32)] * 2, name="s5_disc")
        lbar = jnp.stack([l_re[0], l_im[0], l_re[1], l_im[1]]).reshape(4, ch, LANES)
        bd_of = lambda a: _block_diag(a.reshape(g5, S5_STATE, S5_GW).transpose(0, 2, 1))
        cd_of = lambda a: _block_diag(a.transpose(0, 2, 1))
        bd = [jnp.concatenate([bd_of(bb_re[k]), bd_of(bb_im[k])], axis=1).astype(BF16) for k in range(2)]
        cd = [jnp.concatenate([cd_of(s5_c_re[l, k]), -cd_of(s5_c_im[l, k])], axis=0).astype(BF16) for k in range(2)]
        bu_f = _matmul(zd, bd[0], name="s5_in_f")
        bu_b = _matmul(zd, bd[1], name="s5_in_b")
        st_f, st_b = _s5_scan(slab(bu_f), slab(bu_b), lbar, n_lat=n_lat)
        ys = _matmul(flat2(st_f), cd[0], name="s5_out_f")
        ys = _matmul(flat2(st_b), cd[1], tiles=[ys], epi=add_epi, name="s5_out_b")
        d_skip = col(s5_d[l])
        yg = rw(_s5_act, [ys, zd], [("full", d_skip)], [(db, BF16)], name="s5_act")[0]
        glu_pre = _matmul(yg, w["s5_w_glu"], name="s5_glu_proj")
        br_d = rw(_glu_gate, [glu_pre], [], [(db, BF16)], name="s5_glu")[0]
        branches = (br_a, br_b, br_c, br_d)
        gates = _matmul(h, w["w_gate"], name="gate_proj", **gate_fwd)
        projs = [_matmul(branches[k], w["w_branch"][k], name="branch_proj") for k in range(N_BRANCH)]
        merged = rw(_merge, [gates] + projs, [("full", b_gate_full[l])], [(d, BF16)], name="merge", tm=ROW_TILE // 2)[0]
        o = _matmul(merged, w["w_out"], name="out_proj")
        x1 = rw(_resid, [xa, o], [("mod", gate1)], [(d, F32)], name="resid1")[0]
        h2 = rw(_rms_mod, [x1], [("full", n2g), ("mod", scale2), ("mod", shift2)], [(d, BF16)], name="norm2")[0]
        f1, act = _matmul(h2, w["w_ff1"], out_dtypes=(F32, BF16), name="ff1",
                          epi=lambda acc: (acc, jnp.square(jnp.maximum(acc, 0.0))), **ff1_fwd)
        o_ff = _matmul(act, w["w_ff2"], name="ff2")
        x2 = rw(_resid, [x1, o_ff], [("mod", gate2)], [(d, F32)], name="resid2")[0]
        saved.append(dict(
            w=w, mods=(shift1, scale1, gate1, shift2, scale2, gate2), n1g=n1g, n2g=n2g, xa=xa, h=h, z=z,
            gmlp_params=gmlp_params, conv_params=conv_params, y_conv=y_conv, yc_conv=yc_conv, q4=q4, k3=k3, v3=v3,
            sink_rows=sink_rows, disc_in=disc_in, lbar=lbar, bd=bd, cd=cd, st_f=st_f, st_b=st_b, ys=ys,
            d_skip=d_skip, yg=yg, glu_pre=glu_pre, branches=branches, gates=gates, projs=projs, merged=merged,
            o=o, x1=x1, h2=h2, f1=f1, act=act, o_ff=o_ff))
        xa = x2

    loss_sum, dx_lat, d_final_g = _loss_and_grad(xa, loss_target[0], col(final_g), s_len=s_len)
    loss = lax.psum(loss_sum[0, 0], ("x", "y", "c"))
    dxa = jnp.concatenate([dx_lat, jnp.zeros((n_ctx, d), F32)], axis=0)

    big_grads = [None] * n_layers
    small_grads = [None] * n_layers
    d_mods = [None] * n_layers
    o1, o2, o3, o4, o5 = 2 * db, 4 * db, 4 * db + n_q * HEAD_DIM, 4 * db + (n_q + N_KV_HEADS) * HEAD_DIM, \
        4 * db + (n_q + 2 * N_KV_HEADS) * HEAD_DIM
    for l in reversed(range(n_layers)):
        sv = saved[l]
        w = sv["w"]
        shift1, scale1, gate1, shift2, scale2, gate2 = sv["mods"]
        z = sv["z"]
        za, zb, zd = z[:, :o1], z[:, o1:o2], z[:, o5:]
        bg, sg = {}, {}
        (d_x1, d_off), (d_gate2,) = rwv(_resid, [sv["x1"], sv["o_ff"]], [("mod", gate2)], [dxa],
                                        row_grads=[F32, BF16], name="resid2_b")
        d_f1 = _matmul(d_off, w["w_ff2"], tb=True, tiles=[sv["f1"]], out_dtypes=(BF16,), name="ff2_b",
                       epi=lambda acc, f: (acc * (2.0 * jnp.maximum(f, 0.0)),))
        bg["w_ff2"] = _matmul(sv["act"], d_off, ta=True, name="ff2_w").reshape(N_CHIPS, f_sh, d)
        d_h2 = _matmul(d_f1, w["w_ff1"], name="ff1_b", **ff1_bwd)
        bg["w_ff1"] = _matmul(sv["h2"], d_f1, name="ff1_w", **ff1_wgt)
        (d_x1,), (sg["norm2_g"], d_scale2, d_shift2) = rwv(
            _rms_mod, [sv["x1"]], [("full", sv["n2g"]), ("mod", scale2), ("mod", shift2)], [d_h2],
            row_grads=[F32], adds={0: d_x1}, name="norm2_b")
        (d_xa, d_o), (d_gate1,) = rwv(_resid, [sv["xa"], sv["o"]], [("mod", gate1)], [d_x1],
                                      row_grads=[F32, BF16], name="resid1_b")
        d_merged = _matmul(d_o, w["w_out"], tb=True, name="out_b")
        bg["w_out"] = _matmul(sv["merged"], d_o, ta=True, name="out_w").reshape(N_CHIPS, d_sh, d)
        d_parts, (d_bg,) = rwv(_merge, [sv["gates"]] + sv["projs"], [("full", b_gate_full[l])], [d_merged],
                               row_grads=[BF16] * (1 + N_BRANCH), name="merge_b", tm=ROW_TILE // 4)
        sg["b_gate"] = d_bg.reshape(N_BRANCH, d)
        d_gates, d_projs = d_parts[0], d_parts[1:]
        bg["w_gate"] = _matmul(sv["h"], d_gates, name="gate_w", **gate_wgt)
        d_h = _matmul(d_gates, w["w_gate"], name="gate_b", **gate_bwd)
        d_br = [_matmul(d_projs[k], w["w_branch"][k], tb=True, name="branch_b") for k in range(N_BRANCH)]
        bg["w_branch"] = _to_shards("w_branch", jnp.stack([
            _matmul(sv["branches"][k], d_projs[k], ta=True, name="branch_w") for k in range(N_BRANCH)]))
        (d_glu_pre,), _ = rwv(_glu_gate, [sv["glu_pre"]], [], [d_br[3]], row_grads=[BF16], name="s5_glu_b")
        d_yg = _matmul(d_glu_pre, w["s5_w_glu"], tb=True, name="s5_glu_proj_b")
        bg["s5_w_glu"] = _to_shards("s5_w_glu", _matmul(sv["yg"], d_glu_pre, ta=True, name="s5_glu_proj_w"))
        (d_ys, d_zd), (d_dskip,) = rwv(_s5_act, [sv["ys"], zd], [("full", sv["d_skip"])], [d_yg],
                                       row_grads=[BF16, F32], name="s5_act_b")
        sg["s5_d"] = d_dskip.reshape(db)
        cd, bd = sv["cd"], sv["bd"]
        st2 = [flat2(sv["st_f"]), flat2(sv["st_b"])]
        d_st = [_matmul(d_ys, cd[k], tb=True, name="s5_out_b%d" % k) for k in range(2)]
        d_cd = [_matmul(st2[k], d_ys, ta=True, name="s5_out_w%d" % k) for k in range(2)]
        d_bu_f, d_bu_b, d_lbar = _s5_scan_bwd(slab(d_st[0]), slab(d_st[1]), sv["st_f"], sv["st_b"], sv["lbar"],
                                              n_lat=n_lat)
        d_bu = [flat2(d_bu_f), flat2(d_bu_b)]
        d_zd = _matmul(d_bu[0], bd[0], tb=True, tiles=[d_zd], epi=add_epi, name="s5_in_b0")
        d_zd = _matmul(d_bu[1], bd[1], tb=True, tiles=[d_zd], epi=add_epi, name="s5_in_b1")
        d_bd = [_matmul(zd, d_bu[k], ta=True, name="s5_in_w%d" % k) for k in range(2)]
        blk_c = lambda a: _diag_blocks(a, g5).transpose(0, 2, 1)
        sg["s5_c_re"] = jnp.stack([blk_c(d_cd[k][:gp]) for k in range(2)])
        sg["s5_c_im"] = jnp.stack([-blk_c(d_cd[k][gp:]) for k in range(2)])
        blk_b = lambda a: _diag_blocks(a, g5).transpose(0, 2, 1).reshape(gp, S5_GW)
        d_bb_re = jnp.stack([blk_b(d_bd[k][:, :gp]) for k in range(2)])
        d_bb_im = jnp.stack([blk_b(d_bd[k][:, gp:]) for k in range(2)])
        d_lb = d_lbar.reshape(4, gp, 1)
        disc_ct = [jnp.stack([d_lb[0], d_lb[2]]), jnp.stack([d_lb[1], d_lb[3]]), d_bb_re, d_bb_im]
        d_are, d_aim, d_ls, d_bre, d_bim = _whole_vjp(_s5_discretise, sv["disc_in"], disc_ct, name="s5_disc_b")
        sg["s5_a_re"], sg["s5_a_im"] = d_are.reshape(2, g5, S5_STATE), d_aim.reshape(2, g5, S5_STATE)
        sg["s5_log_step"] = d_ls.reshape(2, g5, S5_STATE).sum(axis=-1)
        sg["s5_b_re"], sg["s5_b_im"] = d_bre.reshape(g5, S5_STATE, S5_GW), d_bim.reshape(g5, S5_STATE, S5_GW)
        d_o4 = to_heads(d_br[2], n_q)
        d_q4, d_k3, d_v3, d_sink_rows = _attn_bwd(sv["q4"], sv["k3"], sv["v3"], sv["sink_rows"], cos, sin_signed,
                                                  perm, d_o4, s_len=s_len)
        sg["attn_sink"] = d_sink_rows.reshape(n_q, BLOCK).sum(axis=-1)
        d_zq = from_heads(d_q4)
        d_zk, d_zv = from_heads(d_k3[:, None]), from_heads(d_v3[:, None])
        cw, cb, clg, clb = sv["conv_params"]
        (d_yc,), (sg["conv_ln_g"], sg["conv_ln_b"]) = rwv(
            _ln_silu, [sv["yc_conv"]], [("full", clg), ("full", clb)], [d_br[1]], row_grads=[F32], name="conv_ln_b")
        d_zb, sg["conv_w"], sg["conv_b"] = _conv_bwd(d_yc, sv["y_conv"], zb, cw, n_lat=n_lat)
        (d_za,), (sg["gmlp_ln_g"], sg["gmlp_ln_b"], sg["gmlp_ws"], d_bfull) = rwv(
            _gmlp, [za], sv["gmlp_params"], [d_br[0]], row_grads=[F32], name="gmlp_b")
        sg["gmlp_bs"] = d_bfull.reshape(CHUNK, GMLP_GROUPS, gw).sum(axis=-1).T
        d_z = jnp.concatenate([d_za, d_zb, d_zq, d_zk, d_zv, d_zd], axis=1).astype(BF16)
        d_h = _matmul(d_z, w["w_in"], tb=True, tiles=[d_h], epi=add_epi, name="in_b")
        bg["w_in"] = _to_shards("w_in", _matmul(sv["h"], d_z, ta=True, name="in_w"))
        (dxa,), (sg["norm1_g"], d_scale1, d_shift1) = rwv(
            _rms_mod, [sv["xa"]], [("full", sv["n1g"]), ("mod", scale1), ("mod", shift1)], [d_h],
            row_grads=[F32], adds={0: d_xa}, name="norm1_b")
        d_mods[l] = jnp.concatenate([d_shift1, d_scale1, d_gate1, d_shift2, d_scale2, d_gate2], axis=-1)[:, 0]
        big_grads[l], small_grads[l] = bg, sg
        saved[l] = None

    grad_x = dxa[:s_len][None]

    dm_loc = jnp.stack(d_mods).reshape(n_layers * 2, N_MOD * d)
    dm_pad = -(-dm_loc.shape[0] // SUBLANES) * SUBLANES
    dm_all = _allgather8(jnp.pad(dm_loc, ((0, dm_pad - dm_loc.shape[0]), (0, 0))), name="gather_dmod")
    dm_all = dm_all.reshape(N_DEV, dm_pad, N_MOD * d)[:, :n_layers * 2].reshape(N_DEV, n_layers, 2, N_MOD * d)
    dm16 = dm_all.transpose(1, 2, 0, 3).reshape(n_layers, 2 * N_DEV, N_MOD * d)
    dm16_sh = lax.dynamic_slice_in_dim(dm16, j_me * ncol, ncol, axis=2)
    col_sum = lambda a: (jnp.sum(a, axis=0, keepdims=True),)
    grads = {}
    grads["w_mod"] = jnp.stack([_matmul(cond16, dm16_sh[l], ta=True, tk=2 * N_DEV, name="mod_w")
                                for l in range(n_layers)])
    grads["b_mod"] = jnp.concatenate([
        _whole(col_sum, [dm16[l]], [((1, N_MOD * d), F32)], name="mod_bias_g")[0] for l in range(n_layers)], axis=0)
    d_cond = _matmul(dm16_sh[0], w_mod[0], tb=True, tm=2 * N_DEV, name="mod_b")
    for l in range(1, n_layers):
        d_cond = _matmul(dm16_sh[l], w_mod[l], tb=True, tm=2 * N_DEV, tiles=[d_cond], epi=add_epi, name="mod_b_acc")
    d_c16 = _whole_vjp(silu_fn, [c16], [d_cond], name="silu_c_b")[0]
    d_cctx_part = jnp.where(ac == 0, d_c16[N_DEV:].sum(axis=0), 0.0)

    for n in BIG:
        grads[n] = [None] * n_layers
    for l in range(n_layers):
        for n, g in zip(BIG, _reduce_scatter([big_grads[l][n] for n in BIG])):
            grads[n][l] = g
    for n in BIG:
        grads[n] = jnp.stack(grads[n])

    small_shapes = {n: small_grads[0][n].shape for n in SMALL}
    for n in SMALL:
        grads[n] = [None] * n_layers
    for l in range(n_layers):
        extra = [d_final_g.reshape(-1), d_cctx_part.reshape(-1)] if l == 0 else []
        flat = pad_rows(jnp.concatenate([small_grads[l][n].reshape(-1) for n in SMALL] + extra))
        red = _allgather8(flat, reduce=True, name="reduce_small").reshape(-1)
        off = 0
        for n in SMALL:
            sz = math.prod(small_shapes[n])
            grads[n][l] = red[off:off + sz].reshape(small_shapes[n])
            off += sz
        if l == 0:
            grads["final_g"] = red[off:off + d]
            grads["c_ctx"] = red[off + d:off + 2 * d]
    for n in SMALL:
        g_full = jnp.stack(grads[n])
        if n in SMALL_SHARDED:
            width = p[n].shape[-1]
            g_full = lax.dynamic_slice_in_dim(g_full, j_me * width, width, axis=g_full.ndim - 1)
        grads[n] = g_full.reshape(p[n].shape)

    delta, new_m, new_v = {}, {}, {}
    large = BIG + ("w_mod",)
    for n in large:
        delta[n], new_m[n], new_v[n] = _adamw_call(p[n], grads[n], mom[n], var[n], name="adamw")
    small = [n for n in WEIGHT_NAMES if n not in large]
    pack = lambda src: pad_rows(jnp.concatenate([src[n].reshape(-1) for n in small]))
    outs = _adamw_call(pack(p), pack(grads), pack(mom), pack(var), name="adamw_small")
    off = 0
    for n in small:
        sz = p[n].size
        delta[n], new_m[n], new_v[n] = (o.reshape(-1)[off:off + sz].reshape(p[n].shape) for o in outs)
        off += sz

    return (loss, grad_x, *[grads[n] for n in WEIGHT_NAMES], *[delta[n] for n in WEIGHT_NAMES],
            *[new_m[n] for n in WEIGHT_NAMES], *[new_v[n] for n in WEIGHT_NAMES])
```

```python
import functools
import math

import jax
import jax.numpy as jnp
from jax import lax
from jax.experimental import pallas as pl
from jax.experimental.pallas import tpu as pltpu

F32 = jnp.float32
BF16 = jnp.bfloat16
MESH = pl.DeviceIdType.MESH

V7X_VMEM_BYTES = 64 * 1024 * 1024
VMEM_LIMIT = V7X_VMEM_BYTES - 8 * 1024 * 1024
LANES = 128
SUBLANES = 8

N_BRANCH = 4
CHUNK = 128
GMLP_GROUPS = 4
CONV_W = 31
CONV_HALO = 16
HEAD_DIM = 64
N_KV_HEADS = 2
WINDOW = 128
BLOCK = 128
ROPE_BASE = 10000.0
GRID_W = 64
S5_GW = 16
S5_STATE = 64
N_MOD = 6
EPS = 1e-6
NEG_INF = -1e30
ADAM_LR = 0.001
ADAM_B1 = 0.9
ADAM_B2 = 0.999
ADAM_EPS = 1e-08
ADAM_WD = 0.01
ADAM_STEP = 10

ROW_TILE = 256


def _cparams(sem=None, **kw):
    if sem is not None:
        kw["dimension_semantics"] = sem
    return pltpu.CompilerParams(vmem_limit_bytes=VMEM_LIMIT, **kw)


def _pick(n, cands):
    for c in cands:
        if n % c == 0:
            return c
    return n


def _bdot(a, b, ca, cb):
    return lax.dot_general(a.astype(BF16), b.astype(BF16), (((ca,), (cb,)), ((), ())),
                           preferred_element_type=F32)


@jax.custom_vjp
def _mm_nn(a, b):
    return _bdot(a, b, 1, 0)


def _mm_nn_fwd(a, b):
    return _bdot(a, b, 1, 0), (a, b)


def _mm_nn_bwd(res, g):
    a, b = res
    return _bdot(g, b, 1, 1).astype(a.dtype), _bdot(a, g, 0, 0).astype(b.dtype)


_mm_nn.defvjp(_mm_nn_fwd, _mm_nn_bwd)


@jax.custom_vjp
def _mm_nt(a, b):
    return _bdot(a, b, 1, 1)


def _mm_nt_fwd(a, b):
    return _bdot(a, b, 1, 1), (a, b)


def _mm_nt_bwd(res, g):
    a, b = res
    return _bdot(g, b, 1, 0).astype(a.dtype), _bdot(g, a, 0, 0).astype(b.dtype)


_mm_nt.defvjp(_mm_nt_fwd, _mm_nt_bwd)


def _matmul(a, b, *, ta=False, tb=False, tm=None, tn=None, tk=None, name,
            out_dtypes=(F32,), epi=None, tiles=(), rowvecs=(), b_blocks=None, out_blocks=None):
    m, k = (a.shape[1], a.shape[0]) if ta else a.shape
    if b_blocks is None:
        k2, n = (b.shape[1], b.shape[0]) if tb else b.shape
        assert k == k2, (a.shape, b.shape, ta, tb)
    else:
        n = b_blocks[0]
    tm = tm or _pick(m, (1024, 512, 256, 128) if ta else (768, 512, 384, 256, 128))
    tn = tn or _pick(n, (2048, 1664, 1024, 512, 256, 128) if ta else (1024, 1664, 512, 256, 128))
    tk = tk or _pick(k, (1408, 768, 512, 384, 256, 128) if ta else (2048, 1664, 1024, 768, 512, 384, 256, 128))
    assert m % tm == 0 and n % tn == 0 and k % tk == 0, (m, n, k, tm, tn, tk)
    nk = k // tk
    n_t, n_v, n_o = len(tiles), len(rowvecs), len(out_dtypes)
    ca, cb = (0 if ta else 1), (1 if tb else 0)

    def body(*refs):
        a_ref, b_ref = refs[:2]
        t_refs = refs[2:2 + n_t]
        v_refs = refs[2 + n_t:2 + n_t + n_v]
        o_refs = refs[2 + n_t + n_v:2 + n_t + n_v + n_o]

        def finish(acc):
            outs = (acc,) if epi is None else epi(acc, *[t[...] for t in t_refs], *[v[...] for v in v_refs])
            for o_ref, o in zip(o_refs, outs):
                o_ref[...] = o.astype(o_ref.dtype)

        b_tile = b_ref[...]
        if b_tile.ndim == 3:
            b_tile = b_tile.reshape(b_tile.shape[0] * b_tile.shape[1], b_tile.shape[2])
        part = _bdot(a_ref[...], b_tile, ca, cb)
        if nk == 1:
            finish(part)
        else:
            acc_ref = refs[-1]
            kk = pl.program_id(2)

            @pl.when(kk == 0)
            def _():
                acc_ref[...] = part

            @pl.when(kk > 0)
            def _():
                acc_ref[...] += part

            @pl.when(kk == nk - 1)
            def _():
                finish(acc_ref[...])

    a_spec = pl.BlockSpec((tk, tm), lambda i, j, kk: (kk, i)) if ta else pl.BlockSpec((tm, tk), lambda i, j, kk: (i, kk))
    b_spec = pl.BlockSpec((tn, tk), lambda i, j, kk: (j, kk)) if tb else pl.BlockSpec((tk, tn), lambda i, j, kk: (kk, j))
    if b_blocks is not None:
        b_spec = pl.BlockSpec(b_blocks[1], b_blocks[2])
    in_specs = [a_spec, b_spec]
    in_specs += [pl.BlockSpec((tm, tn), lambda i, j, kk: (i, j)) for _ in tiles]
    in_specs += [pl.BlockSpec((1, tn), lambda i, j, kk: (0, j)) for _ in rowvecs]
    out_specs = [pl.BlockSpec((tm, tn), lambda i, j, kk: (i, j)) for _ in out_dtypes]
    out_shape = [jax.ShapeDtypeStruct((m, n), d) for d in out_dtypes]
    if out_blocks is not None:
        out_specs = [pl.BlockSpec(out_blocks[1], out_blocks[2])]
        out_shape = [jax.ShapeDtypeStruct(out_blocks[0], out_dtypes[0])]
    out = pl.pallas_call(
        body, name=name,
        grid=(m // tm, n // tn, nk),
        in_specs=in_specs,
        out_specs=out_specs,
        out_shape=out_shape,
        scratch_shapes=[pltpu.VMEM((tm, tn), F32)] if nk > 1 else [],
        compiler_params=_cparams(("parallel", "parallel", "arbitrary")),
    )(a, b, *tiles, *rowvecs)
    return out[0] if n_o == 1 else tuple(out)


def _param_spec(kind, p, n_lat):
    if kind == "mod":
        return pl.BlockSpec((None,) + p.shape[1:], lambda i: (i // n_lat,) + (0,) * (p.ndim - 1))
    return pl.BlockSpec(p.shape, lambda i: (0,) * p.ndim)


def _rowwise(fn, rows, params, outs, *, name, tm=ROW_TILE, s_rows=None):
    r = rows[0].shape[0]
    assert r % tm == 0, (r, tm)
    n_lat = r // tm + 1 if s_rows is None else s_rows // tm
    n_r, n_p = len(rows), len(params)

    def body(*refs):
        vals = [x[...] for x in refs[:n_r + n_p]]
        res = fn(*vals)
        for o_ref, o in zip(refs[n_r + n_p:], res):
            o_ref[...] = o.astype(o_ref.dtype)

    out = pl.pallas_call(
        body, name=name, grid=(r // tm,),
        in_specs=[pl.BlockSpec((tm, x.shape[1]), lambda i: (i, 0)) for x in rows]
        + [_param_spec(kind, p, n_lat) for kind, p in params],
        out_specs=[pl.BlockSpec((tm, w), lambda i: (i, 0)) for w, _ in outs],
        out_shape=[jax.ShapeDtypeStruct((r, w), d) for w, d in outs],
        compiler_params=_cparams(("parallel",)),
    )(*rows, *[p for _, p in params])
    return tuple(out)


def _rowwise_vjp(fn, rows, params, cts, *, name, row_grads, tm=ROW_TILE, s_rows=None, adds=None):
    r = rows[0].shape[0]
    assert r % tm == 0, (r, tm)
    n_lat = r // tm + 1 if s_rows is None else s_rows // tm
    adds = adds or {}
    n_r, n_p, n_c, n_a = len(rows), len(params), len(cts), len(adds)
    want = [i for i, d in enumerate(row_grads) if d is not None]
    add_at = {idx: k for k, idx in enumerate(sorted(adds))}

    def body(*refs):
        i = pl.program_id(0)
        prim = [x[...].astype(F32) for x in refs[:n_r + n_p]]
        ct = [x[...].astype(F32) for x in refs[n_r + n_p:n_r + n_p + n_c]]
        a_refs = refs[n_r + n_p + n_c:n_r + n_p + n_c + n_a]
        o_refs = refs[n_r + n_p + n_c + n_a:]
        _, vjp = jax.vjp(fn, *prim)
        grads = vjp(tuple(ct))
        for o_ref, idx in zip(o_refs[:len(want)], want):
            g = grads[idx] if idx not in add_at else grads[idx] + a_refs[add_at[idx]][...]
            o_ref[...] = g.astype(o_ref.dtype)
        for o_ref, (kind, _), g in zip(o_refs[len(want):], params, grads[n_r:]):
            first = (i == 0) | (i == n_lat) if kind == "mod" else (i == 0)

            @pl.when(first)
            def _(o_ref=o_ref, g=g):
                o_ref[...] = g

            @pl.when(jnp.logical_not(first))
            def _(o_ref=o_ref, g=g):
                o_ref[...] += g

    out = pl.pallas_call(
        body, name=name, grid=(r // tm,),
        in_specs=[pl.BlockSpec((tm, x.shape[1]), lambda i: (i, 0)) for x in rows]
        + [_param_spec(kind, p, n_lat) for kind, p in params]
        + [pl.BlockSpec((tm, c.shape[1]), lambda i: (i, 0)) for c in cts]
        + [pl.BlockSpec((tm, adds[idx].shape[1]), lambda i: (i, 0)) for idx in sorted(adds)],
        out_specs=[pl.BlockSpec((tm, rows[idx].shape[1]), lambda i: (i, 0)) for idx in want]
        + [_param_spec(kind, p, n_lat) for kind, p in params],
        out_shape=[jax.ShapeDtypeStruct(rows[idx].shape, row_grads[idx]) for idx in want]
        + [jax.ShapeDtypeStruct(p.shape, F32) for _, p in params],
        compiler_params=_cparams(("arbitrary",)),
    )(*rows, *[p for _, p in params], *cts, *[adds[idx] for idx in sorted(adds)])
    return tuple(out[:len(want)]), tuple(out[len(want):])


def _rms_mod(x, g, scale, shift):
    y = x * lax.rsqrt(jnp.mean(x * x, axis=-1, keepdims=True) + EPS)
    return ((y * g) * (1.0 + scale) + shift,)


def _resid(x, o, gate):
    return (x + gate * o,)


def _layer_norm(x, g, b):
    xc = x - jnp.mean(x, axis=-1, keepdims=True)
    var = jnp.mean(xc * xc, axis=-1, keepdims=True)
    return xc * lax.rsqrt(var + EPS) * g + b


def _gmlp(za, ln_g, ln_b, ws, b_full):
    db = za.shape[1] // 2
    gw = db // GMLP_GROUPS
    za = jax.nn.gelu(za)
    u, v = za[:, :db], za[:, db:]
    v = _layer_norm(v, ln_g, ln_b)
    chunks = []
    for n in range(za.shape[0] // CHUNK):
        vn = v[n * CHUNK:(n + 1) * CHUNK]
        cols = [_mm_nn(ws[g], vn[:, g * gw:(g + 1) * gw]) for g in range(GMLP_GROUPS)]
        chunks.append(jnp.concatenate(cols, axis=1) + b_full)
    mixed = chunks[0] if len(chunks) == 1 else jnp.concatenate(chunks, axis=0)
    return (u * mixed,)


def _glu_gate(zb):
    db = zb.shape[1] // 2
    return (zb[:, :db] * jax.nn.sigmoid(zb[:, db:]),)


def _ln_silu(yc, g, b):
    return (jax.nn.silu(_layer_norm(yc, g, b)),)


def _s5_act(ys, u, d_skip):
    return (jax.nn.gelu(ys + d_skip * u),)


def _merge(*args):
    g, ps, b = args[0], args[1:1 + N_BRANCH], args[1 + N_BRANCH]
    d = ps[0].shape[1]
    s = jax.nn.sigmoid(g + b)
    out = s[:, :d] * ps[0]
    for k in range(1, N_BRANCH):
        out = out + s[:, k * d:(k + 1) * d] * ps[k]
    return (out,)


def _loss_rows(x, tgt, g):
    y = x * lax.rsqrt(jnp.mean(x * x, axis=-1, keepdims=True) + EPS) * g
    e = y - tgt
    return (0.5 * jnp.mean(e * e, axis=-1, keepdims=True),)


def _adamw(w, g, m, v):
    m = ADAM_B1 * m + (1.0 - ADAM_B1) * g
    v = ADAM_B2 * v + (1.0 - ADAM_B2) * jnp.square(g)
    m_hat = m / (1.0 - ADAM_B1 ** ADAM_STEP)
    v_hat = v / (1.0 - ADAM_B2 ** ADAM_STEP)
    delta = -ADAM_LR * (m_hat / (jnp.sqrt(v_hat) + ADAM_EPS) + ADAM_WD * w)
    return delta, m, v


def _whole(fn, args, outs, *, name):
    n_a = len(args)

    def body(*refs):
        res = fn(*[x[...] for x in refs[:n_a]])
        for o_ref, o in zip(refs[n_a:], res):
            o_ref[...] = o.astype(o_ref.dtype)

    return tuple(pl.pallas_call(
        body, name=name,
        out_shape=[jax.ShapeDtypeStruct(s, d) for s, d in outs],
        compiler_params=_cparams(),
    )(*args))


def _whole_vjp(fn, args, cts, *, name):
    n_a, n_c = len(args), len(cts)

    def body(*refs):
        prim = [x[...] for x in refs[:n_a]]
        ct = [x[...] for x in refs[n_a:n_a + n_c]]
        _, vjp = jax.vjp(fn, *prim)
        for o_ref, g in zip(refs[n_a + n_c:], vjp(tuple(ct))):
            o_ref[...] = g

    return tuple(pl.pallas_call(
        body, name=name,
        out_shape=[jax.ShapeDtypeStruct(a.shape, F32) for a in args],
        compiler_params=_cparams(),
    )(*args, *cts))


def _conv_flags(i, n_lat, n_tiles):
    has_prev = jnp.logical_and(i != 0, i != n_lat)
    has_next = jnp.logical_and(i != n_lat - 1, i != n_tiles - 1)
    return has_prev, has_next


def _halo_specs(width, tm, n_rows):
    per = tm // CONV_HALO
    last = n_rows // CONV_HALO - 1
    prev = pl.BlockSpec((CONV_HALO, width), lambda i: (jnp.maximum(i * per - 1, 0), 0))
    cur = pl.BlockSpec((tm, width), lambda i: (i, 0))
    nxt = pl.BlockSpec((CONV_HALO, width), lambda i: (jnp.minimum((i + 1) * per, last), 0))
    return [prev, cur, nxt]


def _with_halo(prev, cur, nxt, has_prev, has_next):
    prev = jnp.where(has_prev, prev, 0.0)
    nxt = jnp.where(has_next, nxt, 0.0)
    return jnp.concatenate([prev, cur, nxt], axis=0)


def _conv_fwd(zb, conv_w, conv_b, ln_g, ln_b, *, n_lat, tm=ROW_TILE):
    r, db = zb.shape[0], zb.shape[1] // 2
    n_tiles = r // tm
    half = CONV_W // 2

    def body(zp_ref, zc_ref, zn_ref, w_ref, b_ref, g_ref, lb_ref, out_ref, y_ref, yc_ref):
        i = pl.program_id(0)
        has_prev, has_next = _conv_flags(i, n_lat, n_tiles)
        y = _glu_gate(zc_ref[...])[0]
        y_ext = _with_halo(_glu_gate(zp_ref[...])[0], y, _glu_gate(zn_ref[...])[0], has_prev, has_next)
        w = w_ref[...]
        acc = jnp.zeros((tm, db), F32) + b_ref[...]
        for k in range(CONV_W):
            s = CONV_HALO - half + k
            acc = acc + w[k:k + 1, :] * y_ext[s:s + tm, :]
        y_ref[...] = y
        yc_ref[...] = acc
        out_ref[...] = _ln_silu(acc, g_ref[...], lb_ref[...])[0].astype(out_ref.dtype)

    full = lambda p: pl.BlockSpec(p.shape, lambda i: (0,) * p.ndim)
    return pl.pallas_call(
        body, name="conv_fwd", grid=(n_tiles,),
        in_specs=_halo_specs(2 * db, tm, r) + [full(conv_w), full(conv_b), full(ln_g), full(ln_b)],
        out_specs=[pl.BlockSpec((tm, db), lambda i: (i, 0))] * 3,
        out_shape=[jax.ShapeDtypeStruct((r, db), BF16), jax.ShapeDtypeStruct((r, db), F32),
                   jax.ShapeDtypeStruct((r, db), F32)],
        compiler_params=_cparams(("parallel",)),
    )(zb, zb, zb, conv_w, conv_b, ln_g, ln_b)


def _conv_bwd(d_yc, y, zb, conv_w, *, n_lat, tm=ROW_TILE):
    r, db = y.shape
    n_tiles = r // tm
    half = CONV_W // 2

    def body(gp_ref, gc_ref, gn_ref, yp_ref, yc_ref, yn_ref, z_ref, w_ref, dz_ref, dw_ref, db_ref):
        i = pl.program_id(0)
        has_prev, has_next = _conv_flags(i, n_lat, n_tiles)
        g = gc_ref[...]
        g_ext = _with_halo(gp_ref[...], g, gn_ref[...], has_prev, has_next)
        y_ext = _with_halo(yp_ref[...], yc_ref[...], yn_ref[...], has_prev, has_next)
        w = w_ref[...]

        @pl.when(i == 0)
        def _():
            dw_ref[...] = jnp.zeros_like(dw_ref)
            db_ref[...] = jnp.zeros_like(db_ref)

        d_y = jnp.zeros((tm, db), F32)
        for k in range(CONV_W):
            s = CONV_HALO + half - k
            d_y = d_y + w[k:k + 1, :] * g_ext[s:s + tm, :]
            s = CONV_HALO - half + k
            dw_ref[pl.ds(k, 1), :] += jnp.sum(g * y_ext[s:s + tm, :], axis=0, keepdims=True)
        db_ref[...] += jnp.sum(g, axis=0, keepdims=True)
        _, vjp = jax.vjp(_glu_gate, z_ref[...])
        dz_ref[...] = vjp((d_y,))[0]

    return pl.pallas_call(
        body, name="conv_bwd", grid=(n_tiles,),
        in_specs=_halo_specs(db, tm, r) + _halo_specs(db, tm, r)
        + [pl.BlockSpec((tm, 2 * db), lambda i: (i, 0)), pl.BlockSpec(conv_w.shape, lambda i: (0, 0))],
        out_specs=[pl.BlockSpec((tm, 2 * db), lambda i: (i, 0)), pl.BlockSpec((CONV_W, db), lambda i: (0, 0)),
                   pl.BlockSpec((1, db), lambda i: (0, 0))],
        out_shape=[jax.ShapeDtypeStruct((r, 2 * db), F32), jax.ShapeDtypeStruct((CONV_W, db), F32),
                   jax.ShapeDtypeStruct((1, db), F32)],
        compiler_params=_cparams(("arbitrary",)),
    )(d_yc, d_yc, d_yc, y, y, y, zb, conv_w)


def _rope(x, cos, sin_signed, perm):
    return x * cos + jnp.dot(x, perm, precision=lax.Precision.HIGHEST, preferred_element_type=F32) * sin_signed


def _softmax3(s_loc, s_ctx, sink_col):
    m = sink_col
    if s_loc is not None:
        m = jnp.maximum(m, jnp.max(s_loc, axis=-1, keepdims=True))
    m = lax.stop_gradient(jnp.maximum(m, jnp.max(s_ctx, axis=-1, keepdims=True)))
    e_ctx = jnp.exp(s_ctx - m)
    den = jnp.sum(e_ctx, axis=-1, keepdims=True) + jnp.exp(sink_col - m)
    if s_loc is None:
        return None, e_ctx / den
    e_loc = jnp.exp(s_loc - m)
    den = den + jnp.sum(e_loc, axis=-1, keepdims=True)
    return e_loc / den, e_ctx / den


def _attn_latent(q4, kb, vb, kc, vc, sink_col, cq, sq, ck, sk, perm, allowed):
    qpk = q4.shape[0]
    scale = HEAD_DIM ** -0.5
    q = _rope(q4.reshape(qpk * BLOCK, HEAD_DIM), jnp.concatenate([cq] * qpk, 0), jnp.concatenate([sq] * qpk, 0), perm)
    k = _rope(kb, ck, sk, perm)
    s_loc = jnp.where(allowed, _mm_nt(q, k) * scale, NEG_INF)
    s_ctx = _mm_nt(q, kc) * scale
    p_loc, p_ctx = _softmax3(s_loc, s_ctx, sink_col)
    o = _mm_nn(p_loc, vb) + _mm_nn(p_ctx, vc)
    return o.reshape(qpk, BLOCK, HEAD_DIM)


def _attn_context(q4, kc, vc, sink_col):
    qpk = q4.shape[0]
    scale = HEAD_DIM ** -0.5
    s_ctx = _mm_nt(q4.reshape(qpk * BLOCK, HEAD_DIM), kc) * scale
    _, p_ctx = _softmax3(None, s_ctx, sink_col)
    return _mm_nn(p_ctx, vc).reshape(qpk, BLOCK, HEAD_DIM)


def _attn_specs(nkv, qpk, nq, n_ctx, s_len):
    blk = lambda off: (lambda i: (0, jnp.clip(i + off, 0, nq - 1), 0))
    tab = lambda off: (lambda i: (jnp.clip(i + off, 0, nq - 1), 0))
    q_spec = pl.BlockSpec((nkv, qpk, BLOCK, HEAD_DIM), lambda i: (0, 0, i, 0))
    band = [pl.BlockSpec((nkv, BLOCK, HEAD_DIM), blk(off)) for off in (-1, 0, 1)]
    ctx = pl.BlockSpec((nkv, n_ctx, HEAD_DIM), lambda i: (0, s_len // n_ctx, 0))
    sink = pl.BlockSpec((nkv, qpk * BLOCK, 1), lambda i: (0, 0, 0))
    tabs = [pl.BlockSpec((BLOCK, HEAD_DIM), tab(off)) for off in (-1, 0, 1)]
    perm = pl.BlockSpec((HEAD_DIM, HEAD_DIM), lambda i: (0, 0))
    return q_spec, band, ctx, sink, tabs, perm


def _attn_mask(i, qpk, s_len):
    qpos = i * BLOCK + lax.broadcasted_iota(jnp.int32, (BLOCK, 3 * BLOCK), 0)
    kpos = (i - 1) * BLOCK + lax.broadcasted_iota(jnp.int32, (BLOCK, 3 * BLOCK), 1)
    ok = (jnp.abs(qpos - kpos) <= WINDOW) & (kpos >= 0) & (kpos < s_len)
    return jnp.concatenate([ok] * qpk, axis=0)


def _attn_fwd(q, k, v, sink_rows, cos, sin_signed, perm, *, s_len):
    nkv, qpk, r, _ = q.shape
    nq, n_ctx = s_len // BLOCK, r - s_len
    q_spec, band, ctx, sink, tabs, perm_spec = _attn_specs(nkv, qpk, nq, n_ctx, s_len)

    def body(q_ref, kp, kc_, kn, vp, vc_, vn, kx, vx, sk_ref, cp, cc, cn, sp, sc, sn, perm_ref, o_ref):
        i = pl.program_id(0)

        @pl.when(i < nq)
        def _():
            allowed = _attn_mask(i, qpk, s_len)
            ck = jnp.concatenate([cp[...], cc[...], cn[...]], 0)
            sk = jnp.concatenate([sp[...], sc[...], sn[...]], 0)
            for g in range(nkv):
                kb = jnp.concatenate([kp[g], kc_[g], kn[g]], 0)
                vb = jnp.concatenate([vp[g], vc_[g], vn[g]], 0)
                o = _attn_latent(q_ref[g], kb, vb, kx[g], vx[g], sk_ref[g], cc[...], sc[...], ck, sk, perm_ref[...], allowed)
                o_ref[g] = o.astype(o_ref.dtype)

        @pl.when(i >= nq)
        def _():
            for g in range(nkv):
                o_ref[g] = _attn_context(q_ref[g], kx[g], vx[g], sk_ref[g]).astype(o_ref.dtype)

    return pl.pallas_call(
        body, name="attn_fwd", grid=(r // BLOCK,),
        in_specs=[q_spec] + band + band + [ctx, ctx, sink] + tabs + tabs + [perm_spec],
        out_specs=q_spec,
        out_shape=jax.ShapeDtypeStruct(q.shape, BF16),
        compiler_params=_cparams(("parallel",)),
    )(q, k, k, k, v, v, v, k, v, sink_rows, cos, cos, cos, sin_signed, sin_signed, sin_signed, perm)


def _attn_bwd(q, k, v, sink_rows, cos, sin_signed, perm, d_o, *, s_len):
    nkv, qpk, r, _ = q.shape
    nq, n_ctx = s_len // BLOCK, r - s_len
    n_steps = r // BLOCK
    q_spec, band, ctx, sink, tabs, perm_spec = _attn_specs(nkv, qpk, nq, n_ctx, s_len)

    def body(q_ref, kp, kc_, kn, vp, vc_, vn, kx, vx, sk_ref, cp, cc, cn, sp, sc, sn, perm_ref, do_ref,
             dq_ref, dk_hbm, dv_hbm, dsk_ref, dk_acc, dv_acc):
        i = pl.program_id(0)

        @pl.when(i == 0)
        def _():
            dk_acc[...] = jnp.zeros_like(dk_acc)
            dv_acc[...] = jnp.zeros_like(dv_acc)
            dsk_ref[...] = jnp.zeros_like(dsk_ref)

        ctx_rows = pl.ds(s_len, n_ctx)

        @pl.when(i < nq)
        def _():
            allowed = _attn_mask(i, qpk, s_len)
            ck = jnp.concatenate([cp[...], cc[...], cn[...]], 0)
            sk = jnp.concatenate([sp[...], sc[...], sn[...]], 0)
            for g in range(nkv):
                kb = jnp.concatenate([kp[g], kc_[g], kn[g]], 0)
                vb = jnp.concatenate([vp[g], vc_[g], vn[g]], 0)
                fn = lambda q4, kb_, vb_, kc, vc, s_col: _attn_latent(
                    q4, kb_, vb_, kc, vc, s_col, cc[...], sc[...], ck, sk, perm_ref[...], allowed)
                _, vjp = jax.vjp(fn, q_ref[g], kb, vb, kx[g], vx[g], sk_ref[g])
                dq4, dkb, dvb, dkc, dvc, dsk = vjp(do_ref[g].astype(F32))
                dq_ref[g] = dq4
                for seg, off in enumerate((-1, 0, 1)):
                    rows = pl.ds(pl.multiple_of(jnp.clip(i + off, 0, nq - 1) * BLOCK, BLOCK), BLOCK)
                    dk_acc[g, rows, :] += dkb[seg * BLOCK:(seg + 1) * BLOCK]
                    dv_acc[g, rows, :] += dvb[seg * BLOCK:(seg + 1) * BLOCK]
                dk_acc[g, ctx_rows, :] += dkc
                dv_acc[g, ctx_rows, :] += dvc
                dsk_ref[g] += dsk

        @pl.when(i >= nq)
        def _():
            for g in range(nkv):
                _, vjp = jax.vjp(_attn_context, q_ref[g], kx[g], vx[g], sk_ref[g])
                dq4, dkc, dvc, dsk = vjp(do_ref[g].astype(F32))
                dq_ref[g] = dq4
                dk_acc[g, ctx_rows, :] += dkc
                dv_acc[g, ctx_rows, :] += dvc
                dsk_ref[g] += dsk

        @pl.when(i == n_steps - 1)
        def _():
            pltpu.sync_copy(dk_acc, dk_hbm)
            pltpu.sync_copy(dv_acc, dv_hbm)

    any_spec = pl.BlockSpec(memory_space=pl.ANY)
    return pl.pallas_call(
        body, name="attn_bwd", grid=(n_steps,),
        in_specs=[q_spec] + band + band + [ctx, ctx, sink] + tabs + tabs + [perm_spec, q_spec],
        out_specs=[q_spec, any_spec, any_spec, sink],
        out_shape=[jax.ShapeDtypeStruct(q.shape, F32), jax.ShapeDtypeStruct(k.shape, F32),
                   jax.ShapeDtypeStruct(v.shape, F32), jax.ShapeDtypeStruct(sink_rows.shape, F32)],
        scratch_shapes=[pltpu.VMEM(k.shape, F32), pltpu.VMEM(v.shape, F32)],
        compiler_params=_cparams(("arbitrary",)),
    )(q, k, k, k, v, v, v, k, v, sink_rows, cos, cos, cos, sin_signed, sin_signed, sin_signed, perm, d_o)


def _scan_orders(n_lat, n_ctx):
    fwd = lambda i: jnp.where(i < n_ctx, n_lat + i, i - n_ctx)
    bwd = lambda i: jnp.where(i < n_ctx, n_lat + n_ctx - 1 - i, n_lat - 1 - (i - n_ctx))
    return fwd, bwd


def _s5_scan(bu_f, bu_b, lbar, *, n_lat, tb=ROW_TILE):
    r, ch2, _ = bu_f.shape
    ch = ch2 // 2
    n_tiles = r // tb
    of, ob = _scan_orders(n_lat, n_tiles - n_lat)

    def body(bf_ref, bb_ref, a_ref, sf_ref, sb_ref, st_ref):
        @pl.when(pl.program_id(0) == 0)
        def _():
            st_ref[...] = jnp.zeros_like(st_ref)

        afr, afi, abr, abi = a_ref[0], a_ref[1], a_ref[2], a_ref[3]

        def step(t, carry):
            sfr, sfi, sbr, sbi = carry
            x = bf_ref[t]
            nfr = afr * sfr - afi * sfi + x[:ch]
            nfi = afr * sfi + afi * sfr + x[ch:]
            sf_ref[t] = jnp.concatenate([nfr, nfi], axis=0)
            u = tb - 1 - t
            x = bb_ref[u]
            nbr = abr * sbr - abi * sbi + x[:ch]
            nbi = abr * sbi + abi * sbr + x[ch:]
            sb_ref[u] = jnp.concatenate([nbr, nbi], axis=0)
            return nfr, nfi, nbr, nbi

        out = lax.fori_loop(0, tb, step, (st_ref[0], st_ref[1], st_ref[2], st_ref[3]), unroll=4)
        for n in range(4):
            st_ref[n] = out[n]

    spec = lambda order: pl.BlockSpec((tb, ch2, LANES), lambda i: (order(i), 0, 0))
    return pl.pallas_call(
        body, name="s5_scan", grid=(n_tiles,),
        in_specs=[spec(of), spec(ob), pl.BlockSpec(lbar.shape, lambda i: (0, 0, 0))],
        out_specs=[spec(of), spec(ob)],
        out_shape=[jax.ShapeDtypeStruct(bu_f.shape, F32), jax.ShapeDtypeStruct(bu_b.shape, F32)],
        scratch_shapes=[pltpu.VMEM((4, ch, LANES), F32)],
        compiler_params=_cparams(("arbitrary",)),
    )(bu_f, bu_b, lbar)


def _s5_scan_bwd(ds_f, ds_b, s_f, s_b, lbar, *, n_lat, tb=ROW_TILE):
    r, ch2, _ = ds_f.shape
    ch = ch2 // 2
    n_tiles = r // tb
    of, ob = _scan_orders(n_lat, n_tiles - n_lat)
    rof = lambda i: of(n_tiles - 1 - i)
    rob = lambda i: ob(n_tiles - 1 - i)

    def body(gf_ref, gb_ref, sf_ref, sb_ref, a_ref, df_ref, db_ref, da_ref, st_ref):
        @pl.when(pl.program_id(0) == 0)
        def _():
            st_ref[...] = jnp.zeros_like(st_ref)
            da_ref[...] = jnp.zeros_like(da_ref)

        afr, afi, abr, abi = a_ref[0], a_ref[1], a_ref[2], a_ref[3]

        def one(g_re, g_im, acc_re, acc_im, a_re, a_im, s, ds):
            acc_re = acc_re + s[:ch] * g_re + s[ch:] * g_im
            acc_im = acc_im - s[ch:] * g_re + s[:ch] * g_im
            n_re = ds[:ch] + a_re * g_re + a_im * g_im
            n_im = ds[ch:] - a_im * g_re + a_re * g_im
            return n_re, n_im, acc_re, acc_im

        def step(t, carry):
            gfr, gfi, gbr, gbi, cfr, cfi, cbr, cbi = carry
            u = tb - 1 - t
            gfr, gfi, cfr, cfi = one(gfr, gfi, cfr, cfi, afr, afi, sf_ref[u], gf_ref[u])
            df_ref[u] = jnp.concatenate([gfr, gfi], axis=0)
            gbr, gbi, cbr, cbi = one(gbr, gbi, cbr, cbi, abr, abi, sb_ref[t], gb_ref[t])
            db_ref[t] = jnp.concatenate([gbr, gbi], axis=0)
            return gfr, gfi, gbr, gbi, cfr, cfi, cbr, cbi

        zero = jnp.zeros((ch, LANES), F32)
        out = lax.fori_loop(0, tb, step, (st_ref[0], st_ref[1], st_ref[2], st_ref[3], zero, zero, zero, zero), unroll=4)
        for n in range(4):
            st_ref[n] = out[n]
            da_ref[n] += out[4 + n]

    spec = lambda order: pl.BlockSpec((tb, ch2, LANES), lambda i: (order(i), 0, 0))
    return pl.pallas_call(
        body, name="s5_scan_bwd", grid=(n_tiles,),
        in_specs=[spec(rof), spec(rob), spec(rof), spec(rob), pl.BlockSpec(lbar.shape, lambda i: (0, 0, 0))],
        out_specs=[spec(rof), spec(rob), pl.BlockSpec(lbar.shape, lambda i: (0, 0, 0))],
        out_shape=[jax.ShapeDtypeStruct(ds_f.shape, F32), jax.ShapeDtypeStruct(ds_b.shape, F32),
                   jax.ShapeDtypeStruct(lbar.shape, F32)],
        scratch_shapes=[pltpu.VMEM((4, ch, LANES), F32)],
        compiler_params=_cparams(("arbitrary",)),
    )(ds_f, ds_b, s_f, s_b, lbar)


def _s5_discretise(a_re, a_im, log_step, b_re, b_im):
    dt = jnp.exp(log_step)
    mag = jnp.exp(a_re * dt)
    l_re, l_im = mag * jnp.cos(a_im * dt), mag * jnp.sin(a_im * dt)
    den = a_re * a_re + a_im * a_im
    q_re = ((l_re - 1.0) * a_re + l_im * a_im) / den
    q_im = (l_im * a_re - (l_re - 1.0) * a_im) / den
    bb_re = q_re * b_re[None] - q_im * b_im[None]
    bb_im = q_re * b_im[None] + q_im * b_re[None]
    return l_re, l_im, bb_re, bb_im


N_DEV = 8
N_CHIPS = 4


def _place():
    x, y, c = lax.axis_index("x"), lax.axis_index("y"), lax.axis_index("c")
    chips = [(1 - x, y), (x, 1 - y), (1 - x, 1 - y)]
    return x, y, c, chips


def _allgather8(v, *, reduce=False, name):
    m_per, n = v.shape

    def body(x_ref, out_ref, *scratch):
        if reduce:
            all_ref, send_sems, recv_sems, local_sem = scratch
        else:
            all_ref = out_ref
            send_sems, recv_sems, local_sem = scratch
        x, y, c, chips = _place()
        me, sibling = (x, y, c), (x, y, 1 - c)

        def rows(px, py, pc):
            return all_ref.at[pl.ds((4 * px + 2 * py + pc) * m_per, m_per), :]

        def copy(k, block, to, src=None):
            return pltpu.make_async_remote_copy(
                src_ref=rows(*block) if src is None else src, dst_ref=rows(*block),
                send_sem=send_sems.at[k], recv_sem=recv_sems.at[k], device_id=to, device_id_type=MESH)

        mine = pltpu.make_async_copy(x_ref, rows(*me), local_sem)
        mine.start()
        first = [copy(0, me, sibling, src=x_ref)]
        first += [copy(1 + j, me, (*chip, c), src=x_ref) for j, chip in enumerate(chips)]
        for cp in first:
            cp.start()
        passed = [copy(4 + j, (*chip, c), sibling) for j, chip in enumerate(chips)]
        for j, chip in enumerate(chips):
            copy(1 + j, (*chip, c), me).wait_recv()
            passed[j].start()
        copy(0, sibling, me).wait_recv()
        for j, chip in enumerate(chips):
            copy(4 + j, (*chip, 1 - c), me).wait_recv()
        for cp in first + passed:
            cp.wait_send()
        mine.wait()
        if reduce:
            acc = all_ref[pl.ds(0, m_per), :]
            for d in range(1, N_DEV):
                acc = acc + all_ref[pl.ds(d * m_per, m_per), :]
            out_ref[...] = acc

    sems = [pltpu.SemaphoreType.DMA((7,)), pltpu.SemaphoreType.DMA((7,)), pltpu.SemaphoreType.DMA]
    return pl.pallas_call(
        body, name=name,
        out_shape=jax.ShapeDtypeStruct((m_per if reduce else N_DEV * m_per, n), v.dtype),
        in_specs=[pl.BlockSpec(memory_space=pltpu.VMEM)],
        out_specs=pl.BlockSpec(memory_space=pltpu.VMEM),
        scratch_shapes=([pltpu.VMEM((N_DEV * m_per, n), v.dtype)] if reduce else []) + sems,
        compiler_params=_cparams(),
    )(v)


def _halves(ref, half):
    h = ref.shape[0] // 2
    return ref.at[pl.ds(half * h, h)]


def _gather_shards(stacks):
    n_w = len(stacks)

    def body(*refs):
        out_refs = refs[n_w:2 * n_w]
        send_sems, recv_sems = refs[2 * n_w:]
        x, y, c, _ = _place()
        me, sibling = (x, y, c), (x, y, 1 - c)
        first, second, diag = _stage_peers()

        def copy(n, k, chip, half, to):
            blk = _halves(out_refs[n].at[2 * chip[0] + chip[1]], half)
            return pltpu.make_async_remote_copy(
                src_ref=blk, dst_ref=blk, send_sem=send_sems.at[n, k], recv_sem=recv_sems.at[n, k],
                device_id=to, device_id_type=MESH)

        sent = [copy(n, k, (x, y), c, (*peer, c)) for n in range(n_w) for k, peer in enumerate((first, second))]
        for cp in sent:
            cp.start()
        for n in range(n_w):
            copy(n, 0, first, c, me).wait_recv()
            sent += [copy(n, 2, first, c, (*second, c)), copy(n, 3, first, c, sibling)]
            sent[-2].start()
            sent[-1].start()
        for n in range(n_w):
            copy(n, 1, second, c, me).wait_recv()
            sent.append(copy(n, 4, second, c, sibling))
            sent[-1].start()
            copy(n, 2, diag, c, me).wait_recv()
            sent.append(copy(n, 5, diag, c, sibling))
            sent[-1].start()
        for n in range(n_w):
            copy(n, 3, second, 1 - c, me).wait_recv()
            copy(n, 4, first, 1 - c, me).wait_recv()
            copy(n, 5, diag, 1 - c, me).wait_recv()
        for cp in sent:
            cp.wait_send()

    any_spec = pl.BlockSpec(memory_space=pl.ANY)
    return pl.pallas_call(
        body, name="gather_shards",
        out_shape=[jax.ShapeDtypeStruct(s.shape, s.dtype) for s in stacks],
        in_specs=[any_spec] * n_w, out_specs=[any_spec] * n_w,
        input_output_aliases={n: n for n in range(n_w)},
        scratch_shapes=[pltpu.SemaphoreType.DMA((n_w, 6)), pltpu.SemaphoreType.DMA((n_w, 6))],
        compiler_params=_cparams(),
    )(*stacks)


def _rs_swap_in(grads):
    n_w = len(grads)

    def body(*refs):
        x_refs, out_refs = refs[:n_w], refs[n_w:2 * n_w]
        send_sems, recv_sems = refs[2 * n_w:]
        x, y, c, _ = _place()
        cps = []
        for n in range(n_w):
            h = x_refs[n].shape[1] // 2
            cps.append(pltpu.make_async_remote_copy(
                src_ref=x_refs[n].at[:, pl.ds((1 - c) * h, h)], dst_ref=out_refs[n],
                send_sem=send_sems.at[n], recv_sem=recv_sems.at[n], device_id=(x, y, 1 - c), device_id_type=MESH))
        for cp in cps:
            cp.start()
        for cp in cps:
            cp.wait()

    any_spec = pl.BlockSpec(memory_space=pl.ANY)
    return pl.pallas_call(
        body, name="rs_swap_in",
        out_shape=[jax.ShapeDtypeStruct((g.shape[0], g.shape[1] // 2) + g.shape[2:], g.dtype) for g in grads],
        in_specs=[any_spec] * n_w, out_specs=[any_spec] * n_w,
        scratch_shapes=[pltpu.SemaphoreType.DMA((n_w,)), pltpu.SemaphoreType.DMA((n_w,))],
        compiler_params=_cparams(),
    )(*grads)


def _stage_peers():
    x, y, c = lax.axis_index("x"), lax.axis_index("y"), lax.axis_index("c")
    fx, fy = 1 - x, 1 - y
    first = (x + (1 - c) * (fx - x), y + c * (fy - y))
    second = (x + c * (fx - x), y + (1 - c) * (fy - y))
    return first, second, (fx, fy)


def _chip_exchange_1(parts):
    n_w = len(parts)

    def body(*refs):
        x_refs, out_refs = refs[:n_w], refs[n_w:2 * n_w]
        send_sems, recv_sems = refs[2 * n_w:]
        c = lax.axis_index("c")
        first, _, diag = _stage_peers()
        cps = [pltpu.make_async_remote_copy(
            src_ref=x_refs[n].at[2 * chip[0] + chip[1]], dst_ref=out_refs[n].at[k],
            send_sem=send_sems.at[n, k], recv_sem=recv_sems.at[n, k], device_id=(*first, c), device_id_type=MESH)
            for n in range(n_w) for k, chip in enumerate((first, diag))]
        for cp in cps:
            cp.start()
        for cp in cps:
            cp.wait()

    any_spec = pl.BlockSpec(memory_space=pl.ANY)
    return pl.pallas_call(
        body, name="chip_exchange_1",
        out_shape=[jax.ShapeDtypeStruct((2,) + v.shape[1:], v.dtype) for v in parts],
        in_specs=[any_spec] * n_w, out_specs=[any_spec] * n_w,
        scratch_shapes=[pltpu.SemaphoreType.DMA((n_w, 2)), pltpu.SemaphoreType.DMA((n_w, 2))],
        compiler_params=_cparams(),
    )(*parts)


def _chip_exchange_2(parts):
    n_w = len(parts)

    def body(*refs):
        x_refs, out_refs = refs[:n_w], refs[n_w:2 * n_w]
        send_sems, recv_sems = refs[2 * n_w:]
        c = lax.axis_index("c")
        _, second, _ = _stage_peers()
        cps = [pltpu.make_async_remote_copy(
            src_ref=x_refs[n].at[1], dst_ref=out_refs[n], send_sem=send_sems.at[n], recv_sem=recv_sems.at[n],
            device_id=(*second, c), device_id_type=MESH) for n in range(n_w)]
        for cp in cps:
            cp.start()
        for cp in cps:
            cp.wait()

    any_spec = pl.BlockSpec(memory_space=pl.ANY)
    return pl.pallas_call(
        body, name="chip_exchange_2",
        out_shape=[jax.ShapeDtypeStruct(v.shape[1:], v.dtype) for v in parts],
        in_specs=[any_spec] * n_w, out_specs=[any_spec] * n_w,
        scratch_shapes=[pltpu.SemaphoreType.DMA((n_w,)), pltpu.SemaphoreType.DMA((n_w,))],
        compiler_params=_cparams(),
    )(*parts)


def _rs_finish(reduced):
    n_w = len(reduced)

    def body(*refs):
        out_refs = refs[n_w:2 * n_w]
        send_sems, recv_sems = refs[2 * n_w:]
        x, y, c, _ = _place()

        def copy(n, half):
            blk = _halves(out_refs[n], half)
            return pltpu.make_async_remote_copy(
                src_ref=blk, dst_ref=blk, send_sem=send_sems.at[n], recv_sem=recv_sems.at[n],
                device_id=(x, y, 1 - c), device_id_type=MESH)

        for n in range(n_w):
            copy(n, c).start()
        for n in range(n_w):
            copy(n, c).wait_send()
            copy(n, 1 - c).wait_recv()

    any_spec = pl.BlockSpec(memory_space=pl.ANY)
    return pl.pallas_call(
        body, name="rs_finish",
        out_shape=[jax.ShapeDtypeStruct(v.shape, v.dtype) for v in reduced],
        in_specs=[any_spec] * n_w, out_specs=[any_spec] * n_w,
        input_output_aliases={n: n for n in range(n_w)},
        scratch_shapes=[pltpu.SemaphoreType.DMA((n_w,)), pltpu.SemaphoreType.DMA((n_w,))],
        compiler_params=_cparams(),
    )(*reduced)


def _as_rows(shape):
    return (math.prod(shape[:-1]), shape[-1])


def _row_tile(rows, cols):
    return _pick(rows, tuple(t for t in (2048, 1024, 512, 256, 128, 64, 32, 16, 8) if t * cols * 4 <= (1 << 21)))


def _pair_sum(g, t, c_idx):
    rows, cols = _as_rows(t.shape[1:])
    tr = _row_tile(rows, cols)

    def body(c_ref, g_ref, t_ref, o_ref):
        o_ref[...] = g_ref[...] + t_ref[...]

    out = pl.pallas_call(
        body, name="rs_pair_sum",
        grid_spec=pltpu.PrefetchScalarGridSpec(
            num_scalar_prefetch=1, grid=(N_CHIPS, rows // tr),
            in_specs=[pl.BlockSpec((None, None, tr, cols), lambda j, i, c: (j, c[0], i, 0)),
                      pl.BlockSpec((None, tr, cols), lambda j, i, c: (j, i, 0))],
            out_specs=pl.BlockSpec((None, tr, cols), lambda j, i, c: (j, i, 0))),
        out_shape=jax.ShapeDtypeStruct((N_CHIPS, rows, cols), F32),
        compiler_params=_cparams(("parallel", "parallel")),
    )(c_idx, g.reshape(N_CHIPS, 2, rows, cols), t.reshape(N_CHIPS, rows, cols))
    return out.reshape(t.shape)


def _chip_sum_1(q, u, idx):
    rows, cols = _as_rows(q.shape[1:])
    tr = _row_tile(rows, cols)

    def body(idx_ref, q_ref, u_ref, o_ref):
        o_ref[...] = q_ref[...] + u_ref[...]

    out = pl.pallas_call(
        body, name="rs_chip_sum_1",
        grid_spec=pltpu.PrefetchScalarGridSpec(
            num_scalar_prefetch=1, grid=(2, rows // tr),
            in_specs=[pl.BlockSpec((None, tr, cols), lambda s, i, idx: (idx[s], i, 0)),
                      pl.BlockSpec((None, tr, cols), lambda s, i, idx: (s, i, 0))],
            out_specs=pl.BlockSpec((None, tr, cols), lambda s, i, idx: (s, i, 0))),
        out_shape=jax.ShapeDtypeStruct((2, rows, cols), F32),
        compiler_params=_cparams(("parallel", "parallel")),
    )(idx, q.reshape(N_CHIPS, rows, cols), u.reshape(2, rows, cols))
    return out.reshape((2,) + q.shape[1:])


def _chip_sum_2(s, u, idx):
    rows, cols = _as_rows(u.shape)
    tr = _row_tile(rows, cols)
    nb = rows // tr

    def body(idx_ref, s_ref, u_ref, o_ref):
        o_ref[...] = s_ref[...] + u_ref[...]

    out = pl.pallas_call(
        body, name="rs_chip_sum_2",
        grid_spec=pltpu.PrefetchScalarGridSpec(
            num_scalar_prefetch=1, grid=(nb,),
            in_specs=[pl.BlockSpec((None, tr, cols), lambda i, idx: (0, i, 0)),
                      pl.BlockSpec((tr, cols), lambda i, idx: (i, 0))],
            out_specs=pl.BlockSpec((tr, cols), lambda i, idx: (idx[2] * nb + i, 0))),
        out_shape=jax.ShapeDtypeStruct((2 * rows, cols), F32),
        compiler_params=_cparams(("parallel",)),
    )(idx, s.reshape(2, rows, cols), u.reshape(rows, cols))
    return out.reshape((2 * u.shape[0],) + u.shape[1:])


def _reduce_scatter(grads):
    x, y, c = lax.axis_index("x"), lax.axis_index("y"), lax.axis_index("c")
    c_idx = jnp.reshape(c, (1,)).astype(jnp.int32)
    _, second, _ = _stage_peers()
    idx = jnp.stack([2 * x + y, 2 * second[0] + second[1], c]).astype(jnp.int32)
    theirs = _rs_swap_in(grads)
    pair = [_pair_sum(g, t, c_idx) for g, t in zip(grads, theirs)]
    got = _chip_exchange_1(pair)
    part = [_chip_sum_1(q, u, idx) for q, u in zip(pair, got)]
    got = _chip_exchange_2(part)
    reduced = [_chip_sum_2(s, u, idx) for s, u in zip(part, got)]
    return _rs_finish(reduced)


WEIGHT_NAMES = ('c_ctx', 'w_mod', 'b_mod', 'norm1_g', 'norm2_g', 'w_in', 'gmlp_ln_g', 'gmlp_ln_b', 'gmlp_ws',
                'gmlp_bs', 'conv_w', 'conv_b', 'conv_ln_g', 'conv_ln_b', 'attn_sink', 's5_a_re', 's5_a_im',
                's5_log_step', 's5_b_re', 's5_b_im', 's5_c_re', 's5_c_im', 's5_d', 's5_w_glu', 'w_branch',
                'w_gate', 'b_gate', 'w_out', 'w_ff1', 'w_ff2', 'final_g')
BIG = ('w_in', 's5_w_glu', 'w_branch', 'w_gate', 'w_out', 'w_ff1', 'w_ff2')
SMALL = ('norm1_g', 'norm2_g', 'gmlp_ln_g', 'gmlp_ln_b', 'gmlp_ws', 'gmlp_bs', 'conv_w', 'conv_b', 'conv_ln_g',
         'conv_ln_b', 'attn_sink', 's5_a_re', 's5_a_im', 's5_log_step', 's5_b_re', 's5_b_im', 's5_c_re',
         's5_c_im', 's5_d', 'b_gate')
SMALL_SHARDED = ('conv_w', 'b_gate')


def _from_shards(name, sh):
    j = sh.shape[0]
    if name in ('w_in', 's5_w_glu', 'w_ff1'):
        return sh.transpose(1, 0, 2).reshape(sh.shape[1], j * sh.shape[2])
    if name == 'w_branch':
        return sh.transpose(1, 2, 0, 3).reshape(sh.shape[1], sh.shape[2], j * sh.shape[3])
    if name == 'w_gate':
        return sh.transpose(0, 2, 1, 3).reshape(j * sh.shape[2], sh.shape[1] * sh.shape[3])
    return sh.reshape(j * sh.shape[1], sh.shape[2])


def _to_shards(name, dw, j=N_CHIPS):
    if name in ('w_in', 's5_w_glu', 'w_ff1'):
        return dw.reshape(dw.shape[0], j, dw.shape[1] // j).transpose(1, 0, 2)
    if name == 'w_branch':
        return dw.reshape(dw.shape[0], dw.shape[1], j, dw.shape[2] // j).transpose(2, 0, 1, 3)
    if name == 'w_gate':
        d = dw.shape[0]
        return dw.reshape(j, d // j, dw.shape[1] // d, d).transpose(0, 2, 1, 3)
    return dw.reshape(j, dw.shape[0] // j, dw.shape[1])


def _pack_rows(flat_parts, lead):
    flat = jnp.concatenate(flat_parts, axis=-1)
    assert flat.shape[-1] % LANES == 0, flat.shape
    return flat.reshape(lead + (flat.shape[-1] // LANES, LANES))


def _block_diag(blocks):
    g, a, b = blocks.shape
    eye = jnp.eye(g, dtype=blocks.dtype)
    return (blocks[:, :, None, :] * eye[:, None, :, None]).reshape(g * a, g * b)


def _diag_blocks(mat, g):
    a, b = mat.shape[0] // g, mat.shape[1] // g
    eye = jnp.eye(g, dtype=mat.dtype)
    return (mat.reshape(g, a, g, b) * eye[:, None, :, None]).sum(axis=2)


def _rope_tables(s_len):
    rows = s_len // GRID_W
    row = jnp.repeat(jnp.arange(rows), GRID_W).astype(F32)
    col = jnp.tile(jnp.arange(GRID_W), rows).astype(F32)
    d = HEAD_DIM // 2
    inv = ROPE_BASE ** (-jnp.arange(0, d, 2, dtype=F32) / d)
    ar, ac = row[:, None] * inv[None, :], col[:, None] * inv[None, :]
    cos = jnp.concatenate([jnp.cos(ar), jnp.cos(ar), jnp.cos(ac), jnp.cos(ac)], axis=1)
    sin_signed = jnp.concatenate([-jnp.sin(ar), jnp.sin(ar), -jnp.sin(ac), jnp.sin(ac)], axis=1)
    idx = jnp.arange(HEAD_DIM)
    partner = jnp.where(idx % d < d // 2, idx + d // 2, idx - d // 2)
    perm = (idx[:, None] == partner[None, :]).astype(F32)
    return cos, sin_signed, perm


def _loss_and_grad(xa, tgt, g, *, s_len, tm=ROW_TILE):
    d = xa.shape[1]

    def body(x_ref, t_ref, g_ref, l_ref, dx_ref, dg_ref):
        i = pl.program_id(0)
        (rows,), vjp = jax.vjp(_loss_rows, x_ref[...], t_ref[...], g_ref[...])
        dx, _, dg = vjp((jnp.ones_like(rows),))
        dx_ref[...] = dx
        part = jnp.zeros(l_ref.shape, F32) + jnp.sum(rows)

        @pl.when(i == 0)
        def _():
            l_ref[...] = part
            dg_ref[...] = dg

        @pl.when(i > 0)
        def _():
            l_ref[...] += part
            dg_ref[...] += dg

    return pl.pallas_call(
        body, name="loss_head", grid=(s_len // tm,),
        in_specs=[pl.BlockSpec((tm, d), lambda i: (i, 0)), pl.BlockSpec((tm, d), lambda i: (i, 0)),
                  pl.BlockSpec((1, d), lambda i: (0, 0))],
        out_specs=[pl.BlockSpec((SUBLANES, LANES), lambda i: (0, 0)), pl.BlockSpec((tm, d), lambda i: (i, 0)),
                   pl.BlockSpec((1, d), lambda i: (0, 0))],
        out_shape=[jax.ShapeDtypeStruct((SUBLANES, LANES), F32), jax.ShapeDtypeStruct((s_len, d), F32),
                   jax.ShapeDtypeStruct((1, d), F32)],
        compiler_params=_cparams(("arbitrary",)),
    )(xa, tgt, g)


def _adamw_call(w, g, m, v, *, name):
    shape = w.shape
    cols = shape[-1] if w.ndim > 1 else LANES
    two_d = lambda a: a.reshape(-1, cols)
    rows = two_d(w).shape[0]
    tm = _pick(rows, tuple(t for t in (1024, 512, 256, 128, 64, 32, 16, 8) if t * cols * 4 <= (1 << 20)))
    outs = _rowwise(_adamw, [two_d(w), two_d(g), two_d(m), two_d(v)], [], [(cols, F32)] * 3, name=name, tm=tm)
    return tuple(o.reshape(shape) for o in outs)


def kernel(x, c, ctx, c_ctx, w_mod, b_mod, norm1_g, norm2_g, w_in, gmlp_ln_g, gmlp_ln_b, gmlp_ws, gmlp_bs, conv_w, conv_b, conv_ln_g, conv_ln_b, attn_sink, s5_a_re, s5_a_im, s5_log_step, s5_b_re, s5_b_im, s5_c_re, s5_c_im, s5_d, s5_w_glu, w_branch, w_gate, b_gate, w_out, w_ff1, w_ff2, final_g, loss_target, m_c_ctx, m_w_mod, m_b_mod, m_norm1_g, m_norm2_g, m_w_in, m_gmlp_ln_g, m_gmlp_ln_b, m_gmlp_ws, m_gmlp_bs, m_conv_w, m_conv_b, m_conv_ln_g, m_conv_ln_b, m_attn_sink, m_s5_a_re, m_s5_a_im, m_s5_log_step, m_s5_b_re, m_s5_b_im, m_s5_c_re, m_s5_c_im, m_s5_d, m_s5_w_glu, m_w_branch, m_w_gate, m_b_gate, m_w_out, m_w_ff1, m_w_ff2, m_final_g, v_c_ctx, v_w_mod, v_b_mod, v_norm1_g, v_norm2_g, v_w_in, v_gmlp_ln_g, v_gmlp_ln_b, v_gmlp_ws, v_gmlp_bs, v_conv_w, v_conv_b, v_conv_ln_g, v_conv_ln_b, v_attn_sink, v_s5_a_re, v_s5_a_im, v_s5_log_step, v_s5_b_re, v_s5_b_im, v_s5_c_re, v_s5_c_im, v_s5_d, v_s5_w_glu, v_w_branch, v_w_gate, v_b_gate, v_w_out, v_w_ff1, v_w_ff2, v_final_g):
    given = dict(locals())
    p = {n: given[n] for n in WEIGHT_NAMES}
    mom = {n: given["m_" + n] for n in WEIGHT_NAMES}
    var = {n: given["v_" + n] for n in WEIGHT_NAMES}

    s_len, d = x.shape[1], x.shape[2]
    n_ctx = ctx.shape[1]
    r = s_len + n_ctx
    n_layers = w_mod.shape[0]
    db = d // N_BRANCH
    n_q = db // HEAD_DIM
    qpk = n_q // N_KV_HEADS
    g5 = db // S5_GW
    gp = g5 * S5_STATE
    ch = gp // LANES
    gw = db // GMLP_GROUPS
    n_lat = s_len // ROW_TILE
    assert s_len % ROW_TILE == 0 and n_ctx % ROW_TILE == 0 and s_len % n_ctx == 0 and gp % LANES == 0
    ax, ay, ac = lax.axis_index("x"), lax.axis_index("y"), lax.axis_index("c")
    j_me = 2 * ax + ay
    d_me = 4 * ax + 2 * ay + ac
    rw = functools.partial(_rowwise, s_rows=s_len)
    rwv = functools.partial(_rowwise_vjp, s_rows=s_len)
    add_epi = lambda acc, t: (acc + t,)

    c_all = _allgather8(jnp.broadcast_to(c, (SUBLANES, d)), name="gather_c")[::SUBLANES]
    c16 = jnp.concatenate([c_all, jnp.broadcast_to(c_ctx[None], (N_DEV, d))], axis=0)
    silu_fn = lambda a: (jax.nn.silu(a),)
    cond16 = _whole(silu_fn, [c16], [(c16.shape, F32)], name="silu_c")[0]
    ncol = w_mod.shape[2]
    b_mod_sh = lax.dynamic_slice_in_dim(b_mod, j_me * ncol, ncol, axis=1)
    mod_part = jnp.concatenate([
        _matmul(cond16, w_mod[l], tm=2 * N_DEV, rowvecs=[b_mod_sh[l][None]], epi=add_epi, name="mod_proj")
        for l in range(n_layers)], axis=0)
    mod_all = _allgather8(mod_part, name="gather_mod").reshape(N_CHIPS, 2, n_layers, 2 * N_DEV, ncol)[:, 0]
    mod_all = mod_all.transpose(1, 2, 0, 3).reshape(n_layers, 2 * N_DEV, N_CHIPS * ncol)

    def mods_of(l):
        two = jnp.stack([lax.dynamic_index_in_dim(mod_all[l], d_me, 0, keepdims=False), mod_all[l, N_DEV]])
        return [two[:, None, k * d:(k + 1) * d] for k in range(N_MOD)]

    d_sh, f_all = d // N_CHIPS, w_ff1.shape[2] * N_CHIPS
    f_sh = f_all // N_CHIPS

    def gather_layer(l):
        stacks = []
        for n in BIG:
            sh = p[n][l].astype(BF16)
            at = [j_me.astype(jnp.int32)] + [jnp.zeros((), jnp.int32)] * sh.ndim
            stacks.append(lax.dynamic_update_slice(jnp.zeros((N_CHIPS,) + sh.shape, BF16), sh[None], at))
        st = dict(zip(BIG, _gather_shards(stacks)))
        w = {n: _from_shards(n, st[n]) for n in ("w_in", "s5_w_glu", "w_branch")}
        w["w_out"] = st["w_out"].reshape(d, d)
        w["w_ff2"] = st["w_ff2"].reshape(f_all, d)
        w["w_gate"], w["w_ff1"] = st["w_gate"], st["w_ff1"]
        return w

    g_tn = min(1024, d)
    gate_fwd = dict(tk=d, tn=g_tn, b_blocks=(N_BRANCH * d, (N_CHIPS, None, d_sh, g_tn),
                                             lambda i, j, kk: (0, j // (d // g_tn), 0, j % (d // g_tn))))
    gate_bwd = dict(tb=True, tn=2 * d_sh, tk=d, b_blocks=(d, (2, None, d_sh, d), lambda i, j, kk: (j, kk, 0, 0)))
    gw_tn = min(2048, d)
    gate_wgt = dict(ta=True, tm=d_sh, tn=gw_tn, out_blocks=(
        (N_CHIPS, N_BRANCH, d_sh, d), (None, None, d_sh, gw_tn),
        lambda i, j, kk: (i, j // (d // gw_tn), 0, j % (d // gw_tn))))
    f_tn = min(1024, f_sh)
    ff1_fwd = dict(tn=f_tn, tk=d, b_blocks=(f_all, (None, d, f_tn), lambda i, j, kk: (j // (f_sh // f_tn), 0, j % (f_sh // f_tn))))
    f_tk = min(2048, f_sh)
    fb_tn = min(1024, d)
    ff1_bwd = dict(tb=True, tn=fb_tn, tk=f_tk, b_blocks=(
        d, (None, fb_tn, f_tk), lambda i, j, kk: (kk // (f_sh // f_tk), j, kk % (f_sh // f_tk))))
    fw_tn, fw_tm = min(2048, f_sh), min(1024, d)
    ff1_wgt = dict(ta=True, tm=fw_tm, tn=fw_tn, out_blocks=(
        (N_CHIPS, d, f_sh), (None, fw_tm, fw_tn), lambda i, j, kk: (j // (f_sh // fw_tn), i, j % (f_sh // fw_tn))))

    def pad_rows(flat):
        n = -(-flat.shape[0] // (SUBLANES * LANES)) * SUBLANES * LANES
        return jnp.pad(flat, (0, n - flat.shape[0])).reshape(n // LANES, LANES)

    sh_flat = pad_rows(jnp.concatenate([conv_w.reshape(-1), b_gate.reshape(-1)]))
    sh_all = _allgather8(sh_flat, name="gather_small_w").reshape(N_CHIPS, 2, -1)[:, 0]
    conv_w_full = sh_all[:, :conv_w.size].reshape((N_CHIPS,) + conv_w.shape).transpose(1, 2, 0, 3)
    conv_w_full = conv_w_full.reshape(n_layers, CONV_W, db)
    b_gate_full = sh_all[:, conv_w.size:conv_w.size + b_gate.size].reshape((N_CHIPS,) + b_gate.shape)
    b_gate_full = b_gate_full.transpose(1, 2, 0, 3).reshape(n_layers, 1, N_BRANCH * d)

    cos, sin_signed, perm = _rope_tables(s_len)
    xa = jnp.concatenate([x[0], ctx[0]], axis=0)
    col = lambda a: a[None] if a.ndim == 1 else a
    to_heads = lambda a, nh: a.reshape(r, N_KV_HEADS, nh // N_KV_HEADS, HEAD_DIM).transpose(1, 2, 0, 3)
    from_heads = lambda a: a.transpose(2, 0, 1, 3).reshape(r, -1)
    sink_rows_of = lambda l: jnp.repeat(attn_sink[l].reshape(N_KV_HEADS, qpk, 1), BLOCK, axis=1).reshape(
        N_KV_HEADS, qpk * BLOCK, 1)
    slab = lambda a: a.reshape(r, 2 * ch, LANES)
    flat2 = lambda a: a.reshape(r, 2 * gp)

    saved = []
    for l in range(n_layers):
        w = gather_layer(l)
        shift1, scale1, gate1, shift2, scale2, gate2 = mods_of(l)
        n1g, n2g = col(norm1_g[l]), col(norm2_g[l])
        h = rw(_rms_mod, [xa], [("full", n1g), ("mod", scale1), ("mod", shift1)], [(d, BF16)], name="norm1")[0]
        z = _matmul(h, w["w_in"], name="in_proj")
        o1, o2, o3, o4, o5 = 2 * db, 4 * db, 4 * db + n_q * HEAD_DIM, 4 * db + (n_q + N_KV_HEADS) * HEAD_DIM, \
            4 * db + (n_q + 2 * N_KV_HEADS) * HEAD_DIM
        za, zb, zq, zk, zv, zd = z[:, :o1], z[:, o1:o2], z[:, o2:o3], z[:, o3:o4], z[:, o4:o5], z[:, o5:]
        b_full = jnp.repeat(gmlp_bs[l].T, gw, axis=1)
        gmlp_params = [("full", col(gmlp_ln_g[l])), ("full", col(gmlp_ln_b[l])), ("full", gmlp_ws[l]), ("full", b_full)]
        br_a = rw(_gmlp, [za], gmlp_params, [(db, BF16)], name="gmlp")[0]
        conv_params = (conv_w_full[l], col(conv_b[l]), col(conv_ln_g[l]), col(conv_ln_b[l]))
        br_b, y_conv, yc_conv = _conv_fwd(zb, *conv_params, n_lat=n_lat)
        q4, k3, v3 = to_heads(zq, n_q), to_heads(zk, N_KV_HEADS)[:, 0], to_heads(zv, N_KV_HEADS)[:, 0]
        sink_rows = sink_rows_of(l)
        br_c = from_heads(_attn_fwd(q4, k3, v3, sink_rows, cos, sin_signed, perm, s_len=s_len))
        disc_in = [s5_a_re[l].reshape(2, gp, 1), s5_a_im[l].reshape(2, gp, 1),
                   jnp.broadcast_to(s5_log_step[l][:, :, None], (2, g5, S5_STATE)).reshape(2, gp, 1),
                   s5_b_re[l].reshape(gp, S5_GW), s5_b_im[l].reshape(gp, S5_GW)]
        l_re, l_im, bb_re, bb_im = _whole(
            _s5_discretise, disc_in, [((2, gp, 1), F32)] * 2 + [((2, gp, S5_GW), F32)] * 2, name="s5_disc")
        lbar = jnp.stack([l_re[0], l_im[0], l_re[1], l_im[1]]).reshape(4, ch, LANES)
        bd_of = lambda a: _block_diag(a.reshape(g5, S5_STATE, S5_GW).transpose(0, 2, 1))
        cd_of = lambda a: _block_diag(a.transpose(0, 2, 1))
        bd = [jnp.concatenate([bd_of(bb_re[k]), bd_of(bb_im[k])], axis=1).astype(BF16) for k in range(2)]
        cd = [jnp.concatenate([cd_of(s5_c_re[l, k]), -cd_of(s5_c_im[l, k])], axis=0).astype(BF16) for k in range(2)]
        bu_f = _matmul(zd, bd[0], name="s5_in_f")
        bu_b = _matmul(zd, bd[1], name="s5_in_b")
        st_f, st_b = _s5_scan(slab(bu_f), slab(bu_b), lbar, n_lat=n_lat)
        ys = _matmul(flat2(st_f), cd[0], name="s5_out_f")
        ys = _matmul(flat2(st_b), cd[1], tiles=[ys], epi=add_epi, name="s5_out_b")
        d_skip = col(s5_d[l])
        yg = rw(_s5_act, [ys, zd], [("full", d_skip)], [(db, BF16)], name="s5_act")[0]
        glu_pre = _matmul(yg, w["s5_w_glu"], name="s5_glu_proj")
        br_d = rw(_glu_gate, [glu_pre], [], [(db, BF16)], name="s5_glu")[0]
        branches = (br_a, br_b, br_c, br_d)
        gates = _matmul(h, w["w_gate"], name="gate_proj", **gate_fwd)
        projs = [_matmul(branches[k], w["w_branch"][k], name="branch_proj") for k in range(N_BRANCH)]
        merged = rw(_merge, [gates] + projs, [("full", b_gate_full[l])], [(d, BF16)], name="merge", tm=ROW_TILE // 2)[0]
        o = _matmul(merged, w["w_out"], name="out_proj")
        x1 = rw(_resid, [xa, o], [("mod", gate1)], [(d, F32)], name="resid1")[0]
        h2 = rw(_rms_mod, [x1], [("full", n2g), ("mod", scale2), ("mod", shift2)], [(d, BF16)], name="norm2")[0]
        f1, act = _matmul(h2, w["w_ff1"], out_dtypes=(F32, BF16), name="ff1",
                          epi=lambda acc: (acc, jnp.square(jnp.maximum(acc, 0.0))), **ff1_fwd)
        o_ff = _matmul(act, w["w_ff2"], name="ff2")
        x2 = rw(_resid, [x1, o_ff], [("mod", gate2)], [(d, F32)], name="resid2")[0]
        saved.append(dict(
            w=w, mods=(shift1, scale1, gate1, shift2, scale2, gate2), n1g=n1g, n2g=n2g, xa=xa, h=h, z=z,
            gmlp_params=gmlp_params, conv_params=conv_params, y_conv=y_conv, yc_conv=yc_conv, q4=q4, k3=k3, v3=v3,
            sink_rows=sink_rows, disc_in=disc_in, lbar=lbar, bd=bd, cd=cd, st_f=st_f, st_b=st_b, ys=ys,
            d_skip=d_skip, yg=yg, glu_pre=glu_pre, branches=branches, gates=gates, projs=projs, merged=merged,
            o=o, x1=x1, h2=h2, f1=f1, act=act, o_ff=o_ff))
        xa = x2

    loss_sum, dx_lat, d_final_g = _loss_and_grad(xa, loss_target[0], col(final_g), s_len=s_len)
    loss = lax.psum(loss_sum[0, 0], ("x", "y", "c"))
    dxa = jnp.concatenate([dx_lat, jnp.zeros((n_ctx, d), F32)], axis=0)

    big_grads = [None] * n_layers
    small_grads = [None] * n_layers
    d_mods = [None] * n_layers
    o1, o2, o3, o4, o5 = 2 * db, 4 * db, 4 * db + n_q * HEAD_DIM, 4 * db + (n_q + N_KV_HEADS) * HEAD_DIM, \
        4 * db + (n_q + 2 * N_KV_HEADS) * HEAD_DIM
    for l in reversed(range(n_layers)):
        sv = saved[l]
        w = sv["w"]
        shift1, scale1, gate1, shift2, scale2, gate2 = sv["mods"]
        z = sv["z"]
        za, zb, zd = z[:, :o1], z[:, o1:o2], z[:, o5:]
        bg, sg = {}, {}
        (d_x1, d_off), (d_gate2,) = rwv(_resid, [sv["x1"], sv["o_ff"]], [("mod", gate2)], [dxa],
                                        row_grads=[F32, BF16], name="resid2_b")
        d_f1 = _matmul(d_off, w["w_ff2"], tb=True, tiles=[sv["f1"]], out_dtypes=(BF16,), name="ff2_b",
                       epi=lambda acc, f: (acc * (2.0 * jnp.maximum(f, 0.0)),))
        bg["w_ff2"] = _matmul(sv["act"], d_off, ta=True, name="ff2_w").reshape(N_CHIPS, f_sh, d)
        d_h2 = _matmul(d_f1, w["w_ff1"], name="ff1_b", **ff1_bwd)
        bg["w_ff1"] = _matmul(sv["h2"], d_f1, name="ff1_w", **ff1_wgt)
        (d_x1,), (sg["norm2_g"], d_scale2, d_shift2) = rwv(
            _rms_mod, [sv["x1"]], [("full", sv["n2g"]), ("mod", scale2), ("mod", shift2)], [d_h2],
            row_grads=[F32], adds={0: d_x1}, name="norm2_b")
        (d_xa, d_o), (d_gate1,) = rwv(_resid, [sv["xa"], sv["o"]], [("mod", gate1)], [d_x1],
                                      row_grads=[F32, BF16], name="resid1_b")
        d_merged = _matmul(d_o, w["w_out"], tb=True, name="out_b")
        bg["w_out"] = _matmul(sv["merged"], d_o, ta=True, name="out_w").reshape(N_CHIPS, d_sh, d)
        d_parts, (d_bg,) = rwv(_merge, [sv["gates"]] + sv["projs"], [("full", b_gate_full[l])], [d_merged],
                               row_grads=[BF16] * (1 + N_BRANCH), name="merge_b", tm=ROW_TILE // 4)
        sg["b_gate"] = d_bg.reshape(N_BRANCH, d)
        d_gates, d_projs = d_parts[0], d_parts[1:]
        bg["w_gate"] = _matmul(sv["h"], d_gates, name="gate_w", **gate_wgt)
        d_h = _matmul(d_gates, w["w_gate"], name="gate_b", **gate_bwd)
        d_br = [_matmul(d_projs[k], w["w_branch"][k], tb=True, name="branch_b") for k in range(N_BRANCH)]
        bg["w_branch"] = _to_shards("w_branch", jnp.stack([
            _matmul(sv["branches"][k], d_projs[k], ta=True, name="branch_w") for k in range(N_BRANCH)]))
        (d_glu_pre,), _ = rwv(_glu_gate, [sv["glu_pre"]], [], [d_br[3]], row_grads=[BF16], name="s5_glu_b")
        d_yg = _matmul(d_glu_pre, w["s5_w_glu"], tb=True, name="s5_glu_proj_b")
        bg["s5_w_glu"] = _to_shards("s5_w_glu", _matmul(sv["yg"], d_glu_pre, ta=True, name="s5_glu_proj_w"))
        (d_ys, d_zd), (d_dskip,) = rwv(_s5_act, [sv["ys"], zd], [("full", sv["d_skip"])], [d_yg],
                                       row_grads=[BF16, F32], name="s5_act_b")
        sg["s5_d"] = d_dskip.reshape(db)
        cd, bd = sv["cd"], sv["bd"]
        st2 = [flat2(sv["st_f"]), flat2(sv["st_b"])]
        d_st = [_matmul(d_ys, cd[k], tb=True, name="s5_out_b%d" % k) for k in range(2)]
        d_cd = [_matmul(st2[k], d_ys, ta=True, name="s5_out_w%d" % k) for k in range(2)]
        d_bu_f, d_bu_b, d_lbar = _s5_scan_bwd(slab(d_st[0]), slab(d_st[1]), sv["st_f"], sv["st_b"], sv["lbar"],
                                              n_lat=n_lat)
        d_bu = [flat2(d_bu_f), flat2(d_bu_b)]
        d_zd = _matmul(d_bu[0], bd[0], tb=True, tiles=[d_zd], epi=add_epi, name="s5_in_b0")
        d_zd = _matmul(d_bu[1], bd[1], tb=True, tiles=[d_zd], epi=add_epi, name="s5_in_b1")
        d_bd = [_matmul(zd, d_bu[k], ta=True, name="s5_in_w%d" % k) for k in range(2)]
        blk_c = lambda a: _diag_blocks(a, g5).transpose(0, 2, 1)
        sg["s5_c_re"] = jnp.stack([blk_c(d_cd[k][:gp]) for k in range(2)])
        sg["s5_c_im"] = jnp.stack([-blk_c(d_cd[k][gp:]) for k in range(2)])
        blk_b = lambda a: _diag_blocks(a, g5).transpose(0, 2, 1).reshape(gp, S5_GW)
        d_bb_re = jnp.stack([blk_b(d_bd[k][:, :gp]) for k in range(2)])
        d_bb_im = jnp.stack([blk_b(d_bd[k][:, gp:]) for k in range(2)])
        d_lb = d_lbar.reshape(4, gp, 1)
        disc_ct = [jnp.stack([d_lb[0], d_lb[2]]), jnp.stack([d_lb[1], d_lb[3]]), d_bb_re, d_bb_im]
        d_are, d_aim, d_ls, d_bre, d_bim = _whole_vjp(_s5_discretise, sv["disc_in"], disc_ct, name="s5_disc_b")
        sg["s5_a_re"], sg["s5_a_im"] = d_are.reshape(2, g5, S5_STATE), d_aim.reshape(2, g5, S5_STATE)
        sg["s5_log_step"] = d_ls.reshape(2, g5, S5_STATE).sum(axis=-1)
        sg["s5_b_re"], sg["s5_b_im"] = d_bre.reshape(g5, S5_STATE, S5_GW), d_bim.reshape(g5, S5_STATE, S5_GW)
        d_o4 = to_heads(d_br[2], n_q)
        d_q4, d_k3, d_v3, d_sink_rows = _attn_bwd(sv["q4"], sv["k3"], sv["v3"], sv["sink_rows"], cos, sin_signed,
                                                  perm, d_o4, s_len=s_len)
        sg["attn_sink"] = d_sink_rows.reshape(n_q, BLOCK).sum(axis=-1)
        d_zq = from_heads(d_q4)
        d_zk, d_zv = from_heads(d_k3[:, None]), from_heads(d_v3[:, None])
        cw, cb, clg, clb = sv["conv_params"]
        (d_yc,), (sg["conv_ln_g"], sg["conv_ln_b"]) = rwv(
            _ln_silu, [sv["yc_conv"]], [("full", clg), ("full", clb)], [d_br[1]], row_grads=[F32], name="conv_ln_b")
        d_zb, sg["conv_w"], sg["conv_b"] = _conv_bwd(d_yc, sv["y_conv"], zb, cw, n_lat=n_lat)
        (d_za,), (sg["gmlp_ln_g"], sg["gmlp_ln_b"], sg["gmlp_ws"], d_bfull) = rwv(
            _gmlp, [za], sv["gmlp_params"], [d_br[0]], row_grads=[F32], name="gmlp_b")
        sg["gmlp_bs"] = d_bfull.reshape(CHUNK, GMLP_GROUPS, gw).sum(axis=-1).T
        d_z = jnp.concatenate([d_za, d_zb, d_zq, d_zk, d_zv, d_zd], axis=1).astype(BF16)
        d_h = _matmul(d_z, w["w_in"], tb=True, tiles=[d_h], epi=add_epi, name="in_b")
        bg["w_in"] = _to_shards("w_in", _matmul(sv["h"], d_z, ta=True, name="in_w"))
        (dxa,), (sg["norm1_g"], d_scale1, d_shift1) = rwv(
            _rms_mod, [sv["xa"]], [("full", sv["n1g"]), ("mod", scale1), ("mod", shift1)], [d_h],
            row_grads=[F32], adds={0: d_xa}, name="norm1_b")
        d_mods[l] = jnp.concatenate([d_shift1, d_scale1, d_gate1, d_shift2, d_scale2, d_gate2], axis=-1)[:, 0]
        big_grads[l], small_grads[l] = bg, sg
        saved[l] = None

    grad_x = dxa[:s_len][None]

    dm_loc = jnp.stack(d_mods).reshape(n_layers * 2, N_MOD * d)
    dm_pad = -(-dm_loc.shape[0] // SUBLANES) * SUBLANES
    dm_all = _allgather8(jnp.pad(dm_loc, ((0, dm_pad - dm_loc.shape[0]), (0, 0))), name="gather_dmod")
    dm_all = dm_all.reshape(N_DEV, dm_pad, N_MOD * d)[:, :n_layers * 2].reshape(N_DEV, n_layers, 2, N_MOD * d)
    dm16 = dm_all.transpose(1, 2, 0, 3).reshape(n_layers, 2 * N_DEV, N_MOD * d)
    dm16_sh = lax.dynamic_slice_in_dim(dm16, j_me * ncol, ncol, axis=2)
    col_sum = lambda a: (jnp.sum(a, axis=0, keepdims=True),)
    grads = {}
    grads["w_mod"] = jnp.stack([_matmul(cond16, dm16_sh[l], ta=True, tk=2 * N_DEV, name="mod_w")
                                for l in range(n_layers)])
    grads["b_mod"] = jnp.concatenate([
        _whole(col_sum, [dm16[l]], [((1, N_MOD * d), F32)], name="mod_bias_g")[0] for l in range(n_layers)], axis=0)
    d_cond = _matmul(dm16_sh[0], w_mod[0], tb=True, tm=2 * N_DEV, name="mod_b")
    for l in range(1, n_layers):
        d_cond = _matmul(dm16_sh[l], w_mod[l], tb=True, tm=2 * N_DEV, tiles=[d_cond], epi=add_epi, name="mod_b_acc")
    d_c16 = _whole_vjp(silu_fn, [c16], [d_cond], name="silu_c_b")[0]
    d_cctx_part = jnp.where(ac == 0, d_c16[N_DEV:].sum(axis=0), 0.0)

    for n in BIG:
        grads[n] = [None] * n_layers
    for l in range(n_layers):
        for n, g in zip(BIG, _reduce_scatter([big_grads[l][n] for n in BIG])):
            grads[n][l] = g
    for n in BIG:
        grads[n] = jnp.stack(grads[n])

    small_shapes = {n: small_grads[0][n].shape for n in SMALL}
    for n in SMALL:
        grads[n] = [None] * n_layers
    for l in range(n_layers):
        extra = [d_final_g.reshape(-1), d_cctx_part.reshape(-1)] if l == 0 else []
        flat = pad_rows(jnp.concatenate([small_grads[l][n].reshape(-1) for n in SMALL] + extra))
        red = _allgather8(flat, reduce=True, name="reduce_small").reshape(-1)
        off = 0
        for n in SMALL:
            sz = math.prod(small_shapes[n])
            grads[n][l] = red[off:off + sz].reshape(small_shapes[n])
            off += sz
        if l == 0:
            grads["final_g"] = red[off:off + d]
            grads["c_ctx"] = red[off + d:off + 2 * d]
    for n in SMALL:
        g_full = jnp.stack(grads[n])
        if n in SMALL_SHARDED:
            width = p[n].shape[-1]
            g_full = lax.dynamic_slice_in_dim(g_full, j_me * width, width, axis=g_full.ndim - 1)
        grads[n] = g_full.reshape(p[n].shape)

    delta, new_m, new_v = {}, {}, {}
    large = BIG + ("w_mod",)
    for n in large:
        delta[n], new_m[n], new_v[n] = _adamw_call(p[n], grads[n], mom[n], var[n], name="adamw")
    small = [n for n in WEIGHT_NAMES if n not in large]
    pack = lambda src: pad_rows(jnp.concatenate([src[n].reshape(-1) for n in small]))
    outs = _adamw_call(pack(p), pack(grads), pack(mom), pack(var), name="adamw_small")
    off = 0
    for n in small:
        sz = p[n].size
        delta[n], new_m[n], new_v[n] = (o.reshape(-1)[off:off + sz].reshape(p[n].shape) for o in outs)
        off += sz

    return (loss, grad_x, *[grads[n] for n in WEIGHT_NAMES], *[delta[n] for n in WEIGHT_NAMES],
            *[new_m[n] for n in WEIGHT_NAMES], *[new_v[n] for n in WEIGHT_NAMES])
```
